```python
import jax
import jax.numpy as jnp
from jax import lax
import numpy as np

D_MODEL = 1024
BATCH = 8
SEQ = 2048
DEPTH = 4

GRID_W = 64
CTX_LEN = 256
D_MIX = D_MODEL
N_MIXERS = 4
D_GROUP = D_MIX // N_MIXERS
CONV_W = 3
HEAD_DIM = 64
N_Q_HEADS = D_GROUP // HEAD_DIM
N_KV_HEADS = 2
Q_PER_KV = N_Q_HEADS // N_KV_HEADS
WINDOW = 128
BLOCK = 128
ROPE_THETA = 10000.0
CHUNK = 128
N_SGU_GROUPS = 4
SGU_GROUP_DIM = D_GROUP // N_SGU_GROUPS
POOL_WINDOWS = (2, 4, 8, 16)
POOL_GROUP_DIM = D_GROUP // len(POOL_WINDOWS)
D_FF = 4 * D_MODEL
EPS = 1e-6

A_COLS = 3 * D_GROUP
Q_COLS = N_Q_HEADS * HEAD_DIM
KV_COLS = N_KV_HEADS * HEAD_DIM
B_COLS = Q_COLS + 2 * KV_COLS
C_COLS = 2 * D_GROUP
D_COLS = D_GROUP
B_OFF = A_COLS
KV_OFF = B_OFF + Q_COLS
C_OFF = B_OFF + B_COLS
D_OFF = C_OFF + C_COLS
D_PROJ = D_OFF + D_COLS

kernel_name = "hybrid_parallel_group_diffusion_trunk"


def rms_norm(x, g):
    xf = x.astype(jnp.float32)
    y = xf * lax.rsqrt(jnp.mean(xf * xf, axis=-1, keepdims=True) + EPS)
    return (y * g.astype(jnp.float32)).astype(x.dtype)


def axial_rope_tables(length):
    rows = length // GRID_W
    row = jnp.repeat(jnp.arange(rows), GRID_W).astype(jnp.float32)
    col = jnp.tile(jnp.arange(GRID_W), rows).astype(jnp.float32)
    n_freq = HEAD_DIM // 4
    inv = ROPE_THETA ** (-jnp.arange(n_freq, dtype=jnp.float32) / n_freq)
    ang_r = row[:, None] * inv[None, :]
    ang_c = col[:, None] * inv[None, :]
    ang = jnp.concatenate([ang_r, ang_r, ang_c, ang_c], axis=-1)
    return jnp.cos(ang), jnp.sin(ang)


def apply_rope(x, cos, sin):
    bshape = (cos.shape[0],) + (1,) * (x.ndim - 3) + (cos.shape[1],)
    xs = x.reshape(x.shape[:-1] + (2, 2, HEAD_DIM // 4))
    rot = jnp.stack([-xs[..., 1, :], xs[..., 0, :]], axis=-2).reshape(x.shape)
    return (x * cos.reshape(bshape) + rot * sin.reshape(bshape)).astype(x.dtype)


def short_conv_mixer(p, conv_w):
    h, gate_b, gate_c = jnp.split(p, 3, axis=-1)
    z = gate_c * h
    zp = jnp.pad(z, ((0, 0), (1, 1), (0, 0)))
    y = conv_w[0] * zp[:, :-2] + conv_w[1] * zp[:, 1:-1] + conv_w[2] * zp[:, 2:]
    return gate_b * y


def split_kv(pkv):
    k = pkv[..., :KV_COLS].reshape(pkv.shape[:-1] + (N_KV_HEADS, HEAD_DIM))
    v = pkv[..., KV_COLS:].reshape(pkv.shape[:-1] + (N_KV_HEADS, HEAD_DIM))
    return k, v


def window_attention(q, k, v, k_ctx, v_ctx, sink):
    bsz, length = q.shape[0], q.shape[1]
    nb = length // BLOCK
    qb = q.reshape(bsz, nb, BLOCK, N_KV_HEADS, Q_PER_KV, HEAD_DIM)

    def band(t):
        tb = t.reshape(bsz, nb, BLOCK, N_KV_HEADS, HEAD_DIM)
        tp = jnp.pad(tb, ((0, 0), (1, 1), (0, 0), (0, 0), (0, 0)))
        return jnp.concatenate([tp[:, :-2], tp[:, 1:-1], tp[:, 2:]], axis=2)

    kb, vb = band(k), band(v)
    s_loc = jnp.einsum("bnqkgd,bnjkd->bnkgqj", qb, kb).astype(jnp.float32)
    r = jnp.arange(BLOCK)
    j = jnp.arange(3 * BLOCK)
    blk = jnp.arange(nb)
    rel = j[None, :] - BLOCK - r[:, None]
    kpos = blk[:, None] * BLOCK - BLOCK + j[None, :]
    mask = (jnp.abs(rel) <= WINDOW)[None, :, :] & ((kpos >= 0) & (kpos < length))[:, None, :]
    s_loc = jnp.where(mask[None, :, None, None, :, :], s_loc, -jnp.inf)
    s_ctx = jnp.einsum("bnqkgd,bckd->bnkgqc", qb, k_ctx).astype(jnp.float32)
    s_sink = jnp.broadcast_to(sink.astype(jnp.float32).reshape(1, 1, N_KV_HEADS, Q_PER_KV, 1, 1),
                              s_loc.shape[:-1] + (1,))
    probs = jax.nn.softmax(jnp.concatenate([s_loc, s_ctx, s_sink], axis=-1), axis=-1).astype(v.dtype)
    n_loc = 3 * BLOCK
    n_ctx = k_ctx.shape[1]
    o = (jnp.einsum("bnkgqj,bnjkd->bnqkgd", probs[..., :n_loc], vb)
         + jnp.einsum("bnkgqc,bckd->bnqkgd", probs[..., n_loc:n_loc + n_ctx], v_ctx))
    return o.reshape(bsz, length, Q_COLS)


def context_attention(q, k, v, sink):
    s = jnp.einsum("bqkgd,bckd->bkgqc", q, k).astype(jnp.float32)
    s_sink = jnp.broadcast_to(sink.astype(jnp.float32).reshape(1, N_KV_HEADS, Q_PER_KV, 1, 1),
                              s.shape[:-1] + (1,))
    probs = jax.nn.softmax(jnp.concatenate([s, s_sink], axis=-1), axis=-1).astype(v.dtype)
    o = jnp.einsum("bkgqc,bckd->bqkgd", probs[..., :-1], v)
    return o.reshape(q.shape[0], q.shape[1], Q_COLS)


def chunk_sgu_mixer(p, sgu_norm, w_s, b_s):
    u, v = jnp.split(p, 2, axis=-1)
    v = rms_norm(v, sgu_norm)
    bsz, length = v.shape[0], v.shape[1]
    vc = v.reshape(bsz, length // CHUNK, CHUNK, N_SGU_GROUPS, SGU_GROUP_DIM)
    z = jnp.einsum("gpq,bnqgc->bnpgc", w_s, vc) + b_s.T[:, :, None]
    return u * z.reshape(u.shape)


def centred_mean(x, w):
    length = x.shape[1]
    xf = x.astype(jnp.float32)
    cs = jnp.concatenate([jnp.zeros_like(xf[:, :1]), jnp.cumsum(xf, axis=1)], axis=1)
    t = jnp.arange(length)
    lo = jnp.clip(t - w // 2, 0, length)
    hi = jnp.clip(t + w // 2, 0, length)
    cnt = (hi - lo).astype(jnp.float32)
    return ((cs[:, hi] - cs[:, lo]) / cnt[None, :, None]).astype(x.dtype)


def pool_mixer(p, w_pool, pool_scale):
    groups = jnp.split(p, len(POOL_WINDOWS), axis=-1)
    d = jnp.stack([centred_mean(g, w) - g for g, w in zip(groups, POOL_WINDOWS)], axis=-2)
    y = jnp.einsum("btgc,gcd->btgd", d, w_pool)
    return y.reshape(p.shape) * pool_scale


def local_mixers(p, conv_w, sgu_norm, w_sgu, b_sgu, w_pool, pool_scale):
    y_a = short_conv_mixer(p[..., :A_COLS], conv_w)
    y_c = chunk_sgu_mixer(p[..., C_OFF:D_OFF], sgu_norm, w_sgu, b_sgu)
    y_d = pool_mixer(p[..., D_OFF:], w_pool, pool_scale)
    return y_a, y_c, y_d


def setup_inputs(seed: int = 0) -> dict:
    key = jax.random.key(seed)
    ks = jax.random.split(key, 24)
    f32 = jnp.float32

    def nrm(k, shape, scale):
        return jax.random.normal(k, shape, f32) * scale

    def gain(k, shape):
        return 1.0 + 0.1 * jax.random.normal(k, shape, f32)

    return {
        "x": nrm(ks[0], (BATCH, SEQ, D_MODEL), 1.0),
        "c": nrm(ks[1], (BATCH, D_MODEL), 1.0),
        "ctx": nrm(ks[2], (BATCH, CTX_LEN, D_MODEL), 1.0),
        "c_ctx": nrm(ks[3], (D_MODEL,), 1.0),
        "norm_mix": gain(ks[4], (DEPTH, D_MODEL)),
        "norm_ff": gain(ks[5], (DEPTH, D_MODEL)),
        "w_ada": nrm(ks[6], (DEPTH, D_MODEL, 6 * D_MODEL), 0.5 * D_MODEL ** -0.5),
        "b_ada": nrm(ks[7], (DEPTH, 6 * D_MODEL), 0.01),
        "w_in": nrm(ks[8], (DEPTH, D_MODEL, D_PROJ), D_MODEL ** -0.5),
        "w_out": nrm(ks[9], (DEPTH, D_MIX, D_MODEL), D_MIX ** -0.5),
        "conv_w": nrm(ks[10], (DEPTH, CONV_W, D_GROUP), CONV_W ** -0.5),
        "q_norm": gain(ks[11], (DEPTH, HEAD_DIM)),
        "k_norm": gain(ks[12], (DEPTH, HEAD_DIM)),
        "sink": nrm(ks[13], (DEPTH, N_Q_HEADS), 1.0),
        "sgu_norm": gain(ks[14], (DEPTH, D_GROUP)),
        "w_sgu": nrm(ks[15], (DEPTH, N_SGU_GROUPS, CHUNK, CHUNK), CHUNK ** -0.5),
        "b_sgu": gain(ks[16], (DEPTH, N_SGU_GROUPS, CHUNK)),
        "w_pool": nrm(ks[17], (DEPTH, len(POOL_WINDOWS), POOL_GROUP_DIM, POOL_GROUP_DIM), POOL_GROUP_DIM ** -0.5),
        "pool_scale": gain(ks[18], (DEPTH, D_GROUP)),
        "w_ff1": nrm(ks[19], (DEPTH, D_MODEL, D_FF), D_MODEL ** -0.5),
        "w_ff2": nrm(ks[20], (DEPTH, D_FF, D_MODEL), D_FF ** -0.5),
    }


def reference(x, c, ctx, c_ctx, norm_mix, norm_ff, w_ada, b_ada, w_in, w_out, conv_w, q_norm, k_norm,
              sink, sgu_norm, w_sgu, b_sgu, w_pool, pool_scale, w_ff1, w_ff2):
    length = x.shape[1]
    cos, sin = axial_rope_tables(length)
    silu_c = jax.nn.silu(c)
    silu_cc = jax.nn.silu(c_ctx)
    h_lat, h_ctx = x, ctx
    for l in range(DEPTH):
        last = l == DEPTH - 1
        mod = (silu_c @ w_ada[l] + b_ada[l])[:, None, :]
        sh1, sc1, g1, sh2, sc2, g2 = jnp.split(mod, 6, axis=-1)
        if last:
            modc = silu_cc @ w_ada[l][:, :2 * D_MODEL] + b_ada[l][:2 * D_MODEL]
            csh1, csc1 = jnp.split(modc, 2, axis=-1)
        else:
            modc = silu_cc @ w_ada[l] + b_ada[l]
            csh1, csc1, cg1, csh2, csc2, cg2 = jnp.split(modc, 6, axis=-1)

        a_lat = rms_norm(h_lat, norm_mix[l]) * (1 + sc1) + sh1
        a_ctx = rms_norm(h_ctx, norm_mix[l]) * (1 + csc1) + csh1
        p_lat = a_lat @ w_in[l]
        if last:
            pkv_ctx = a_ctx @ w_in[l][:, KV_OFF:C_OFF]
        else:
            p_ctx = a_ctx @ w_in[l]
            pkv_ctx = p_ctx[..., KV_OFF:C_OFF]
        k_c, v_c = split_kv(pkv_ctx)
        k_c = rms_norm(k_c, k_norm[l])

        q_l = p_lat[..., B_OFF:KV_OFF].reshape(p_lat.shape[:2] + (N_KV_HEADS, Q_PER_KV, HEAD_DIM))
        k_l, v_l = split_kv(p_lat[..., KV_OFF:C_OFF])
        q_l = apply_rope(rms_norm(q_l, q_norm[l]), cos, sin) * (HEAD_DIM ** -0.5)
        k_l = apply_rope(rms_norm(k_l, k_norm[l]), cos, sin)
        y_b = window_attention(q_l, k_l, v_l, k_c, v_c, sink[l])
        y_a, y_c, y_d = local_mixers(p_lat, conv_w[l], sgu_norm[l], w_sgu[l], b_sgu[l], w_pool[l], pool_scale[l])
        y_lat = jnp.concatenate([y_a, y_b, y_c, y_d], axis=-1) @ w_out[l]
        h_lat = h_lat + g1 * y_lat

        if not last:
            q_c = p_ctx[..., B_OFF:KV_OFF].reshape(p_ctx.shape[:2] + (N_KV_HEADS, Q_PER_KV, HEAD_DIM))
            q_c = rms_norm(q_c, q_norm[l]) * (HEAD_DIM ** -0.5)
            yc_b = context_attention(q_c, k_c, v_c, sink[l])
            yc_a, yc_c, yc_d = local_mixers(p_ctx, conv_w[l], sgu_norm[l], w_sgu[l], b_sgu[l], w_pool[l], pool_scale[l])
            y_ctx = jnp.concatenate([yc_a, yc_b, yc_c, yc_d], axis=-1) @ w_out[l]
            h_ctx = h_ctx + cg1 * y_ctx

        f_lat = rms_norm(h_lat, norm_ff[l]) * (1 + sc2) + sh2
        h_lat = h_lat + g2 * (jnp.square(jax.nn.relu(f_lat @ w_ff1[l])) @ w_ff2[l])
        if not last:
            f_ctx = rms_norm(h_ctx, norm_ff[l]) * (1 + csc2) + csh2
            h_ctx = h_ctx + cg2 * (jnp.square(jax.nn.relu(f_ctx @ w_ff1[l])) @ w_ff2[l])
    return h_lat
```

```python
import functools

import jax
import jax.numpy as jnp
import numpy as np
from jax import lax
from jax.experimental import pallas as pl
from jax.experimental.pallas import tpu as pltpu

D_MODEL = 1024
DEPTH = 4
GRID_W = 64
D_GROUP = 256
HEAD_DIM = 64
N_Q_HEADS = 4
N_KV_HEADS = 2
BLOCK = 128
ROPE_THETA = 10000.0
CHUNK = 128
N_SGU_GROUPS = 4
POOL_WINDOWS = (2, 4, 8, 16)
POOL_HALO = 8
D_FF = 4 * D_MODEL
EPS = 1e-6
D_PROJ = 2048

A_H, A_GB, A_GC = 0, 256, 512
Q_OFF, K_OFF, V_OFF = 768, 1024, 1152
KV_COL_BLOCK = K_OFF // 256
C_U, C_V = 1280, 1536
D_OFF = 1792

NEG = -1e30
MOD_ROWS = 16
LAT_TILE = 512
VMEM_LIMIT = 56 * 1024 * 1024

F32 = jnp.float32
BF16 = jnp.bfloat16


def _rms(x, g):
    ms = jnp.mean(x * x, axis=-1, keepdims=True)
    return x * lax.rsqrt(ms + EPS) * g


def _lane_group(shape, width):
    return lax.broadcasted_iota(jnp.int32, shape, len(shape) - 1) // width


def _head_rms(x, g):
    n = x.shape[-1]
    r = lax.broadcasted_iota(jnp.int32, (n, n), 0) // HEAD_DIM
    c = lax.broadcasted_iota(jnp.int32, (n, n), 1) // HEAD_DIM
    ones = jnp.where(r == c, 1.0, 0.0).astype(BF16)
    x2 = x * x
    hi = x2.astype(BF16)
    lo = (x2 - hi.astype(F32)).astype(BF16)
    ms = (jnp.dot(hi, ones, preferred_element_type=F32)
          + jnp.dot(lo, ones, preferred_element_type=F32)) * (1.0 / HEAD_DIM)
    return x * lax.rsqrt(ms + EPS) * g


def _swap16(x):
    n = x.shape[-1]
    lane = lax.broadcasted_iota(jnp.int32, x.shape, x.ndim - 1)
    return jnp.where(lane % 32 < 16, pltpu.roll(x, n - 16, x.ndim - 1), pltpu.roll(x, 16, x.ndim - 1))


def _rope(x, cos, sin_signed):
    parts = []
    for s in range(0, x.shape[-1], 128):
        xs = x[:, s:s + 128]
        parts.append(xs * cos[:, s:s + 128] + _swap16(xs) * sin_signed[:, s:s + 128])
    return parts[0] if len(parts) == 1 else jnp.concatenate(parts, axis=-1)


def _ada_kernel(cc_ref, w_ref, b_ref, o_ref):
    cc = cc_ref[...]
    s = cc * jax.nn.sigmoid(cc)
    o_ref[...] = jnp.dot(s.astype(BF16), w_ref[...].astype(BF16), preferred_element_type=F32) + b_ref[...]


def _ada_call(cc, w_ada, b_ada):
    nblk = 4
    wb = 6 * D_MODEL // nblk
    return pl.pallas_call(
        _ada_kernel,
        out_shape=jax.ShapeDtypeStruct((DEPTH, MOD_ROWS, 6 * D_MODEL), F32),
        grid=(DEPTH, nblk),
        in_specs=[
            pl.BlockSpec((MOD_ROWS, D_MODEL), lambda l, n: (0, 0)),
            pl.BlockSpec((None, D_MODEL, wb), lambda l, n: (l, 0, n)),
            pl.BlockSpec((None, 1, wb), lambda l, n: (l, 0, n)),
        ],
        out_specs=pl.BlockSpec((None, MOD_ROWS, wb), lambda l, n: (l, 0, n)),
        compiler_params=pltpu.CompilerParams(dimension_semantics=("arbitrary", "arbitrary"),
                                             vmem_limit_bytes=VMEM_LIMIT),
        name="ada_mod",
    )(cc, w_ada, b_ada.reshape(DEPTH, 1, 6 * D_MODEL))


def _in_kernel(h_ref, mod_ref, g_ref, w_ref, qn_ref, kn_ref, cos_ref, sin_ref, p_ref, *, rope):
    mod = mod_ref[...]
    sh1 = mod[:, 0:D_MODEL]
    sc1 = mod[:, D_MODEL:2 * D_MODEL]
    a = _rms(h_ref[...], g_ref[...]) * (1.0 + sc1) + sh1
    p = jnp.dot(a.astype(BF16), w_ref[...], preferred_element_type=F32)
    q = _head_rms(p[:, Q_OFF:K_OFF], qn_ref[...])
    k = _head_rms(p[:, K_OFF:V_OFF], kn_ref[...])
    if rope:
        q = _rope(q, cos_ref[...], sin_ref[...])
        k = _rope(k, cos_ref[:, 0:128], sin_ref[:, 0:128])
    p_ref[:, 0:Q_OFF] = p[:, 0:Q_OFF]
    p_ref[:, Q_OFF:K_OFF] = q * (HEAD_DIM ** -0.5)
    p_ref[:, K_OFF:V_OFF] = k
    p_ref[:, V_OFF:D_PROJ] = p[:, V_OFF:D_PROJ]


def _const_spec(shape):
    nd = len(shape)
    return pl.BlockSpec(shape, lambda b, j: (0,) * nd, pipeline_mode=pl.Buffered(1))


def _in_call(h, mods_l, mod_row, g, w_in, qn, kn, cos, sin, *, tile, rope):
    bsz, seq, _ = h.shape
    if mod_row is None:
        mod_map = lambda b, j: (b, 0, 0)
    else:
        mod_map = lambda b, j: (mod_row, 0, 0)
    return pl.pallas_call(
        functools.partial(_in_kernel, rope=rope),
        out_shape=jax.ShapeDtypeStruct((bsz, seq, D_PROJ), F32),
        grid=(bsz, seq // tile),
        in_specs=[
            pl.BlockSpec((None, tile, D_MODEL), lambda b, j: (b, j, 0)),
            pl.BlockSpec((None, 1, 6 * D_MODEL), mod_map),
            _const_spec((1, D_MODEL)),
            _const_spec((D_MODEL, D_PROJ)),
            _const_spec((1, 256)),
            _const_spec((1, 128)),
            pl.BlockSpec((tile, 256), lambda b, j: (j, 0)),
            pl.BlockSpec((tile, 256), lambda b, j: (j, 0)),
        ],
        out_specs=pl.BlockSpec((None, tile, D_PROJ), lambda b, j: (b, j, 0)),
        compiler_params=pltpu.CompilerParams(dimension_semantics=("parallel", "parallel"),
                                             vmem_limit_bytes=VMEM_LIMIT),
        name="in_proj_rope" if rope else "in_proj",
    )(h, mods_l, g, w_in, qn, kn, cos, sin)


def _shift_rows(x, k):
    n = x.shape[0]
    return pltpu.roll(x, k % n, 0)


def _conv_mixer(p_ref, halo_prev, halo_next, conv_ref, tile):
    z = p_ref[:, A_GC:A_GC + 256] * p_ref[:, A_H:A_H + 256]
    z_ext = jnp.concatenate([halo_prev, z, halo_next], axis=0)
    z_prev = _shift_rows(z_ext, 1)[POOL_HALO:POOL_HALO + tile]
    z_next = _shift_rows(z_ext, -1)[POOL_HALO:POOL_HALO + tile]
    cw = conv_ref[...]
    y = cw[0:1] * z_prev + cw[1:2] * z + cw[2:3] * z_next
    return p_ref[:, A_GB:A_GB + 256] * y


def _pool_mixer(p_ref, halo_prev, halo_next, wpool_ref, pscale_ref, tile, seq, tile_start):
    x = p_ref[:, D_OFF:D_OFF + 256]
    x_ext = jnp.concatenate([halo_prev, x, halo_next], axis=0)
    s2 = _shift_rows(x_ext, 1) + x_ext
    s4 = _shift_rows(s2, 1) + _shift_rows(s2, -1)
    s8 = _shift_rows(s4, 2) + _shift_rows(s4, -2)
    s16 = _shift_rows(s8, 4) + _shift_rows(s8, -4)
    grp = _lane_group(x_ext.shape, HEAD_DIM)
    s = jnp.where(grp == 0, s2, jnp.where(grp == 1, s4, jnp.where(grp == 2, s8, s16)))
    s = s[POOL_HALO:POOL_HALO + tile]
    t = tile_start + lax.broadcasted_iota(jnp.int32, (tile, 256), 0)
    half = jnp.left_shift(1, _lane_group((tile, 256), HEAD_DIM))
    cnt = jnp.minimum(t + half, seq) - jnp.maximum(t - half, 0)
    d = s / cnt.astype(F32) - x
    y = jnp.dot(d.astype(BF16), wpool_ref[...], preferred_element_type=F32)
    return y * pscale_ref[...]


def _sgu_mixer(p_ref, sgun_ref, wsgu_ref, bsgu_ref, tile):
    vn = _rms(p_ref[:, C_V:C_V + 256], sgun_ref[...]).astype(BF16)
    grp = _lane_group((CHUNK, 256), HEAD_DIM)
    w = wsgu_ref[...]
    bias = bsgu_ref[...]
    outs = []
    for c in range(tile // CHUNK):
        r = jnp.dot(w, vn[c * CHUNK:(c + 1) * CHUNK], preferred_element_type=F32)
        z = bias
        for g in range(N_SGU_GROUPS):
            z = z + jnp.where(grp == g, r[g * CHUNK:(g + 1) * CHUNK], 0.0)
        outs.append(z)
    z = outs[0] if len(outs) == 1 else jnp.concatenate(outs, axis=0)
    return p_ref[:, C_U:C_U + 256] * z


def _attention_block(qb, segs, sink_ref):
    kv_of_lane = _lane_group((BLOCK, 128), HEAD_DIM)
    q_g = [qb[:, 0:128], qb[:, 128:256]]
    rows = []
    for kv in range(N_KV_HEADS):
        for g in range(2):
            rows.append(jnp.where(kv_of_lane == kv, q_g[g], 0.0))
    q4 = jnp.concatenate(rows, axis=0).astype(BF16)
    rb = lax.broadcasted_iota(jnp.int32, (4 * BLOCK, 1), 0) // BLOCK
    sink = jnp.where(rb == 0, sink_ref[0],
                     jnp.where(rb == 1, sink_ref[1], jnp.where(rb == 2, sink_ref[2], sink_ref[3])))
    scores = []
    m = sink
    for k, _, mask in segs:
        s = lax.dot_general(q4, k, (((1,), (1,)), ((), ())), preferred_element_type=F32)
        if mask is not None:
            s = jnp.where(mask, s, NEG)
        scores.append(s)
        m = jnp.maximum(m, jnp.max(s, axis=-1, keepdims=True))
    denom = jnp.exp(sink - m)
    acc = None
    for s, (_, v, _) in zip(scores, segs):
        pr = jnp.exp(s - m)
        denom = denom + jnp.sum(pr, axis=-1, keepdims=True)
        o = jnp.dot(pr.astype(BF16), v, preferred_element_type=F32)
        acc = o if acc is None else acc + o
    acc = acc / denom
    outs = []
    for g in range(2):
        outs.append(jnp.where(kv_of_lane == 0, acc[g * BLOCK:(g + 1) * BLOCK],
                              acc[(2 + g) * BLOCK:(3 + g) * BLOCK]))
    return jnp.concatenate(outs, axis=-1)


def _mlp(h1, mod, nff_ref, w1_ref, w2_ref, hid_ref):
    sh2 = mod[:, 3 * D_MODEL:4 * D_MODEL]
    sc2 = mod[:, 4 * D_MODEL:5 * D_MODEL]
    g2 = mod[:, 5 * D_MODEL:6 * D_MODEL]
    f = (_rms(h1, nff_ref[...]) * (1.0 + sc2) + sh2).astype(BF16)
    for c in range(0, D_FF, 1024):
        u = jnp.dot(f, w1_ref[:, c:c + 1024], preferred_element_type=F32)
        u = jnp.maximum(u, 0.0)
        hid_ref[:, c:c + 1024] = (u * u).astype(BF16)
    return h1 + g2 * jnp.dot(hid_ref[...], w2_ref[...], preferred_element_type=F32)


def _mix_kernel(*refs, tile, seq, is_ctx, with_mlp):
    if is_ctx:
        (p_ref, h_ref, mod_ref, sink_ref, conv_ref, sgun_ref, wsgu_ref, bsgu_ref, wpool_ref, pscale_ref,
         nff_ref, wout_ref, w1_ref, w2_ref, o_ref, ycat_ref, hid_ref) = refs
    else:
        (p_ref, kvp_ref, kvn_ref, hp_ref, hn_ref, kvc_ref, h_ref, mod_ref, sink_ref, conv_ref, sgun_ref,
         wsgu_ref, bsgu_ref, wpool_ref, pscale_ref, nff_ref, wout_ref, w1_ref, w2_ref, o_ref,
         ycat_ref, hid_ref) = refs
    j = pl.program_id(1)
    n_tiles = seq // tile
    tile_start = j * tile

    if is_ctx:
        zeros = jnp.zeros((POOL_HALO, 256), F32)
        z_prev = z_next = x_prev = x_next = zeros
    else:
        has_prev = j > 0
        has_next = j < n_tiles - 1
        z_prev = jnp.where(has_prev, hp_ref[:, A_GC:A_GC + 256] * hp_ref[:, A_H:A_H + 256], 0.0)
        z_next = jnp.where(has_next, hn_ref[:, A_GC:A_GC + 256] * hn_ref[:, A_H:A_H + 256], 0.0)
        x_prev = jnp.where(has_prev, hp_ref[:, D_OFF:D_OFF + 256], 0.0)
        x_next = jnp.where(has_next, hn_ref[:, D_OFF:D_OFF + 256], 0.0)

    ycat_ref[:, 0:256] = _conv_mixer(p_ref, z_prev, z_next, conv_ref, tile).astype(BF16)
    ycat_ref[:, 512:768] = _sgu_mixer(p_ref, sgun_ref, wsgu_ref, bsgu_ref, tile).astype(BF16)
    ycat_ref[:, 768:1024] = _pool_mixer(p_ref, x_prev, x_next, wpool_ref, pscale_ref,
                                        tile, seq, tile_start).astype(BF16)

    k_tile = p_ref[:, K_OFF:V_OFF].astype(BF16)
    v_tile = p_ref[:, V_OFF:V_OFF + 128].astype(BF16)
    if is_ctx:
        for i in range(tile // BLOCK):
            qb = p_ref[i * BLOCK:(i + 1) * BLOCK, Q_OFF:K_OFF]
            ycat_ref[i * BLOCK:(i + 1) * BLOCK, 256:512] = _attention_block(
                qb, [(k_tile, v_tile, None)], sink_ref).astype(BF16)
    else:
        k_ext = jnp.concatenate([kvp_ref[:, 0:128].astype(BF16), k_tile, kvn_ref[:, 0:128].astype(BF16)], axis=0)
        v_ext = jnp.concatenate([kvp_ref[:, 128:256].astype(BF16), v_tile, kvn_ref[:, 128:256].astype(BF16)], axis=0)
        k_ctx = kvc_ref[:, 0:128].astype(BF16)
        v_ctx = kvc_ref[:, 128:256].astype(BF16)
        r = lax.broadcasted_iota(jnp.int32, (4 * BLOCK, 3 * BLOCK), 0) % BLOCK
        col = lax.broadcasted_iota(jnp.int32, (4 * BLOCK, 3 * BLOCK), 1)
        seg = col // BLOCK
        jj = col % BLOCK
        band_prev = (seg == 0) & (jj >= r)
        band_next = (seg == 2) & (jj <= r)
        nblk = tile // BLOCK
        for i in range(nblk):
            bp = band_prev if i > 0 else band_prev & has_prev
            bn = band_next if i < nblk - 1 else band_next & has_next
            mask = (seg == 1) | bp | bn
            qb = p_ref[i * BLOCK:(i + 1) * BLOCK, Q_OFF:K_OFF]
            segs = [(k_ext[i * BLOCK:(i + 3) * BLOCK], v_ext[i * BLOCK:(i + 3) * BLOCK], mask),
                    (k_ctx, v_ctx, None)]
            ycat_ref[i * BLOCK:(i + 1) * BLOCK, 256:512] = _attention_block(qb, segs, sink_ref).astype(BF16)

    mod = mod_ref[...]
    g1 = mod[:, 2 * D_MODEL:3 * D_MODEL]
    h1 = h_ref[...] + g1 * jnp.dot(ycat_ref[...], wout_ref[...], preferred_element_type=F32)
    if with_mlp:
        h1 = _mlp(h1, mod, nff_ref, w1_ref, w2_ref, hid_ref)
    o_ref[...] = h1


def _mix_call(p, p_ctx, h, mods_l, mod_row, lw, *, tile, is_ctx):
    bsz, seq, _ = h.shape
    n_tiles = seq // tile
    if mod_row is None:
        mod_map = lambda b, j: (b, 0, 0)
    else:
        mod_map = lambda b, j: (mod_row, 0, 0)
    in_specs = [pl.BlockSpec((None, tile, D_PROJ), lambda b, j: (b, j, 0))]
    args = [p]
    if not is_ctx:
        bpt = tile // BLOCK
        hpt = tile // POOL_HALO
        in_specs += [
            pl.BlockSpec((None, BLOCK, 256), lambda b, j: (b, jnp.maximum(j * bpt - 1, 0), KV_COL_BLOCK)),
            pl.BlockSpec((None, BLOCK, 256),
                         lambda b, j: (b, jnp.minimum((j + 1) * bpt, seq // BLOCK - 1), KV_COL_BLOCK)),
            pl.BlockSpec((None, POOL_HALO, D_PROJ), lambda b, j: (b, jnp.maximum(j * hpt - 1, 0), 0)),
            pl.BlockSpec((None, POOL_HALO, D_PROJ),
                         lambda b, j: (b, jnp.minimum((j + 1) * hpt, seq // POOL_HALO - 1), 0)),
            pl.BlockSpec((None, p_ctx.shape[1], 256), lambda b, j: (b, 0, KV_COL_BLOCK)),
        ]
        args += [p, p, p, p, p_ctx]
    in_specs += [
        pl.BlockSpec((None, tile, D_MODEL), lambda b, j: (b, j, 0)),
        pl.BlockSpec((None, 1, 6 * D_MODEL), mod_map),
        pl.BlockSpec(memory_space=pltpu.SMEM),
        _const_spec((3, 256)),
        _const_spec((1, 256)),
        _const_spec((N_SGU_GROUPS * CHUNK, CHUNK)),
        _const_spec((CHUNK, 256)),
        _const_spec((256, 256)),
        _const_spec((1, 256)),
        _const_spec((1, D_MODEL)),
        _const_spec((D_MODEL, D_MODEL)),
        _const_spec((D_MODEL, D_FF)),
        _const_spec((D_FF, D_MODEL)),
    ]
    args += [h, mods_l, lw["sink"], lw["conv_w"], lw["sgu_norm"], lw["w_sgu"], lw["b_sgu"], lw["w_pool"],
             lw["pool_scale"], lw["norm_ff"], lw["w_out"], lw["w_ff1"], lw["w_ff2"]]
    return pl.pallas_call(
        functools.partial(_mix_kernel, tile=tile, seq=seq, is_ctx=is_ctx, with_mlp=True),
        out_shape=jax.ShapeDtypeStruct(h.shape, F32),
        grid=(bsz, n_tiles),
        in_specs=in_specs,
        out_specs=pl.BlockSpec((None, tile, D_MODEL), lambda b, j: (b, j, 0)),
        scratch_shapes=[pltpu.VMEM((tile, D_MODEL), BF16), pltpu.VMEM((tile, D_FF), BF16)],
        compiler_params=pltpu.CompilerParams(dimension_semantics=("parallel", "parallel"),
                                             vmem_limit_bytes=VMEM_LIMIT),
        name="mix_mlp_ctx" if is_ctx else "mix_mlp_lat",
    )(*args)


def _rope_tables(length):
    rows = length // GRID_W
    row = np.repeat(np.arange(rows), GRID_W).astype(np.float32)
    col = np.tile(np.arange(GRID_W), rows).astype(np.float32)
    n_freq = HEAD_DIM // 4
    inv = jnp.asarray(ROPE_THETA, F32) ** (-jnp.arange(n_freq, dtype=F32) / n_freq)
    ang_r = jnp.asarray(row)[:, None] * inv[None, :]
    ang_c = jnp.asarray(col)[:, None] * inv[None, :]
    ang = jnp.concatenate([ang_r, ang_r, ang_c, ang_c], axis=-1)
    sign = jnp.asarray(np.where(np.arange(HEAD_DIM) % 32 < 16, -1.0, 1.0), F32)
    cos = jnp.tile(jnp.cos(ang), (1, N_Q_HEADS))
    sin = jnp.tile(jnp.sin(ang) * sign, (1, N_Q_HEADS))
    return cos, sin


def _head_perm():
    idx = np.arange(256).reshape(N_KV_HEADS, 2, HEAD_DIM)
    return np.transpose(idx, (1, 0, 2)).reshape(-1)


def kernel(x, c, ctx, c_ctx, norm_mix, norm_ff, w_ada, b_ada, w_in, w_out, conv_w, q_norm, k_norm, sink,
           sgu_norm, w_sgu, b_sgu, w_pool, pool_scale, w_ff1, w_ff2):
    bsz, seq, _ = x.shape
    ctx_len = ctx.shape[1]
    assert bsz + 1 <= MOD_ROWS and seq % LAT_TILE == 0 and ctx_len % BLOCK == 0

    cc = jnp.concatenate([c, c_ctx[None, :], jnp.zeros((MOD_ROWS - bsz - 1, D_MODEL), F32)], axis=0)
    mods = _ada_call(cc, w_ada, b_ada).reshape(DEPTH, MOD_ROWS, 1, 6 * D_MODEL)

    perm = _head_perm()
    in_cols = np.concatenate([np.arange(Q_OFF), Q_OFF + perm, np.arange(K_OFF, D_PROJ)])
    out_rows = np.concatenate([np.arange(256), 256 + perm, np.arange(512, D_MODEL)])
    w_in_b = w_in[:, :, in_cols].astype(BF16)
    w_out_b = w_out[:, out_rows, :].astype(BF16)
    w_ff1_b = w_ff1.astype(BF16)
    w_ff2_b = w_ff2.astype(BF16)
    w_sgu_b = w_sgu.reshape(DEPTH, N_SGU_GROUPS * CHUNK, CHUNK).astype(BF16)
    b_sgu_t = jnp.repeat(jnp.swapaxes(b_sgu, 1, 2), HEAD_DIM, axis=2)
    eye = jnp.eye(len(POOL_WINDOWS), dtype=F32)
    w_pool_bd = jnp.einsum("lgcd,gh->lgchd", w_pool, eye).reshape(DEPTH, 256, 256).astype(BF16)
    qn = jnp.tile(q_norm, (1, N_Q_HEADS)).reshape(DEPTH, 1, 256)
    kn = jnp.tile(k_norm, (1, N_KV_HEADS)).reshape(DEPTH, 1, 128)
    cos, sin = _rope_tables(seq)

    h_lat, h_ctx = x, ctx
    for l in range(DEPTH):
        last = l == DEPTH - 1
        lw = dict(sink=sink[l], conv_w=conv_w[l], sgu_norm=sgu_norm[l][None, :], w_sgu=w_sgu_b[l],
                  b_sgu=b_sgu_t[l], w_pool=w_pool_bd[l], pool_scale=pool_scale[l][None, :],
                  norm_ff=norm_ff[l][None, :], w_out=w_out_b[l], w_ff1=w_ff1_b[l], w_ff2=w_ff2_b[l])
        g_mix = norm_mix[l][None, :]
        p_ctx = _in_call(h_ctx, mods[l], bsz, g_mix, w_in_b[l], qn[l], kn[l], cos[:ctx_len], sin[:ctx_len],
                         tile=ctx_len, rope=False)
        p_lat = _in_call(h_lat, mods[l], None, g_mix, w_in_b[l], qn[l], kn[l], cos, sin,
                         tile=LAT_TILE, rope=True)
        h_lat = _mix_call(p_lat, p_ctx, h_lat, mods[l], None, lw, tile=LAT_TILE, is_ctx=False)
        if not last:
            h_ctx = _mix_call(p_ctx, None, h_ctx, mods[l], bsz, lw, tile=ctx_len, is_ctx=True)
    return h_lat
```

```python
import functools

import jax
import jax.numpy as jnp
import numpy as np
from jax import lax
from jax.experimental import pallas as pl
from jax.experimental.pallas import tpu as pltpu

D_MODEL = 1024
DEPTH = 4
GRID_W = 64
HEAD_DIM = 64
N_Q_HEADS = 4
N_KV_HEADS = 2
BLOCK = 128
ROPE_THETA = 10000.0
CHUNK = 128
N_SGU_GROUPS = 4
POOL_WINDOWS = (2, 4, 8, 16)
POOL_HALO = 8
EDGE = 16
D_FF = 4 * D_MODEL
EPS = 1e-6
D_PROJ = 2048

A_H, A_GB, A_GC = 0, 256, 512
Q_OFF, K_OFF, V_OFF = 768, 1024, 1152
C_U, C_V = 1280, 1536
D_OFF = 1792

NEG = -1e30
MOD_ROWS = 16
LAT_TILE = 512
VMEM_LIMIT = 58 * 1024 * 1024

F32 = jnp.float32
BF16 = jnp.bfloat16


def _rms(x, g):
    ms = jnp.mean(x * x, axis=-1, keepdims=True)
    return x * lax.rsqrt(ms + EPS) * g


def _lane_group(shape, width):
    return lax.broadcasted_iota(jnp.int32, shape, len(shape) - 1) // width


def _head_rms(x, g):
    n = x.shape[-1]
    r = lax.broadcasted_iota(jnp.int32, (n, n), 0) // HEAD_DIM
    c = lax.broadcasted_iota(jnp.int32, (n, n), 1) // HEAD_DIM
    ones = jnp.where(r == c, 1.0, 0.0).astype(BF16)
    x2 = x * x
    hi = x2.astype(BF16)
    lo = (x2 - hi.astype(F32)).astype(BF16)
    ms = (jnp.dot(hi, ones, preferred_element_type=F32)
          + jnp.dot(lo, ones, preferred_element_type=F32)) * (1.0 / HEAD_DIM)
    return x * lax.rsqrt(ms + EPS) * g


def _swap_lanes(x, width):
    lane = lax.broadcasted_iota(jnp.int32, x.shape, x.ndim - 1)
    return jnp.where(lane % (2 * width) < width, pltpu.roll(x, 128 - width, x.ndim - 1),
                     pltpu.roll(x, width, x.ndim - 1))


def _rope(x, cs):
    cos, sin = cs[:, 0:128], cs[:, 128:256]
    parts = [x[:, s:s + 128] * cos + _swap_lanes(x[:, s:s + 128], 16) * sin for s in range(0, x.shape[-1], 128)]
    return parts[0] if len(parts) == 1 else jnp.concatenate(parts, axis=-1)


def _modulated_norm(x, g, shift, scale):
    return (_rms(x, g) * (1.0 + scale) + shift).astype(BF16)


def _ada_kernel(cc_ref, w_ref, b_ref, o_ref):
    cc = cc_ref[...]
    s = cc * jax.nn.sigmoid(cc)
    o_ref[...] = jnp.dot(s.astype(BF16), w_ref[...].astype(BF16), preferred_element_type=F32) + b_ref[...]


def _ada_call(cc, w_ada, b_ada):
    nblk = 4
    wb = 6 * D_MODEL // nblk
    return pl.pallas_call(
        _ada_kernel,
        out_shape=jax.ShapeDtypeStruct((DEPTH, MOD_ROWS, 6 * D_MODEL), F32),
        grid=(DEPTH, nblk),
        in_specs=[
            pl.BlockSpec((MOD_ROWS, D_MODEL), lambda l, n: (0, 0)),
            pl.BlockSpec((None, D_MODEL, wb), lambda l, n: (l, 0, n)),
            pl.BlockSpec((None, 1, wb), lambda l, n: (l, 0, n)),
        ],
        out_specs=pl.BlockSpec((None, MOD_ROWS, wb), lambda l, n: (l, 0, n)),
        compiler_params=pltpu.CompilerParams(dimension_semantics=("arbitrary", "arbitrary"),
                                             vmem_limit_bytes=VMEM_LIMIT),
        name="ada_mod",
    )(cc, w_ada, b_ada.reshape(DEPTH, 1, 6 * D_MODEL))


def _shift_rows(x, k):
    n = x.shape[0]
    return pltpu.roll(x, k % n, 0)


def _conv_mixer(p_ref, r0, halo_prev, halo_next, conv_ref, tile):
    rows = pl.ds(r0, tile)
    z = p_ref[rows, A_GC:A_GC + 256] * p_ref[rows, A_H:A_H + 256]
    z_ext = jnp.concatenate([halo_prev, z, halo_next], axis=0)
    z_prev = _shift_rows(z_ext, 1)[POOL_HALO:POOL_HALO + tile]
    z_next = _shift_rows(z_ext, -1)[POOL_HALO:POOL_HALO + tile]
    cw = conv_ref[...]
    y = cw[0:1] * z_prev + cw[1:2] * z + cw[2:3] * z_next
    return p_ref[rows, A_GB:A_GB + 256] * y


def _pool_mixer(p_ref, r0, halo_prev, halo_next, wpool_ref, pscale_ref, tile, seq, tile_start):
    x = p_ref[pl.ds(r0, tile), D_OFF:D_OFF + 256]
    x_ext = jnp.concatenate([halo_prev, x, halo_next], axis=0)
    s2 = _shift_rows(x_ext, 1) + x_ext
    s4 = _shift_rows(s2, 1) + _shift_rows(s2, -1)
    s8 = _shift_rows(s4, 2) + _shift_rows(s4, -2)
    s16 = _shift_rows(s8, 4) + _shift_rows(s8, -4)
    grp = _lane_group(x_ext.shape, HEAD_DIM)
    s = jnp.where(grp == 0, s2, jnp.where(grp == 1, s4, jnp.where(grp == 2, s8, s16)))
    s = s[POOL_HALO:POOL_HALO + tile]
    t = tile_start + lax.broadcasted_iota(jnp.int32, (tile, 256), 0)
    half = jnp.left_shift(1, _lane_group((tile, 256), HEAD_DIM))
    cnt = jnp.minimum(t + half, seq) - jnp.maximum(t - half, 0)
    d = s / cnt.astype(F32) - x
    y = jnp.dot(d.astype(BF16), wpool_ref[...], preferred_element_type=F32)
    return y * pscale_ref[...]


def _sgu_mixer(p_ref, r0, sgun_ref, wsgu_ref, bsgu_ref, tile):
    rows = pl.ds(r0, tile)
    vn = _rms(p_ref[rows, C_V:C_V + 256], sgun_ref[...]).astype(BF16)
    grp = _lane_group((CHUNK, 256), HEAD_DIM)
    w = wsgu_ref[...]
    bias = bsgu_ref[...]
    outs = []
    for c in range(tile // CHUNK):
        r = jnp.dot(w, vn[c * CHUNK:(c + 1) * CHUNK], preferred_element_type=F32)
        z = bias
        for g in range(N_SGU_GROUPS):
            z = z + jnp.where(grp == g, r[g * CHUNK:(g + 1) * CHUNK], 0.0)
        outs.append(z)
    z = outs[0] if len(outs) == 1 else jnp.concatenate(outs, axis=0)
    return p_ref[rows, C_U:C_U + 256] * z


def _attention_block(qb, segs, sink):
    low = lax.broadcasted_iota(jnp.int32, (BLOCK, 128), 1) < HEAD_DIM
    t0, t1 = qb[:, 0:128], qb[:, 128:256]
    rows = [jnp.where(low, t0, 0.0), jnp.where(low, pltpu.roll(t0, HEAD_DIM, 1), 0.0),
            jnp.where(low, 0.0, pltpu.roll(t1, HEAD_DIM, 1)), jnp.where(low, 0.0, t1)]
    q4 = jnp.concatenate(rows, axis=0).astype(BF16)
    rb = lax.broadcasted_iota(jnp.int32, (4 * BLOCK, 1), 0) // BLOCK
    sink_col = jnp.where(rb == 0, sink[0], jnp.where(rb == 1, sink[1], jnp.where(rb == 2, sink[2], sink[3])))
    scores = []
    m = sink_col
    for k, _, mask in segs:
        s = lax.dot_general(q4, k, (((1,), (1,)), ((), ())), preferred_element_type=F32)
        if mask is not None:
            s = jnp.where(mask, s, NEG)
        scores.append(s)
        m = jnp.maximum(m, jnp.max(s, axis=-1, keepdims=True))
    denom = jnp.exp(sink_col - m)
    acc = None
    for s, (_, v, _) in zip(scores, segs):
        pr = jnp.exp(s - m)
        denom = denom + jnp.sum(pr, axis=-1, keepdims=True)
        o = jnp.dot(pr.astype(BF16), v, preferred_element_type=F32)
        acc = o if acc is None else acc + o
    acc = acc / denom
    a = [acc[i * BLOCK:(i + 1) * BLOCK] for i in range(4)]
    out0 = jnp.where(low, a[0], pltpu.roll(a[1], HEAD_DIM, 1))
    out1 = jnp.where(low, pltpu.roll(a[2], HEAD_DIM, 1), a[3])
    return jnp.concatenate([out0, out1], axis=-1)


def _mlp(h1, mod, nff_ref, w1_ref, w2_ref, hid_ref):
    f = _modulated_norm(h1, nff_ref[...], mod[:, 3 * D_MODEL:4 * D_MODEL], mod[:, 4 * D_MODEL:5 * D_MODEL])
    g2 = mod[:, 5 * D_MODEL:6 * D_MODEL]
    for c in range(0, D_FF, 1024):
        u = jnp.dot(f, w1_ref[:, c:c + 1024], preferred_element_type=F32)
        u = jnp.maximum(u, 0.0)
        hid_ref[:, c:c + 1024] = (u * u).astype(BF16)
    return h1 + g2 * jnp.dot(hid_ref[...], w2_ref[...], preferred_element_type=F32)


def _layer_kernel(l_ref, *refs, tile, seq, is_ctx):
    if is_ctx:
        (h_ref, mod_ref, sink_ref, nmix_ref, qn_ref, kn_ref, conv_ref, sgun_ref, wsgu_ref, bsgu_ref, wpool_ref,
         pscale_ref, nff_ref, win_ref, wout_ref, w1_ref, w2_ref, o_ref, kv_ref, p_ref, ycat_ref, hid_ref) = refs
    else:
        (h_ref, hp_ref, hn_ref, cs_ref, csp_ref, csn_ref, kvc_ref, mod_ref, sink_ref, nmix_ref, qn_ref, kn_ref,
         conv_ref, sgun_ref, wsgu_ref, bsgu_ref, wpool_ref, pscale_ref, nff_ref, win_ref, wout_ref, w1_ref,
         w2_ref, o_ref, p_ref, ycat_ref, hid_ref) = refs
    layer = l_ref[0]
    j = pl.program_id(1)
    tile_start = j * tile
    mod = mod_ref[...]
    sh1, sc1, g1 = mod[:, 0:D_MODEL], mod[:, D_MODEL:2 * D_MODEL], mod[:, 2 * D_MODEL:3 * D_MODEL]
    nmix = nmix_ref[...]
    r0 = 0 if is_ctx else EDGE
    rows = pl.ds(r0, tile)

    a_main = _modulated_norm(h_ref[...], nmix, sh1, sc1)
    if is_ctx:
        p_ref[...] = jnp.dot(a_main, win_ref[...], preferred_element_type=F32)
    else:
        has_prev = j > 0
        has_next = j < seq // tile - 1
        a_prev = _modulated_norm(hp_ref[...], nmix, sh1, sc1)
        a_next = _modulated_norm(hn_ref[...], nmix, sh1, sc1)
        a_ext = jnp.concatenate([a_prev[BLOCK - EDGE:BLOCK], a_main, a_next[0:EDGE]], axis=0)
        p_ref[...] = jnp.dot(a_ext, win_ref[...], preferred_element_type=F32)
        a_halo = jnp.concatenate([a_prev, a_next], axis=0)
        kv_halo = jnp.dot(a_halo, win_ref[:, K_OFF:K_OFF + 256], preferred_element_type=F32)
        k_halo = _head_rms(kv_halo[:, 0:128], kn_ref[...])
        k_prev = _rope(k_halo[0:BLOCK], csp_ref[...]).astype(BF16)
        k_next = _rope(k_halo[BLOCK:2 * BLOCK], csn_ref[...]).astype(BF16)
        v_prev = kv_halo[0:BLOCK, 128:256].astype(BF16)
        v_next = kv_halo[BLOCK:2 * BLOCK, 128:256].astype(BF16)

    q = _head_rms(p_ref[rows, Q_OFF:K_OFF], qn_ref[...])
    k = _head_rms(p_ref[rows, K_OFF:V_OFF], kn_ref[...])
    if not is_ctx:
        q = _rope(q, cs_ref[...])
        k = _rope(k, cs_ref[...])
    p_ref[rows, Q_OFF:K_OFF] = q * (HEAD_DIM ** -0.5)
    k_tile = k.astype(BF16)
    v_tile = p_ref[rows, V_OFF:V_OFF + 128].astype(BF16)
    if is_ctx:
        kv_ref[:, 0:128] = k
        kv_ref[:, 128:256] = p_ref[rows, V_OFF:V_OFF + 128]

    if is_ctx:
        zeros = jnp.zeros((POOL_HALO, 256), F32)
        z_prev = z_next = x_prev = x_next = zeros
    else:
        pe = pl.ds(EDGE - POOL_HALO, POOL_HALO)
        ne = pl.ds(EDGE + tile, POOL_HALO)
        z_prev = jnp.where(has_prev, p_ref[pe, A_GC:A_GC + 256] * p_ref[pe, A_H:A_H + 256], 0.0)
        z_next = jnp.where(has_next, p_ref[ne, A_GC:A_GC + 256] * p_ref[ne, A_H:A_H + 256], 0.0)
        x_prev = jnp.where(has_prev, p_ref[pe, D_OFF:D_OFF + 256], 0.0)
        x_next = jnp.where(has_next, p_ref[ne, D_OFF:D_OFF + 256], 0.0)
    ycat_ref[:, 0:256] = _conv_mixer(p_ref, r0, z_prev, z_next, conv_ref, tile).astype(BF16)
    ycat_ref[:, 512:768] = _sgu_mixer(p_ref, r0, sgun_ref, wsgu_ref, bsgu_ref, tile).astype(BF16)
    ycat_ref[:, 768:1024] = _pool_mixer(p_ref, r0, x_prev, x_next, wpool_ref, pscale_ref,
                                        tile, seq, tile_start).astype(BF16)

    sink = [sink_ref[layer, i] for i in range(N_Q_HEADS)]
    nblk = tile // BLOCK
    if is_ctx:
        for i in range(nblk):
            qb = p_ref[pl.ds(r0 + i * BLOCK, BLOCK), Q_OFF:K_OFF]
            ycat_ref[i * BLOCK:(i + 1) * BLOCK, 256:512] = _attention_block(
                qb, [(k_tile, v_tile, None)], sink).astype(BF16)
    else:
        k_ext = jnp.concatenate([k_prev, k_tile, k_next], axis=0)
        v_ext = jnp.concatenate([v_prev, v_tile, v_next], axis=0)
        k_ctx = kvc_ref[:, 0:128].astype(BF16)
        v_ctx = kvc_ref[:, 128:256].astype(BF16)
        r = lax.broadcasted_iota(jnp.int32, (4 * BLOCK, 3 * BLOCK), 0) % BLOCK
        col = lax.broadcasted_iota(jnp.int32, (4 * BLOCK, 3 * BLOCK), 1)
        seg = col // BLOCK
        jj = col % BLOCK
        band_prev = (seg == 0) & (jj >= r)
        band_next = (seg == 2) & (jj <= r)
        for i in range(nblk):
            bp = band_prev if i > 0 else band_prev & has_prev
            bn = band_next if i < nblk - 1 else band_next & has_next
            mask = (seg == 1) | bp | bn
            qb = p_ref[pl.ds(r0 + i * BLOCK, BLOCK), Q_OFF:K_OFF]
            segs = [(k_ext[i * BLOCK:(i + 3) * BLOCK], v_ext[i * BLOCK:(i + 3) * BLOCK], mask),
                    (k_ctx, v_ctx, None)]
            ycat_ref[i * BLOCK:(i + 1) * BLOCK, 256:512] = _attention_block(qb, segs, sink).astype(BF16)

    h1 = h_ref[...] + g1 * jnp.dot(ycat_ref[...], wout_ref[...], preferred_element_type=F32)
    o_ref[...] = _mlp(h1, mod, nff_ref, w1_ref, w2_ref, hid_ref)


def _layer_spec(shape):
    nd = len(shape)
    return pl.BlockSpec((None,) + shape, lambda b, j, l: (l[0],) + (0,) * nd, pipeline_mode=pl.Buffered(1))


def _layer_call(layer, h, kv_ctx, cs, mods, mod_row, lw, *, tile, is_ctx):
    bsz, seq, _ = h.shape
    n_tiles = seq // tile
    if mod_row is None:
        mod_map = lambda b, j, l: (l[0], b, 0, 0)
    else:
        mod_map = lambda b, j, l: (l[0], mod_row, 0, 0)
    in_specs = [pl.BlockSpec((None, tile, D_MODEL), lambda b, j, l: (b, j, 0))]
    args = [h]
    if not is_ctx:
        bpt = tile // BLOCK
        last_blk = seq // BLOCK - 1
        prev_blk = lambda j: jnp.maximum(j * bpt - 1, 0)
        next_blk = lambda j: jnp.minimum((j + 1) * bpt, last_blk)
        in_specs += [
            pl.BlockSpec((None, BLOCK, D_MODEL), lambda b, j, l: (b, prev_blk(j), 0)),
            pl.BlockSpec((None, BLOCK, D_MODEL), lambda b, j, l: (b, next_blk(j), 0)),
            pl.BlockSpec((tile, 256), lambda b, j, l: (j, 0)),
            pl.BlockSpec((BLOCK, 256), lambda b, j, l: (prev_blk(j), 0)),
            pl.BlockSpec((BLOCK, 256), lambda b, j, l: (next_blk(j), 0)),
            pl.BlockSpec((None, kv_ctx.shape[1], 256), lambda b, j, l: (b, 0, 0)),
        ]
        args += [h, h, cs, cs, cs, kv_ctx]
    in_specs += [
        pl.BlockSpec((None, None, 1, 6 * D_MODEL), mod_map),
        pl.BlockSpec(memory_space=pltpu.SMEM),
        _layer_spec((1, D_MODEL)),
        _layer_spec((1, 256)),
        _layer_spec((1, 128)),
        _layer_spec((3, 256)),
        _layer_spec((1, 256)),
        _layer_spec((N_SGU_GROUPS * CHUNK, CHUNK)),
        _layer_spec((CHUNK, 256)),
        _layer_spec((256, 256)),
        _layer_spec((1, 256)),
        _layer_spec((1, D_MODEL)),
        _layer_spec((D_MODEL, D_PROJ)),
        _layer_spec((D_MODEL, D_MODEL)),
        _layer_spec((D_MODEL, D_FF)),
        _layer_spec((D_FF, D_MODEL)),
    ]
    args += [mods, lw["sink"], lw["norm_mix"], lw["q_norm"], lw["k_norm"], lw["conv_w"], lw["sgu_norm"], lw["w_sgu"],
             lw["b_sgu"], lw["w_pool"], lw["pool_scale"], lw["norm_ff"], lw["w_in"], lw["w_out"], lw["w_ff1"],
             lw["w_ff2"]]
    tile_spec = pl.BlockSpec((None, tile, D_MODEL), lambda b, j, l: (b, j, 0))
    if is_ctx:
        out_shape = (jax.ShapeDtypeStruct(h.shape, F32), jax.ShapeDtypeStruct((bsz, seq, 256), F32))
        out_specs = (tile_spec, pl.BlockSpec((None, tile, 256), lambda b, j, l: (b, j, 0)))
        p_rows = tile
    else:
        out_shape = jax.ShapeDtypeStruct(h.shape, F32)
        out_specs = tile_spec
        p_rows = tile + 2 * EDGE
    return pl.pallas_call(
        functools.partial(_layer_kernel, tile=tile, seq=seq, is_ctx=is_ctx),
        out_shape=out_shape,
        grid_spec=pltpu.PrefetchScalarGridSpec(
            num_scalar_prefetch=1,
            grid=(bsz, n_tiles),
            in_specs=in_specs,
            out_specs=out_specs,
            scratch_shapes=[pltpu.VMEM((p_rows, D_PROJ), F32), pltpu.VMEM((tile, D_MODEL), BF16),
                            pltpu.VMEM((tile, D_FF), BF16)],
        ),
        compiler_params=pltpu.CompilerParams(dimension_semantics=("parallel", "parallel"),
                                             vmem_limit_bytes=VMEM_LIMIT),
        name="layer_ctx" if is_ctx else "layer_lat",
    )(layer, *args)


def _ctx_kv_kernel(l_ref, h_ref, mod_ref, nmix_ref, kn_ref, w_ref, kv_ref):
    mod = mod_ref[...]
    a = _modulated_norm(h_ref[...], nmix_ref[...], mod[:, 0:D_MODEL], mod[:, D_MODEL:2 * D_MODEL])
    kv = jnp.dot(a, w_ref[...], preferred_element_type=F32)
    kv_ref[:, 0:128] = _head_rms(kv[:, 0:128], kn_ref[...])
    kv_ref[:, 128:256] = kv[:, 128:256]


def _ctx_kv_call(layer, h, mods, mod_row, lw):
    bsz, seq, _ = h.shape
    return pl.pallas_call(
        _ctx_kv_kernel,
        out_shape=jax.ShapeDtypeStruct((bsz, seq, 256), F32),
        grid_spec=pltpu.PrefetchScalarGridSpec(
            num_scalar_prefetch=1,
            grid=(bsz, 1),
            in_specs=[
                pl.BlockSpec((None, seq, D_MODEL), lambda b, j, l: (b, 0, 0)),
                pl.BlockSpec((None, None, 1, 6 * D_MODEL), lambda b, j, l: (l[0], mod_row, 0, 0)),
                _layer_spec((1, D_MODEL)),
                _layer_spec((1, 128)),
                pl.BlockSpec((None, D_MODEL, 256), lambda b, j, l: (l[0], 0, K_OFF // 256),
                             pipeline_mode=pl.Buffered(1)),
            ],
            out_specs=pl.BlockSpec((None, seq, 256), lambda b, j, l: (b, 0, 0)),
        ),
        compiler_params=pltpu.CompilerParams(dimension_semantics=("parallel", "parallel"),
                                             vmem_limit_bytes=VMEM_LIMIT),
        name="ctx_kv",
    )(layer, h, mods, lw["norm_mix"], lw["k_norm"], lw["w_in"])


def _rope_table(length):
    rows = length // GRID_W
    row = np.repeat(np.arange(rows), GRID_W).astype(np.float32)
    col = np.tile(np.arange(GRID_W), rows).astype(np.float32)
    n_freq = HEAD_DIM // 4
    inv = jnp.asarray(ROPE_THETA, F32) ** (-jnp.arange(n_freq, dtype=F32) / n_freq)
    ang_r = jnp.asarray(row)[:, None] * inv[None, :]
    ang_c = jnp.asarray(col)[:, None] * inv[None, :]
    ang = jnp.concatenate([ang_r, ang_r, ang_c, ang_c], axis=-1)
    sign = jnp.asarray(np.where(np.arange(HEAD_DIM) % 32 < 16, -1.0, 1.0), F32)
    return jnp.concatenate([jnp.tile(jnp.cos(ang), (1, 2)), jnp.tile(jnp.sin(ang) * sign, (1, 2))], axis=-1)


def kernel(x, c, ctx, c_ctx, norm_mix, norm_ff, w_ada, b_ada, w_in, w_out, conv_w, q_norm, k_norm, sink,
           sgu_norm, w_sgu, b_sgu, w_pool, pool_scale, w_ff1, w_ff2):
    bsz, seq, _ = x.shape
    ctx_len = ctx.shape[1]
    assert bsz + 1 <= MOD_ROWS and seq % LAT_TILE == 0 and ctx_len % BLOCK == 0

    cc = jnp.concatenate([c, c_ctx[None, :], jnp.zeros((MOD_ROWS - bsz - 1, D_MODEL), F32)], axis=0)
    mods = _ada_call(cc, w_ada, b_ada).reshape(DEPTH, MOD_ROWS, 1, 6 * D_MODEL)

    eye = jnp.eye(len(POOL_WINDOWS), dtype=F32)
    lw = dict(
        sink=sink,
        norm_mix=norm_mix[:, None, :],
        q_norm=jnp.tile(q_norm, (1, N_Q_HEADS))[:, None, :],
        k_norm=jnp.tile(k_norm, (1, N_KV_HEADS))[:, None, :],
        conv_w=conv_w,
        sgu_norm=sgu_norm[:, None, :],
        w_sgu=w_sgu.reshape(DEPTH, N_SGU_GROUPS * CHUNK, CHUNK).astype(BF16),
        b_sgu=jnp.repeat(jnp.swapaxes(b_sgu, 1, 2), HEAD_DIM, axis=2),
        w_pool=jnp.einsum("lgcd,gh->lgchd", w_pool, eye).reshape(DEPTH, 256, 256).astype(BF16),
        pool_scale=pool_scale[:, None, :],
        norm_ff=norm_ff[:, None, :],
        w_in=w_in.astype(BF16),
        w_out=w_out.astype(BF16),
        w_ff1=w_ff1.astype(BF16),
        w_ff2=w_ff2.astype(BF16),
    )
    cs = _rope_table(seq)

    h_lat, h_ctx = x, ctx
    for l in range(DEPTH):
        layer = jnp.full((1,), l, jnp.int32)
        if l < DEPTH - 1:
            h_ctx_next, kv_ctx = _layer_call(layer, h_ctx, None, None, mods, bsz, lw, tile=ctx_len, is_ctx=True)
        else:
            kv_ctx = _ctx_kv_call(layer, h_ctx, mods, bsz, lw)
        h_lat = _layer_call(layer, h_lat, kv_ctx, cs, mods, None, lw, tile=LAT_TILE, is_ctx=False)
        h_ctx = h_ctx_next
    return h_lat
```

```python
import functools

import jax
import jax.numpy as jnp
import numpy as np
from jax import lax
from jax.experimental import pallas as pl
from jax.experimental.pallas import tpu as pltpu

D_MODEL = 1024
DEPTH = 4
GRID_W = 64
HEAD_DIM = 64
N_Q_HEADS = 4
N_KV_HEADS = 2
BLOCK = 128
ROPE_THETA = 10000.0
CHUNK = 128
N_SGU_GROUPS = 4
POOL_WINDOWS = (2, 4, 8, 16)
POOL_HALO = 8
EDGE = 16
D_FF = 4 * D_MODEL
EPS = 1e-6
D_PROJ = 2048

A_H, A_GB, A_GC = 0, 256, 512
Q_OFF, K_OFF, V_OFF = 768, 1024, 1152
C_U, C_V = 1280, 1536
D_OFF = 1792

NEG = -1e30
MOD_ROWS = 16
LAT_TILE = 512
VMEM_LIMIT = 58 * 1024 * 1024

F32 = jnp.float32
BF16 = jnp.bfloat16


def _rms(x, g):
    ms = jnp.mean(x * x, axis=-1, keepdims=True)
    return x * lax.rsqrt(ms + EPS) * g


def _lane_group(shape, width):
    return lax.broadcasted_iota(jnp.int32, shape, len(shape) - 1) // width


def _head_rms(x, g):
    n = x.shape[-1]
    r = lax.broadcasted_iota(jnp.int32, (n, n), 0) // HEAD_DIM
    c = lax.broadcasted_iota(jnp.int32, (n, n), 1) // HEAD_DIM
    ones = jnp.where(r == c, 1.0, 0.0).astype(BF16)
    x2 = x * x
    hi = x2.astype(BF16)
    lo = (x2 - hi.astype(F32)).astype(BF16)
    ms = (jnp.dot(hi, ones, preferred_element_type=F32)
          + jnp.dot(lo, ones, preferred_element_type=F32)) * (1.0 / HEAD_DIM)
    return x * lax.rsqrt(ms + EPS) * g


def _swap_lanes(x, width):
    lane = lax.broadcasted_iota(jnp.int32, x.shape, x.ndim - 1)
    return jnp.where(lane % (2 * width) < width, pltpu.roll(x, 128 - width, x.ndim - 1),
                     pltpu.roll(x, width, x.ndim - 1))


def _rope(x, cs):
    cos, sin = cs[:, 0:128], cs[:, 128:256]
    parts = [x[:, s:s + 128] * cos + _swap_lanes(x[:, s:s + 128], 16) * sin for s in range(0, x.shape[-1], 128)]
    return parts[0] if len(parts) == 1 else jnp.concatenate(parts, axis=-1)


def _modulated_norm(x, g, shift, scale):
    return (_rms(x, g) * (1.0 + scale) + shift).astype(BF16)


def _ada_kernel(cc_ref, w_ref, b_ref, o_ref):
    cc = cc_ref[...]
    s = cc * jax.nn.sigmoid(cc)
    o_ref[...] = jnp.dot(s.astype(BF16), w_ref[...].astype(BF16), preferred_element_type=F32) + b_ref[...]


def _ada_call(cc, w_ada, b_ada):
    nblk = 4
    wb = 6 * D_MODEL // nblk
    return pl.pallas_call(
        _ada_kernel,
        out_shape=jax.ShapeDtypeStruct((DEPTH, MOD_ROWS, 6 * D_MODEL), F32),
        grid=(DEPTH, nblk),
        in_specs=[
            pl.BlockSpec((MOD_ROWS, D_MODEL), lambda l, n: (0, 0)),
            pl.BlockSpec((None, D_MODEL, wb), lambda l, n: (l, 0, n)),
            pl.BlockSpec((None, 1, wb), lambda l, n: (l, 0, n)),
        ],
        out_specs=pl.BlockSpec((None, MOD_ROWS, wb), lambda l, n: (l, 0, n)),
        compiler_params=pltpu.CompilerParams(dimension_semantics=("arbitrary", "arbitrary"),
                                             vmem_limit_bytes=VMEM_LIMIT),
        name="ada_mod",
    )(cc, w_ada, b_ada.reshape(DEPTH, 1, 6 * D_MODEL))


def _shift_rows(x, k):
    n = x.shape[0]
    return pltpu.roll(x, k % n, 0)


def _conv_mixer(p_ref, r0, halo_prev, halo_next, conv_ref, tile):
    rows = pl.ds(r0, tile)
    z = p_ref[rows, A_GC:A_GC + 256] * p_ref[rows, A_H:A_H + 256]
    z_ext = jnp.concatenate([halo_prev, z, halo_next], axis=0)
    z_prev = _shift_rows(z_ext, 1)[POOL_HALO:POOL_HALO + tile]
    z_next = _shift_rows(z_ext, -1)[POOL_HALO:POOL_HALO + tile]
    cw = conv_ref[...]
    y = cw[0:1] * z_prev + cw[1:2] * z + cw[2:3] * z_next
    return p_ref[rows, A_GB:A_GB + 256] * y


def _pool_mixer(p_ref, r0, halo_prev, halo_next, wpool_ref, pscale_ref, tile, seq, tile_start):
    x = p_ref[pl.ds(r0, tile), D_OFF:D_OFF + 256]
    x_ext = jnp.concatenate([halo_prev, x, halo_next], axis=0)
    s2 = _shift_rows(x_ext, 1) + x_ext
    s4 = _shift_rows(s2, 1) + _shift_rows(s2, -1)
    s8 = _shift_rows(s4, 2) + _shift_rows(s4, -2)
    s16 = _shift_rows(s8, 4) + _shift_rows(s8, -4)
    grp = _lane_group(x_ext.shape, HEAD_DIM)
    s = jnp.where(grp == 0, s2, jnp.where(grp == 1, s4, jnp.where(grp == 2, s8, s16)))
    s = s[POOL_HALO:POOL_HALO + tile]
    t = tile_start + lax.broadcasted_iota(jnp.int32, (tile, 256), 0)
    half = jnp.left_shift(1, _lane_group((tile, 256), HEAD_DIM))
    cnt = jnp.minimum(t + half, seq) - jnp.maximum(t - half, 0)
    d = s / cnt.astype(F32) - x
    y = jnp.dot(d.astype(BF16), wpool_ref[...], preferred_element_type=F32)
    return y * pscale_ref[...]


def _sgu_mixer(p_ref, r0, sgun_ref, wsgu_ref, bsgu_ref, tile):
    rows = pl.ds(r0, tile)
    vn = _rms(p_ref[rows, C_V:C_V + 256], sgun_ref[...]).astype(BF16)
    grp = _lane_group((CHUNK, 256), HEAD_DIM)
    w = wsgu_ref[...]
    bias = bsgu_ref[...]
    outs = []
    for c in range(tile // CHUNK):
        r = jnp.dot(w, vn[c * CHUNK:(c + 1) * CHUNK], preferred_element_type=F32)
        z = bias
        for g in range(N_SGU_GROUPS):
            z = z + jnp.where(grp == g, r[g * CHUNK:(g + 1) * CHUNK], 0.0)
        outs.append(z)
    z = outs[0] if len(outs) == 1 else jnp.concatenate(outs, axis=0)
    return p_ref[rows, C_U:C_U + 256] * z


def _attention_block(qb, segs, sink):
    low = lax.broadcasted_iota(jnp.int32, (BLOCK, 128), 1) < HEAD_DIM
    t0, t1 = qb[:, 0:128], qb[:, 128:256]
    rows = [jnp.where(low, t0, 0.0), jnp.where(low, pltpu.roll(t0, HEAD_DIM, 1), 0.0),
            jnp.where(low, 0.0, pltpu.roll(t1, HEAD_DIM, 1)), jnp.where(low, 0.0, t1)]
    q4 = jnp.concatenate(rows, axis=0).astype(BF16)
    rb = lax.broadcasted_iota(jnp.int32, (4 * BLOCK, 1), 0) // BLOCK
    sink_col = jnp.where(rb == 0, sink[0], jnp.where(rb == 1, sink[1], jnp.where(rb == 2, sink[2], sink[3])))
    scores = []
    m = sink_col
    for k, _, bias in segs:
        s = lax.dot_general(q4, k, (((1,), (1,)), ((), ())), preferred_element_type=F32)
        if bias is not None:
            s = s + bias
        scores.append(s)
        m = jnp.maximum(m, jnp.max(s, axis=-1, keepdims=True))
    denom = jnp.exp(sink_col - m)
    acc = None
    for s, (_, v, _) in zip(scores, segs):
        pr = jnp.exp(s - m)
        denom = denom + jnp.sum(pr, axis=-1, keepdims=True)
        o = jnp.dot(pr.astype(BF16), v, preferred_element_type=F32)
        acc = o if acc is None else acc + o
    acc = acc / denom
    a = [acc[i * BLOCK:(i + 1) * BLOCK] for i in range(4)]
    out0 = jnp.where(low, a[0], pltpu.roll(a[1], HEAD_DIM, 1))
    out1 = jnp.where(low, pltpu.roll(a[2], HEAD_DIM, 1), a[3])
    return jnp.concatenate([out0, out1], axis=-1)


def _layer_kernel(l_ref, *refs, tile, seq, n_total, is_ctx):
    if is_ctx:
        (h_ref, mod_ref, sink_ref, nmix_ref, qn_ref, kn_ref, conv_ref, sgun_ref, wsgu_ref, bsgu_ref, wpool_ref,
         pscale_ref, nff_ref, win_ref, wout_ref, w1_ref, w2_ref, o_ref, kv_ref, p_ref, ycat_ref, hid_ref,
         h1_ref, f_ref) = refs
        modb_ref = mod_ref
        j = pl.program_id(1)
    else:
        (h_ref, hp_ref, hn_ref, cs_ref, csp_ref, csn_ref, kvc_ref, mod_ref, modb_ref, sink_ref, nmix_ref, qn_ref,
         kn_ref, conv_ref, sgun_ref, wsgu_ref, bsgu_ref, wpool_ref, pscale_ref, nff_ref, win_ref, wout_ref,
         w1_ref, w2_ref, o_ref, p_ref, ycat_ref, hid_ref, h1_ref, f_ref, bias_ref) = refs
        step = pl.program_id(0)

        @pl.when(step == 0)
        def _():
            h1_ref[...] = jnp.zeros_like(h1_ref)
            f_ref[...] = jnp.zeros_like(f_ref)
            r = lax.broadcasted_iota(jnp.int32, (4 * BLOCK, 3 * BLOCK), 0) % BLOCK
            col = lax.broadcasted_iota(jnp.int32, (4 * BLOCK, 3 * BLOCK), 1)
            seg = col // BLOCK
            jj = col % BLOCK
            band_prev = (seg == 0) & (jj >= r)
            band_next = (seg == 2) & (jj <= r)
            bias_ref[0] = jnp.where((seg == 1) | band_prev | band_next, 0.0, NEG)
            bias_ref[1] = jnp.where((seg == 1) | band_next, 0.0, NEG)
            bias_ref[2] = jnp.where((seg == 1) | band_prev, 0.0, NEG)

        j = jnp.minimum(step, n_total - 1) % (seq // tile)
        has_prev = j > 0
        has_next = j < seq // tile - 1
    layer = l_ref[0]
    tile_start = j * tile
    mod = mod_ref[...]
    sh1, sc1, g1 = mod[:, 0:D_MODEL], mod[:, D_MODEL:2 * D_MODEL], mod[:, 2 * D_MODEL:3 * D_MODEL]
    r0 = 0 if is_ctx else EDGE
    rows = pl.ds(r0, tile)
    nblk = tile // BLOCK
    st = {}

    def ff1(c):
        u = jnp.dot(f_ref[...], w1_ref[:, c:c + 1024], preferred_element_type=F32)
        u = jnp.maximum(u, 0.0)
        hid_ref[:, c:c + 1024] = (u * u).astype(BF16)

    def ff2(c):
        g2 = modb_ref[:, 5 * D_MODEL + c:5 * D_MODEL + c + 256]
        y = jnp.dot(hid_ref[...], w2_ref[:, c:c + 256], preferred_element_type=F32)
        o_ref[:, c:c + 256] = h1_ref[:, c:c + 256] + g2 * y

    back = [functools.partial(ff1, c) for c in range(0, D_FF, 1024)]
    back += [functools.partial(ff2, c) for c in range(0, D_MODEL, 256)]

    def norm_in():
        nmix = nmix_ref[...]
        a_main = _modulated_norm(h_ref[...], nmix, sh1, sc1)
        if is_ctx:
            st["a_ext"] = a_main
        else:
            a_prev = _modulated_norm(hp_ref[...], nmix, sh1, sc1)
            a_next = _modulated_norm(hn_ref[...], nmix, sh1, sc1)
            st["a_ext"] = jnp.concatenate([a_prev[BLOCK - EDGE:BLOCK], a_main, a_next[0:EDGE]], axis=0)
            st["a_halo"] = jnp.concatenate([a_prev, a_next], axis=0)

    def in_proj():
        p_ref[...] = jnp.dot(st["a_ext"], win_ref[...], preferred_element_type=F32)
        if not is_ctx:
            st["kv_halo"] = jnp.dot(st["a_halo"], win_ref[:, K_OFF:K_OFF + 256], preferred_element_type=F32)

    def qk_norm():
        q = _head_rms(p_ref[rows, Q_OFF:K_OFF], qn_ref[...])
        k = _head_rms(p_ref[rows, K_OFF:V_OFF], kn_ref[...])
        if not is_ctx:
            q = _rope(q, cs_ref[...])
            k = _rope(k, cs_ref[...])
        p_ref[rows, Q_OFF:K_OFF] = q * (HEAD_DIM ** -0.5)
        k_tile = k.astype(BF16)
        v_tile = p_ref[rows, V_OFF:V_OFF + 128].astype(BF16)
        if is_ctx:
            kv_ref[:, 0:128] = k
            kv_ref[:, 128:256] = p_ref[rows, V_OFF:V_OFF + 128]
            st["k_ext"], st["v_ext"] = k_tile, v_tile
        else:
            kv_halo = st["kv_halo"]
            k_halo = _head_rms(kv_halo[:, 0:128], kn_ref[...])
            k_prev = _rope(k_halo[0:BLOCK], csp_ref[...]).astype(BF16)
            k_next = _rope(k_halo[BLOCK:2 * BLOCK], csn_ref[...]).astype(BF16)
            v_prev = kv_halo[0:BLOCK, 128:256].astype(BF16)
            v_next = kv_halo[BLOCK:2 * BLOCK, 128:256].astype(BF16)
            st["k_ext"] = jnp.concatenate([k_prev, k_tile, k_next], axis=0)
            st["v_ext"] = jnp.concatenate([v_prev, v_tile, v_next], axis=0)
            st["k_ctx"] = kvc_ref[:, 0:128].astype(BF16)
            st["v_ctx"] = kvc_ref[:, 128:256].astype(BF16)

    def attention(i):
        sink = [sink_ref[layer, n] for n in range(N_Q_HEADS)]
        qb = p_ref[pl.ds(r0 + i * BLOCK, BLOCK), Q_OFF:K_OFF]
        if is_ctx:
            segs = [(st["k_ext"], st["v_ext"], None)]
        else:
            which = 0
            if i == 0:
                which = jnp.where(has_prev, 0, 1)
            if i == nblk - 1:
                which = jnp.where(has_next, 0, 2)
            segs = [(st["k_ext"][i * BLOCK:(i + 3) * BLOCK], st["v_ext"][i * BLOCK:(i + 3) * BLOCK],
                     bias_ref[which]),
                    (st["k_ctx"], st["v_ctx"], None)]
        ycat_ref[i * BLOCK:(i + 1) * BLOCK, 256:512] = _attention_block(qb, segs, sink).astype(BF16)

    def local_mixers():
        if is_ctx:
            zeros = jnp.zeros((POOL_HALO, 256), F32)
            z_prev = z_next = x_prev = x_next = zeros
        else:
            pe = pl.ds(EDGE - POOL_HALO, POOL_HALO)
            ne = pl.ds(EDGE + tile, POOL_HALO)
            z_prev = jnp.where(has_prev, p_ref[pe, A_GC:A_GC + 256] * p_ref[pe, A_H:A_H + 256], 0.0)
            z_next = jnp.where(has_next, p_ref[ne, A_GC:A_GC + 256] * p_ref[ne, A_H:A_H + 256], 0.0)
            x_prev = jnp.where(has_prev, p_ref[pe, D_OFF:D_OFF + 256], 0.0)
            x_next = jnp.where(has_next, p_ref[ne, D_OFF:D_OFF + 256], 0.0)
        ycat_ref[:, 0:256] = _conv_mixer(p_ref, r0, z_prev, z_next, conv_ref, tile).astype(BF16)
        ycat_ref[:, 512:768] = _sgu_mixer(p_ref, r0, sgun_ref, wsgu_ref, bsgu_ref, tile).astype(BF16)
        ycat_ref[:, 768:1024] = _pool_mixer(p_ref, r0, x_prev, x_next, wpool_ref, pscale_ref,
                                            tile, seq, tile_start).astype(BF16)

    def out_proj():
        h1 = h_ref[...] + g1 * jnp.dot(ycat_ref[...], wout_ref[...], preferred_element_type=F32)
        h1_ref[...] = h1
        f_ref[...] = _modulated_norm(h1, nff_ref[...], mod[:, 3 * D_MODEL:4 * D_MODEL],
                                     mod[:, 4 * D_MODEL:5 * D_MODEL])

    att = [functools.partial(attention, i) for i in range(nblk)]
    if is_ctx:
        order = [norm_in, in_proj, qk_norm] + att + [local_mixers, out_proj] + back
    else:
        assert nblk == 4 and len(back) == 8
        order = [back[0], norm_in, in_proj, back[1], qk_norm, back[2], att[0], back[3], att[1],
                 back[4], att[2], back[5], att[3], back[6], local_mixers, back[7], out_proj]
    for piece in order:
        piece()


def _layer_spec(shape):
    nd = len(shape)
    return pl.BlockSpec((None,) + shape, lambda *g: (g[-1][0],) + (0,) * nd, pipeline_mode=pl.Buffered(1))


def _layer_call(layer, h, kv_ctx, cs, mods, mod_row, lw, *, tile, is_ctx):
    bsz, seq, _ = h.shape
    n_tiles = seq // tile
    n_total = bsz * n_tiles
    if is_ctx:
        grid = (bsz, n_tiles)
        front = lambda g: (g[0], g[1])
        back = front
    else:
        grid = (n_total + 1,)
        front = lambda g: (jnp.minimum(g[0], n_total - 1) // n_tiles, jnp.minimum(g[0], n_total - 1) % n_tiles)
        back = lambda g: (jnp.maximum(g[0] - 1, 0) // n_tiles, jnp.maximum(g[0] - 1, 0) % n_tiles)
    mod_spec = lambda which: pl.BlockSpec(
        (None, None, 1, 6 * D_MODEL),
        lambda *g: (g[-1][0], which(g)[0] if mod_row is None else mod_row, 0, 0))
    in_specs = [pl.BlockSpec((None, tile, D_MODEL), lambda *g: front(g) + (0,))]
    args = [h]
    if not is_ctx:
        bpt = tile // BLOCK
        last_blk = seq // BLOCK - 1
        prev_blk = lambda g: jnp.maximum(front(g)[1] * bpt - 1, 0)
        next_blk = lambda g: jnp.minimum((front(g)[1] + 1) * bpt, last_blk)
        in_specs += [
            pl.BlockSpec((None, BLOCK, D_MODEL), lambda *g: (front(g)[0], prev_blk(g), 0)),
            pl.BlockSpec((None, BLOCK, D_MODEL), lambda *g: (front(g)[0], next_blk(g), 0)),
            pl.BlockSpec((tile, 256), lambda *g: (front(g)[1], 0)),
            pl.BlockSpec((BLOCK, 256), lambda *g: (prev_blk(g), 0)),
            pl.BlockSpec((BLOCK, 256), lambda *g: (next_blk(g), 0)),
            pl.BlockSpec((None, kv_ctx.shape[1], 256), lambda *g: (front(g)[0], 0, 0)),
        ]
        args += [h, h, cs, cs, cs, kv_ctx]
    in_specs.append(mod_spec(front))
    args.append(mods)
    if not is_ctx:
        in_specs.append(mod_spec(back))
        args.append(mods)
    in_specs += [
        pl.BlockSpec(memory_space=pltpu.SMEM),
        _layer_spec((1, D_MODEL)),
        _layer_spec((1, 256)),
        _layer_spec((1, 128)),
        _layer_spec((3, 256)),
        _layer_spec((1, 256)),
        _layer_spec((N_SGU_GROUPS * CHUNK, CHUNK)),
        _layer_spec((CHUNK, 256)),
        _layer_spec((256, 256)),
        _layer_spec((1, 256)),
        _layer_spec((1, D_MODEL)),
        _layer_spec((D_MODEL, D_PROJ)),
        _layer_spec((D_MODEL, D_MODEL)),
        _layer_spec((D_MODEL, D_FF)),
        _layer_spec((D_FF, D_MODEL)),
    ]
    args += [lw["sink"], lw["norm_mix"], lw["q_norm"], lw["k_norm"], lw["conv_w"], lw["sgu_norm"], lw["w_sgu"],
             lw["b_sgu"], lw["w_pool"], lw["pool_scale"], lw["norm_ff"], lw["w_in"], lw["w_out"], lw["w_ff1"],
             lw["w_ff2"]]
    tile_spec = pl.BlockSpec((None, tile, D_MODEL), lambda *g: back(g) + (0,))
    scratch = [pltpu.VMEM((tile if is_ctx else tile + 2 * EDGE, D_PROJ), F32),
               pltpu.VMEM((tile, D_MODEL), BF16),
               pltpu.VMEM((tile, D_FF), BF16),
               pltpu.VMEM((tile, D_MODEL), F32),
               pltpu.VMEM((tile, D_MODEL), BF16)]
    if is_ctx:
        out_shape = (jax.ShapeDtypeStruct(h.shape, F32), jax.ShapeDtypeStruct((bsz, seq, 256), F32))
        out_specs = (tile_spec, pl.BlockSpec((None, tile, 256), lambda *g: back(g) + (0,)))
        semantics = ("parallel", "parallel")
    else:
        out_shape = jax.ShapeDtypeStruct(h.shape, F32)
        out_specs = tile_spec
        scratch.append(pltpu.VMEM((3, 4 * BLOCK, 3 * BLOCK), F32))
        semantics = ("arbitrary",)
    return pl.pallas_call(
        functools.partial(_layer_kernel, tile=tile, seq=seq, n_total=n_total, is_ctx=is_ctx),
        out_shape=out_shape,
        grid_spec=pltpu.PrefetchScalarGridSpec(
            num_scalar_prefetch=1,
            grid=grid,
            in_specs=in_specs,
            out_specs=out_specs,
            scratch_shapes=scratch,
        ),
        compiler_params=pltpu.CompilerParams(dimension_semantics=semantics,
                                             vmem_limit_bytes=VMEM_LIMIT),
        name="layer_ctx" if is_ctx else "layer_lat",
    )(layer, *args)


def _ctx_kv_kernel(l_ref, h_ref, mod_ref, nmix_ref, kn_ref, w_ref, kv_ref):
    mod = mod_ref[...]
    a = _modulated_norm(h_ref[...], nmix_ref[...], mod[:, 0:D_MODEL], mod[:, D_MODEL:2 * D_MODEL])
    kv = jnp.dot(a, w_ref[...], preferred_element_type=F32)
    kv_ref[:, 0:128] = _head_rms(kv[:, 0:128], kn_ref[...])
    kv_ref[:, 128:256] = kv[:, 128:256]


def _ctx_kv_call(layer, h, mods, mod_row, lw):
    bsz, seq, _ = h.shape
    return pl.pallas_call(
        _ctx_kv_kernel,
        out_shape=jax.ShapeDtypeStruct((bsz, seq, 256), F32),
        grid_spec=pltpu.PrefetchScalarGridSpec(
            num_scalar_prefetch=1,
            grid=(bsz, 1),
            in_specs=[
                pl.BlockSpec((None, seq, D_MODEL), lambda b, j, l: (b, 0, 0)),
                pl.BlockSpec((None, None, 1, 6 * D_MODEL), lambda b, j, l: (l[0], mod_row, 0, 0)),
                _layer_spec((1, D_MODEL)),
                _layer_spec((1, 128)),
                pl.BlockSpec((None, D_MODEL, 256), lambda b, j, l: (l[0], 0, K_OFF // 256),
                             pipeline_mode=pl.Buffered(1)),
            ],
            out_specs=pl.BlockSpec((None, seq, 256), lambda b, j, l: (b, 0, 0)),
        ),
        compiler_params=pltpu.CompilerParams(dimension_semantics=("parallel", "parallel"),
                                             vmem_limit_bytes=VMEM_LIMIT),
        name="ctx_kv",
    )(layer, h, mods, lw["norm_mix"], lw["k_norm"], lw["w_in"])


def _rope_table(length):
    rows = length // GRID_W
    row = np.repeat(np.arange(rows), GRID_W).astype(np.float32)
    col = np.tile(np.arange(GRID_W), rows).astype(np.float32)
    n_freq = HEAD_DIM // 4
    inv = jnp.asarray(ROPE_THETA, F32) ** (-jnp.arange(n_freq, dtype=F32) / n_freq)
    ang_r = jnp.asarray(row)[:, None] * inv[None, :]
    ang_c = jnp.asarray(col)[:, None] * inv[None, :]
    ang = jnp.concatenate([ang_r, ang_r, ang_c, ang_c], axis=-1)
    sign = jnp.asarray(np.where(np.arange(HEAD_DIM) % 32 < 16, -1.0, 1.0), F32)
    return jnp.concatenate([jnp.tile(jnp.cos(ang), (1, 2)), jnp.tile(jnp.sin(ang) * sign, (1, 2))], axis=-1)


def kernel(x, c, ctx, c_ctx, norm_mix, norm_ff, w_ada, b_ada, w_in, w_out, conv_w, q_norm, k_norm, sink,
           sgu_norm, w_sgu, b_sgu, w_pool, pool_scale, w_ff1, w_ff2):
    bsz, seq, _ = x.shape
    ctx_len = ctx.shape[1]
    assert bsz + 1 <= MOD_ROWS and seq % LAT_TILE == 0 and ctx_len % BLOCK == 0

    cc = jnp.concatenate([c, c_ctx[None, :], jnp.zeros((MOD_ROWS - bsz - 1, D_MODEL), F32)], axis=0)
    mods = _ada_call(cc, w_ada, b_ada).reshape(DEPTH, MOD_ROWS, 1, 6 * D_MODEL)

    eye = jnp.eye(len(POOL_WINDOWS), dtype=F32)
    lw = dict(
        sink=sink,
        norm_mix=norm_mix[:, None, :],
        q_norm=jnp.tile(q_norm, (1, N_Q_HEADS))[:, None, :],
        k_norm=jnp.tile(k_norm, (1, N_KV_HEADS))[:, None, :],
        conv_w=conv_w,
        sgu_norm=sgu_norm[:, None, :],
        w_sgu=w_sgu.reshape(DEPTH, N_SGU_GROUPS * CHUNK, CHUNK).astype(BF16),
        b_sgu=jnp.repeat(jnp.swapaxes(b_sgu, 1, 2), HEAD_DIM, axis=2),
        w_pool=jnp.einsum("lgcd,gh->lgchd", w_pool, eye).reshape(DEPTH, 256, 256).astype(BF16),
        pool_scale=pool_scale[:, None, :],
        norm_ff=norm_ff[:, None, :],
        w_in=w_in.astype(BF16),
        w_out=w_out.astype(BF16),
        w_ff1=w_ff1.astype(BF16),
        w_ff2=w_ff2.astype(BF16),
    )
    cs = _rope_table(seq)

    h_lat, h_ctx = x, ctx
    for l in range(DEPTH):
        layer = jnp.full((1,), l, jnp.int32)
        if l < DEPTH - 1:
            h_ctx_next, kv_ctx = _layer_call(layer, h_ctx, None, None, mods, bsz, lw, tile=ctx_len, is_ctx=True)
        else:
            kv_ctx = _ctx_kv_call(layer, h_ctx, mods, bsz, lw)
        h_lat = _layer_call(layer, h_lat, kv_ctx, cs, mods, None, lw, tile=LAT_TILE, is_ctx=False)
        h_ctx = h_ctx_next
    return h_lat
```

```python
import functools

import jax
import jax.numpy as jnp
import numpy as np
from jax import lax
from jax.experimental import pallas as pl
from jax.experimental.pallas import tpu as pltpu

D_MODEL = 1024
DEPTH = 4
GRID_W = 64
HEAD_DIM = 64
N_Q_HEADS = 4
N_KV_HEADS = 2
BLOCK = 128
ROPE_THETA = 10000.0
CHUNK = 128
N_SGU_GROUPS = 4
POOL_WINDOWS = (2, 4, 8, 16)
POOL_HALO = 8
EDGE = 16
D_FF = 4 * D_MODEL
EPS = 1e-6
D_PROJ = 2048

A_H, A_GB, A_GC = 0, 256, 512
Q_OFF, K_OFF, V_OFF = 768, 1024, 1152
C_U, C_V = 1280, 1536
D_OFF = 1792

NEG = -1e30
MOD_ROWS = 16
LAT_TILE = 512
VMEM_LIMIT = 58 * 1024 * 1024

F32 = jnp.float32
BF16 = jnp.bfloat16


def _rms(x, g):
    ms = jnp.mean(x * x, axis=-1, keepdims=True)
    return x * lax.rsqrt(ms + EPS) * g


def _lane_group(shape, width):
    return lax.broadcasted_iota(jnp.int32, shape, len(shape) - 1) // width


def _head_rms(x, g):
    n = x.shape[-1]
    r = lax.broadcasted_iota(jnp.int32, (n, n), 0) // HEAD_DIM
    c = lax.broadcasted_iota(jnp.int32, (n, n), 1) // HEAD_DIM
    ones = jnp.where(r == c, 1.0, 0.0).astype(BF16)
    x2 = x * x
    hi = x2.astype(BF16)
    lo = (x2 - hi.astype(F32)).astype(BF16)
    ms = (jnp.dot(hi, ones, preferred_element_type=F32)
          + jnp.dot(lo, ones, preferred_element_type=F32)) * (1.0 / HEAD_DIM)
    return x * lax.rsqrt(ms + EPS) * g


def _swap_lanes(x, width):
    lane = lax.broadcasted_iota(jnp.int32, x.shape, x.ndim - 1)
    return jnp.where(lane % (2 * width) < width, pltpu.roll(x, 128 - width, x.ndim - 1),
                     pltpu.roll(x, width, x.ndim - 1))


def _rope(x, cs):
    cos, sin = cs[:, 0:128], cs[:, 128:256]
    parts = [x[:, s:s + 128] * cos + _swap_lanes(x[:, s:s + 128], 16) * sin for s in range(0, x.shape[-1], 128)]
    return parts[0] if len(parts) == 1 else jnp.concatenate(parts, axis=-1)


def _modulated_norm(x, g, shift, scale):
    return (_rms(x, g) * (1.0 + scale) + shift).astype(BF16)


def _ada_kernel(cc_ref, w_ref, b_ref, o_ref):
    cc = cc_ref[...]
    s = cc * jax.nn.sigmoid(cc)
    o_ref[...] = jnp.dot(s.astype(BF16), w_ref[...].astype(BF16), preferred_element_type=F32) + b_ref[...]


def _ada_call(cc, w_ada, b_ada):
    nblk = 4
    wb = 6 * D_MODEL // nblk
    return pl.pallas_call(
        _ada_kernel,
        out_shape=jax.ShapeDtypeStruct((DEPTH, MOD_ROWS, 6 * D_MODEL), F32),
        grid=(DEPTH, nblk),
        in_specs=[
            pl.BlockSpec((MOD_ROWS, D_MODEL), lambda l, n: (0, 0)),
            pl.BlockSpec((None, D_MODEL, wb), lambda l, n: (l, 0, n)),
            pl.BlockSpec((None, 1, wb), lambda l, n: (l, 0, n)),
        ],
        out_specs=pl.BlockSpec((None, MOD_ROWS, wb), lambda l, n: (l, 0, n)),
        compiler_params=pltpu.CompilerParams(dimension_semantics=("arbitrary", "arbitrary"),
                                             vmem_limit_bytes=VMEM_LIMIT),
        name="ada_mod",
    )(cc, w_ada, b_ada.reshape(DEPTH, 1, 6 * D_MODEL))


def _shift_rows(x, k):
    n = x.shape[0]
    return pltpu.roll(x, k % n, 0)


def _conv_mixer(p_ref, r0, halo_prev, halo_next, conv_ref, tile):
    rows = pl.ds(r0, tile)
    z = p_ref[rows, A_GC:A_GC + 256] * p_ref[rows, A_H:A_H + 256]
    z_ext = jnp.concatenate([halo_prev, z, halo_next], axis=0)
    z_prev = _shift_rows(z_ext, 1)[POOL_HALO:POOL_HALO + tile]
    z_next = _shift_rows(z_ext, -1)[POOL_HALO:POOL_HALO + tile]
    cw = conv_ref[...]
    y = cw[0:1] * z_prev + cw[1:2] * z + cw[2:3] * z_next
    return p_ref[rows, A_GB:A_GB + 256] * y


def _pool_mixer(p_ref, r0, halo_prev, halo_next, wpool_ref, pscale_ref, tile, seq, tile_start):
    x = p_ref[pl.ds(r0, tile), D_OFF:D_OFF + 256]
    x_ext = jnp.concatenate([halo_prev, x, halo_next], axis=0)
    s2 = _shift_rows(x_ext, 1) + x_ext
    s4 = _shift_rows(s2, 1) + _shift_rows(s2, -1)
    s8 = _shift_rows(s4, 2) + _shift_rows(s4, -2)
    s16 = _shift_rows(s8, 4) + _shift_rows(s8, -4)
    grp = _lane_group(x_ext.shape, HEAD_DIM)
    s = jnp.where(grp == 0, s2, jnp.where(grp == 1, s4, jnp.where(grp == 2, s8, s16)))
    s = s[POOL_HALO:POOL_HALO + tile]
    t = tile_start + lax.broadcasted_iota(jnp.int32, (tile, 256), 0)
    half = jnp.left_shift(1, _lane_group((tile, 256), HEAD_DIM))
    cnt = jnp.minimum(t + half, seq) - jnp.maximum(t - half, 0)
    d = s / cnt.astype(F32) - x
    y = jnp.dot(d.astype(BF16), wpool_ref[...], preferred_element_type=F32)
    return y * pscale_ref[...]


def _sgu_mixer(p_ref, r0, sgun_ref, wsgu_ref, bsgu_ref, tile):
    rows = pl.ds(r0, tile)
    vn = _rms(p_ref[rows, C_V:C_V + 256], sgun_ref[...]).astype(BF16)
    grp = _lane_group((CHUNK, 256), HEAD_DIM)
    w = wsgu_ref[...]
    bias = bsgu_ref[...]
    outs = []
    for c in range(tile // CHUNK):
        r = jnp.dot(w, vn[c * CHUNK:(c + 1) * CHUNK], preferred_element_type=F32)
        z = bias
        for g in range(N_SGU_GROUPS):
            z = z + jnp.where(grp == g, r[g * CHUNK:(g + 1) * CHUNK], 0.0)
        outs.append(z)
    z = outs[0] if len(outs) == 1 else jnp.concatenate(outs, axis=0)
    return p_ref[rows, C_U:C_U + 256] * z


def _attention_block(qb, segs, sink):
    low = lax.broadcasted_iota(jnp.int32, (BLOCK, 128), 1) < HEAD_DIM
    t0, t1 = qb[:, 0:128], qb[:, 128:256]
    rows = [jnp.where(low, t0, 0.0), jnp.where(low, pltpu.roll(t0, HEAD_DIM, 1), 0.0),
            jnp.where(low, 0.0, pltpu.roll(t1, HEAD_DIM, 1)), jnp.where(low, 0.0, t1)]
    q4 = jnp.concatenate(rows, axis=0).astype(BF16)
    rb = lax.broadcasted_iota(jnp.int32, (4 * BLOCK, 1), 0) // BLOCK
    sink_col = jnp.where(rb == 0, sink[0], jnp.where(rb == 1, sink[1], jnp.where(rb == 2, sink[2], sink[3])))
    scores = []
    m = sink_col
    for k, _, bias in segs:
        s = lax.dot_general(q4, k, (((1,), (1,)), ((), ())), preferred_element_type=F32)
        if bias is not None:
            s = s + bias
        scores.append(s)
        m = jnp.maximum(m, jnp.max(s, axis=-1, keepdims=True))
    denom = jnp.exp(sink_col - m)
    acc = None
    for s, (_, v, _) in zip(scores, segs):
        pr = jnp.exp(s - m)
        denom = denom + jnp.sum(pr, axis=-1, keepdims=True)
        o = jnp.dot(pr.astype(BF16), v, preferred_element_type=F32)
        acc = o if acc is None else acc + o
    acc = acc / denom
    a = [acc[i * BLOCK:(i + 1) * BLOCK] for i in range(4)]
    out0 = jnp.where(low, a[0], pltpu.roll(a[1], HEAD_DIM, 1))
    out1 = jnp.where(low, pltpu.roll(a[2], HEAD_DIM, 1), a[3])
    return jnp.concatenate([out0, out1], axis=-1)


def _layer_kernel(l_ref, *refs, tile, seq, n_total, is_ctx):
    if is_ctx:
        (h_ref, mod_ref, sink_ref, nmix_ref, qn_ref, kn_ref, conv_ref, sgun_ref, wsgu_ref, bsgu_ref, wpool_ref,
         pscale_ref, nff_ref, win_ref, wout_ref, w1_ref, w2_ref, o_ref, kv_ref, p_ref, ycat_ref, hid_ref,
         h1_ref, f_ref) = refs
        modb_ref = mod_ref
        j = pl.program_id(1)
    else:
        (h_ref, hp_ref, hn_ref, cs_ref, csp_ref, csn_ref, kvc_ref, mod_ref, modb_ref, sink_ref, nmix_ref, qn_ref,
         kn_ref, conv_ref, sgun_ref, wsgu_ref, bsgu_ref, wpool_ref, pscale_ref, nff_ref, win_ref, wout_ref,
         w1_ref, w2_ref, o_ref, p_ref, ycat_ref, hid_ref, h1_ref, f_ref, bias_ref) = refs
        step = pl.program_id(0)

        @pl.when(step == 0)
        def _():
            r = lax.broadcasted_iota(jnp.int32, (4 * BLOCK, 3 * BLOCK), 0) % BLOCK
            col = lax.broadcasted_iota(jnp.int32, (4 * BLOCK, 3 * BLOCK), 1)
            seg = col // BLOCK
            jj = col % BLOCK
            band_prev = (seg == 0) & (jj >= r)
            band_next = (seg == 2) & (jj <= r)
            bias_ref[0] = jnp.where((seg == 1) | band_prev | band_next, 0.0, NEG)
            bias_ref[1] = jnp.where((seg == 1) | band_next, 0.0, NEG)
            bias_ref[2] = jnp.where((seg == 1) | band_prev, 0.0, NEG)

        j = jnp.minimum(step, n_total - 1) % (seq // tile)
        has_prev = j > 0
        has_next = j < seq // tile - 1
    layer = l_ref[0]
    tile_start = j * tile
    mod = mod_ref[...]
    sh1, sc1, g1 = mod[:, 0:D_MODEL], mod[:, D_MODEL:2 * D_MODEL], mod[:, 2 * D_MODEL:3 * D_MODEL]
    r0 = 0 if is_ctx else EDGE
    rows = pl.ds(r0, tile)
    nblk = tile // BLOCK
    st = {}

    def ff1(c):
        u = jnp.dot(f_ref[...], w1_ref[:, c:c + 1024], preferred_element_type=F32)
        u = jnp.maximum(u, 0.0)
        hid_ref[:, c:c + 1024] = (u * u).astype(BF16)

    half = D_MODEL // 2

    def ff2(c, h1_cols=None):
        g2 = modb_ref[:, 5 * D_MODEL + c:5 * D_MODEL + c + half]
        y = jnp.dot(hid_ref[...], w2_ref[:, c:c + half], preferred_element_type=F32)
        o_ref[:, c:c + half] = (h1_ref[:, c:c + half] if h1_cols is None else h1_cols) + g2 * y

    back = [functools.partial(ff1, c) for c in range(0, D_FF, 1024)]
    back += [functools.partial(ff2, c) for c in range(0, D_MODEL, half)]

    def norm_in():
        nmix = nmix_ref[...]
        a_main = _modulated_norm(h_ref[...], nmix, sh1, sc1)
        if is_ctx:
            st["a_ext"] = a_main
        else:
            a_prev = _modulated_norm(hp_ref[...], nmix, sh1, sc1)
            a_next = _modulated_norm(hn_ref[...], nmix, sh1, sc1)
            st["a_ext"] = jnp.concatenate([a_prev[BLOCK - EDGE:BLOCK], a_main, a_next[0:EDGE]], axis=0)
            st["a_halo"] = jnp.concatenate([a_prev, a_next], axis=0)

    def in_proj():
        p_ref[...] = jnp.dot(st["a_ext"], win_ref[...], preferred_element_type=F32)
        if not is_ctx:
            st["kv_halo"] = jnp.dot(st["a_halo"], win_ref[:, K_OFF:K_OFF + 256], preferred_element_type=F32)

    def qk_norm():
        q = _head_rms(p_ref[rows, Q_OFF:K_OFF], qn_ref[...])
        k = _head_rms(p_ref[rows, K_OFF:V_OFF], kn_ref[...])
        if not is_ctx:
            q = _rope(q, cs_ref[...])
            k = _rope(k, cs_ref[...])
        p_ref[rows, Q_OFF:K_OFF] = q * (HEAD_DIM ** -0.5)
        k_tile = k.astype(BF16)
        v_tile = p_ref[rows, V_OFF:V_OFF + 128].astype(BF16)
        if is_ctx:
            kv_ref[:, 0:128] = k
            kv_ref[:, 128:256] = p_ref[rows, V_OFF:V_OFF + 128]
            st["k_ext"], st["v_ext"] = k_tile, v_tile
        else:
            kv_halo = st["kv_halo"]
            k_halo = _head_rms(kv_halo[:, 0:128], kn_ref[...])
            k_prev = _rope(k_halo[0:BLOCK], csp_ref[...]).astype(BF16)
            k_next = _rope(k_halo[BLOCK:2 * BLOCK], csn_ref[...]).astype(BF16)
            v_prev = kv_halo[0:BLOCK, 128:256].astype(BF16)
            v_next = kv_halo[BLOCK:2 * BLOCK, 128:256].astype(BF16)
            st["k_ext"] = jnp.concatenate([k_prev, k_tile, k_next], axis=0)
            st["v_ext"] = jnp.concatenate([v_prev, v_tile, v_next], axis=0)
            st["k_ctx"] = kvc_ref[:, 0:128].astype(BF16)
            st["v_ctx"] = kvc_ref[:, 128:256].astype(BF16)

    def attention(i):
        sink = [sink_ref[layer, n] for n in range(N_Q_HEADS)]
        qb = p_ref[pl.ds(r0 + i * BLOCK, BLOCK), Q_OFF:K_OFF]
        if is_ctx:
            segs = [(st["k_ext"], st["v_ext"], None)]
        else:
            which = 0
            if i == 0:
                which = jnp.where(has_prev, 0, 1)
            if i == nblk - 1:
                which = jnp.where(has_next, 0, 2)
            segs = [(st["k_ext"][i * BLOCK:(i + 3) * BLOCK], st["v_ext"][i * BLOCK:(i + 3) * BLOCK],
                     bias_ref[which]),
                    (st["k_ctx"], st["v_ctx"], None)]
        ycat_ref[i * BLOCK:(i + 1) * BLOCK, 256:512] = _attention_block(qb, segs, sink).astype(BF16)

    def local_mixers():
        if is_ctx:
            zeros = jnp.zeros((POOL_HALO, 256), F32)
            z_prev = z_next = x_prev = x_next = zeros
        else:
            pe = pl.ds(EDGE - POOL_HALO, POOL_HALO)
            ne = pl.ds(EDGE + tile, POOL_HALO)
            z_prev = jnp.where(has_prev, p_ref[pe, A_GC:A_GC + 256] * p_ref[pe, A_H:A_H + 256], 0.0)
            z_next = jnp.where(has_next, p_ref[ne, A_GC:A_GC + 256] * p_ref[ne, A_H:A_H + 256], 0.0)
            x_prev = jnp.where(has_prev, p_ref[pe, D_OFF:D_OFF + 256], 0.0)
            x_next = jnp.where(has_next, p_ref[ne, D_OFF:D_OFF + 256], 0.0)
        ycat_ref[:, 0:256] = _conv_mixer(p_ref, r0, z_prev, z_next, conv_ref, tile).astype(BF16)
        ycat_ref[:, 512:768] = _sgu_mixer(p_ref, r0, sgun_ref, wsgu_ref, bsgu_ref, tile).astype(BF16)
        ycat_ref[:, 768:1024] = _pool_mixer(p_ref, r0, x_prev, x_next, wpool_ref, pscale_ref,
                                            tile, seq, tile_start).astype(BF16)

    def out_proj():
        h1 = h_ref[...] + g1 * jnp.dot(ycat_ref[...], wout_ref[...], preferred_element_type=F32)
        h1_ref[...] = h1
        f_ref[...] = _modulated_norm(h1, nff_ref[...], mod[:, 3 * D_MODEL:4 * D_MODEL],
                                     mod[:, 4 * D_MODEL:5 * D_MODEL])

    att = [functools.partial(attention, i) for i in range(nblk)]
    front = [norm_in, in_proj, qk_norm] + att + [local_mixers, out_proj]

    def run(pieces):
        st.clear()
        for piece in pieces:
            piece()

    if is_ctx:
        run(front + back)
        return

    @pl.when(step == 0)
    def _():
        run(front)

    @pl.when(jnp.logical_and(step > 0, step < n_total))
    def _():
        assert nblk == 4 and len(back) == 6

        def load_h1_tail():
            st["h1_tail"] = h1_ref[:, half:D_MODEL]

        run([back[0], norm_in, in_proj, back[1], qk_norm, back[2], att[0], att[1], back[3], att[2], att[3],
             back[4], local_mixers, load_h1_tail, out_proj, lambda: ff2(half, st["h1_tail"])])

    @pl.when(step == n_total)
    def _():
        run(back)


def _layer_spec(shape):
    nd = len(shape)
    return pl.BlockSpec((None,) + shape, lambda *g: (g[-1][0],) + (0,) * nd, pipeline_mode=pl.Buffered(1))


def _layer_call(layer, h, kv_ctx, cs, mods, mod_row, lw, *, tile, is_ctx):
    bsz, seq, _ = h.shape
    n_tiles = seq // tile
    n_total = bsz * n_tiles
    if is_ctx:
        grid = (bsz, n_tiles)
        front = lambda g: (g[0], g[1])
        back = front
    else:
        grid = (n_total + 1,)
        front = lambda g: (jnp.minimum(g[0], n_total - 1) // n_tiles, jnp.minimum(g[0], n_total - 1) % n_tiles)
        back = lambda g: (jnp.maximum(g[0] - 1, 0) // n_tiles, jnp.maximum(g[0] - 1, 0) % n_tiles)
    mod_spec = lambda which: pl.BlockSpec(
        (None, None, 1, 6 * D_MODEL),
        lambda *g: (g[-1][0], which(g)[0] if mod_row is None else mod_row, 0, 0))
    in_specs = [pl.BlockSpec((None, tile, D_MODEL), lambda *g: front(g) + (0,))]
    args = [h]
    if not is_ctx:
        bpt = tile // BLOCK
        last_blk = seq // BLOCK - 1
        prev_blk = lambda g: jnp.maximum(front(g)[1] * bpt - 1, 0)
        next_blk = lambda g: jnp.minimum((front(g)[1] + 1) * bpt, last_blk)
        in_specs += [
            pl.BlockSpec((None, BLOCK, D_MODEL), lambda *g: (front(g)[0], prev_blk(g), 0)),
            pl.BlockSpec((None, BLOCK, D_MODEL), lambda *g: (front(g)[0], next_blk(g), 0)),
            pl.BlockSpec((tile, 256), lambda *g: (front(g)[1], 0)),
            pl.BlockSpec((BLOCK, 256), lambda *g: (prev_blk(g), 0)),
            pl.BlockSpec((BLOCK, 256), lambda *g: (next_blk(g), 0)),
            pl.BlockSpec((None, kv_ctx.shape[1], 256), lambda *g: (front(g)[0], 0, 0)),
        ]
        args += [h, h, cs, cs, cs, kv_ctx]
    in_specs.append(mod_spec(front))
    args.append(mods)
    if not is_ctx:
        in_specs.append(mod_spec(back))
        args.append(mods)
    in_specs += [
        pl.BlockSpec(memory_space=pltpu.SMEM),
        _layer_spec((1, D_MODEL)),
        _layer_spec((1, 256)),
        _layer_spec((1, 128)),
        _layer_spec((3, 256)),
        _layer_spec((1, 256)),
        _layer_spec((N_SGU_GROUPS * CHUNK, CHUNK)),
        _layer_spec((CHUNK, 256)),
        _layer_spec((256, 256)),
        _layer_spec((1, 256)),
        _layer_spec((1, D_MODEL)),
        _layer_spec((D_MODEL, D_PROJ)),
        _layer_spec((D_MODEL, D_MODEL)),
        _layer_spec((D_MODEL, D_FF)),
        _layer_spec((D_FF, D_MODEL)),
    ]
    args += [lw["sink"], lw["norm_mix"], lw["q_norm"], lw["k_norm"], lw["conv_w"], lw["sgu_norm"], lw["w_sgu"],
             lw["b_sgu"], lw["w_pool"], lw["pool_scale"], lw["norm_ff"], lw["w_in"], lw["w_out"], lw["w_ff1"],
             lw["w_ff2"]]
    tile_spec = pl.BlockSpec((None, tile, D_MODEL), lambda *g: back(g) + (0,))
    scratch = [pltpu.VMEM((tile if is_ctx else tile + 2 * EDGE, D_PROJ), F32),
               pltpu.VMEM((tile, D_MODEL), BF16),
               pltpu.VMEM((tile, D_FF), BF16),
               pltpu.VMEM((tile, D_MODEL), F32),
               pltpu.VMEM((tile, D_MODEL), BF16)]
    if is_ctx:
        out_shape = (jax.ShapeDtypeStruct(h.shape, F32), jax.ShapeDtypeStruct((bsz, seq, 256), F32))
        out_specs = (tile_spec, pl.BlockSpec((None, tile, 256), lambda *g: back(g) + (0,)))
        semantics = ("parallel", "parallel")
    else:
        out_shape = jax.ShapeDtypeStruct(h.shape, F32)
        out_specs = tile_spec
        scratch.append(pltpu.VMEM((3, 4 * BLOCK, 3 * BLOCK), F32))
        semantics = ("arbitrary",)
    return pl.pallas_call(
        functools.partial(_layer_kernel, tile=tile, seq=seq, n_total=n_total, is_ctx=is_ctx),
        out_shape=out_shape,
        grid_spec=pltpu.PrefetchScalarGridSpec(
            num_scalar_prefetch=1,
            grid=grid,
            in_specs=in_specs,
            out_specs=out_specs,
            scratch_shapes=scratch,
        ),
        compiler_params=pltpu.CompilerParams(dimension_semantics=semantics,
                                             vmem_limit_bytes=VMEM_LIMIT),
        name="layer_ctx" if is_ctx else "layer_lat",
    )(layer, *args)


def _ctx_kv_kernel(l_ref, h_ref, mod_ref, nmix_ref, kn_ref, w_ref, kv_ref):
    mod = mod_ref[...]
    a = _modulated_norm(h_ref[...], nmix_ref[...], mod[:, 0:D_MODEL], mod[:, D_MODEL:2 * D_MODEL])
    kv = jnp.dot(a, w_ref[...], preferred_element_type=F32)
    kv_ref[:, 0:128] = _head_rms(kv[:, 0:128], kn_ref[...])
    kv_ref[:, 128:256] = kv[:, 128:256]


def _ctx_kv_call(layer, h, mods, mod_row, lw):
    bsz, seq, _ = h.shape
    return pl.pallas_call(
        _ctx_kv_kernel,
        out_shape=jax.ShapeDtypeStruct((bsz, seq, 256), F32),
        grid_spec=pltpu.PrefetchScalarGridSpec(
            num_scalar_prefetch=1,
            grid=(bsz, 1),
            in_specs=[
                pl.BlockSpec((None, seq, D_MODEL), lambda b, j, l: (b, 0, 0)),
                pl.BlockSpec((None, None, 1, 6 * D_MODEL), lambda b, j, l: (l[0], mod_row, 0, 0)),
                _layer_spec((1, D_MODEL)),
                _layer_spec((1, 128)),
                pl.BlockSpec((None, D_MODEL, 256), lambda b, j, l: (l[0], 0, K_OFF // 256),
                             pipeline_mode=pl.Buffered(1)),
            ],
            out_specs=pl.BlockSpec((None, seq, 256), lambda b, j, l: (b, 0, 0)),
        ),
        compiler_params=pltpu.CompilerParams(dimension_semantics=("parallel", "parallel"),
                                             vmem_limit_bytes=VMEM_LIMIT),
        name="ctx_kv",
    )(layer, h, mods, lw["norm_mix"], lw["k_norm"], lw["w_in"])


def _rope_table(length):
    rows = length // GRID_W
    row = np.repeat(np.arange(rows), GRID_W).astype(np.float32)
    col = np.tile(np.arange(GRID_W), rows).astype(np.float32)
    n_freq = HEAD_DIM // 4
    inv = jnp.asarray(ROPE_THETA, F32) ** (-jnp.arange(n_freq, dtype=F32) / n_freq)
    ang_r = jnp.asarray(row)[:, None] * inv[None, :]
    ang_c = jnp.asarray(col)[:, None] * inv[None, :]
    ang = jnp.concatenate([ang_r, ang_r, ang_c, ang_c], axis=-1)
    sign = jnp.asarray(np.where(np.arange(HEAD_DIM) % 32 < 16, -1.0, 1.0), F32)
    return jnp.concatenate([jnp.tile(jnp.cos(ang), (1, 2)), jnp.tile(jnp.sin(ang) * sign, (1, 2))], axis=-1)


def kernel(x, c, ctx, c_ctx, norm_mix, norm_ff, w_ada, b_ada, w_in, w_out, conv_w, q_norm, k_norm, sink,
           sgu_norm, w_sgu, b_sgu, w_pool, pool_scale, w_ff1, w_ff2):
    bsz, seq, _ = x.shape
    ctx_len = ctx.shape[1]
    assert bsz + 1 <= MOD_ROWS and seq % LAT_TILE == 0 and ctx_len % BLOCK == 0

    cc = jnp.concatenate([c, c_ctx[None, :], jnp.zeros((MOD_ROWS - bsz - 1, D_MODEL), F32)], axis=0)
    mods = _ada_call(cc, w_ada, b_ada).reshape(DEPTH, MOD_ROWS, 1, 6 * D_MODEL)

    eye = jnp.eye(len(POOL_WINDOWS), dtype=F32)
    lw = dict(
        sink=sink,
        norm_mix=norm_mix[:, None, :],
        q_norm=jnp.tile(q_norm, (1, N_Q_HEADS))[:, None, :],
        k_norm=jnp.tile(k_norm, (1, N_KV_HEADS))[:, None, :],
        conv_w=conv_w,
        sgu_norm=sgu_norm[:, None, :],
        w_sgu=w_sgu.reshape(DEPTH, N_SGU_GROUPS * CHUNK, CHUNK).astype(BF16),
        b_sgu=jnp.repeat(jnp.swapaxes(b_sgu, 1, 2), HEAD_DIM, axis=2),
        w_pool=jnp.einsum("lgcd,gh->lgchd", w_pool, eye).reshape(DEPTH, 256, 256).astype(BF16),
        pool_scale=pool_scale[:, None, :],
        norm_ff=norm_ff[:, None, :],
        w_in=w_in.astype(BF16),
        w_out=w_out.astype(BF16),
        w_ff1=w_ff1.astype(BF16),
        w_ff2=w_ff2.astype(BF16),
    )
    cs = _rope_table(seq)

    h_lat, h_ctx = x, ctx
    for l in range(DEPTH):
        layer = jnp.full((1,), l, jnp.int32)
        if l < DEPTH - 1:
            h_ctx_next, kv_ctx = _layer_call(layer, h_ctx, None, None, mods, bsz, lw, tile=ctx_len, is_ctx=True)
        else:
            kv_ctx = _ctx_kv_call(layer, h_ctx, mods, bsz, lw)
        h_lat = _layer_call(layer, h_lat, kv_ctx, cs, mods, None, lw, tile=LAT_TILE, is_ctx=False)
        h_ctx = h_ctx_next
    return h_lat
```

```python
import functools

import jax
import jax.numpy as jnp
import numpy as np
from jax import lax
from jax.experimental import pallas as pl
from jax.experimental.pallas import tpu as pltpu

D_MODEL = 1024
DEPTH = 4
GRID_W = 64
HEAD_DIM = 64
N_Q_HEADS = 4
N_KV_HEADS = 2
BLOCK = 128
ROPE_THETA = 10000.0
CHUNK = 128
N_SGU_GROUPS = 4
POOL_WINDOWS = (2, 4, 8, 16)
POOL_HALO = 8
EDGE = 16
D_FF = 4 * D_MODEL
EPS = 1e-6
D_PROJ = 2048

A_H, A_GB, A_GC = 0, 256, 512
Q_OFF, K_OFF, V_OFF = 768, 1024, 1152
C_U, C_V = 1280, 1536
D_OFF = 1792

NEG = -1e30
MOD_ROWS = 16
LAT_TILE = 512
VMEM_LIMIT = 58 * 1024 * 1024

F32 = jnp.float32
BF16 = jnp.bfloat16


def _rms(x, g):
    ms = jnp.mean(x * x, axis=-1, keepdims=True)
    return x * lax.rsqrt(ms + EPS) * g


def _lane_group(shape, width):
    return lax.broadcasted_iota(jnp.int32, shape, len(shape) - 1) // width


def _head_rms(x, g):
    n = x.shape[-1]
    r = lax.broadcasted_iota(jnp.int32, (n, n), 0) // HEAD_DIM
    c = lax.broadcasted_iota(jnp.int32, (n, n), 1) // HEAD_DIM
    ones = jnp.where(r == c, 1.0, 0.0).astype(BF16)
    x2 = x * x
    hi = x2.astype(BF16)
    lo = (x2 - hi.astype(F32)).astype(BF16)
    ms = (jnp.dot(hi, ones, preferred_element_type=F32)
          + jnp.dot(lo, ones, preferred_element_type=F32)) * (1.0 / HEAD_DIM)
    return x * lax.rsqrt(ms + EPS) * g


def _swap_lanes(x, width):
    lane = lax.broadcasted_iota(jnp.int32, x.shape, x.ndim - 1)
    return jnp.where(lane % (2 * width) < width, pltpu.roll(x, 128 - width, x.ndim - 1),
                     pltpu.roll(x, width, x.ndim - 1))


def _rope(x, cs):
    cos, sin = cs[:, 0:128], cs[:, 128:256]
    parts = [x[:, s:s + 128] * cos + _swap_lanes(x[:, s:s + 128], 16) * sin for s in range(0, x.shape[-1], 128)]
    return parts[0] if len(parts) == 1 else jnp.concatenate(parts, axis=-1)


def _modulated_norm(x, g, shift, scale):
    return (_rms(x, g) * (1.0 + scale) + shift).astype(BF16)


def _ada_kernel(cc_ref, w_ref, b_ref, o_ref):
    cc = cc_ref[...]
    s = cc * jax.nn.sigmoid(cc)
    o_ref[...] = jnp.dot(s.astype(BF16), w_ref[...].astype(BF16), preferred_element_type=F32) + b_ref[...]


def _ada_call(cc, w_ada, b_ada):
    nblk = 4
    wb = 6 * D_MODEL // nblk
    return pl.pallas_call(
        _ada_kernel,
        out_shape=jax.ShapeDtypeStruct((DEPTH, MOD_ROWS, 6 * D_MODEL), F32),
        grid=(DEPTH, nblk),
        in_specs=[
            pl.BlockSpec((MOD_ROWS, D_MODEL), lambda l, n: (0, 0)),
            pl.BlockSpec((None, D_MODEL, wb), lambda l, n: (l, 0, n)),
            pl.BlockSpec((None, 1, wb), lambda l, n: (l, 0, n)),
        ],
        out_specs=pl.BlockSpec((None, MOD_ROWS, wb), lambda l, n: (l, 0, n)),
        compiler_params=pltpu.CompilerParams(dimension_semantics=("arbitrary", "arbitrary"),
                                             vmem_limit_bytes=VMEM_LIMIT),
        name="ada_mod",
    )(cc, w_ada, b_ada.reshape(DEPTH, 1, 6 * D_MODEL))


def _shift_rows(x, k):
    n = x.shape[0]
    return pltpu.roll(x, k % n, 0)


def _conv_mixer(p_ref, halo_prev, halo_next, conv_ref, tile):
    rows = pl.ds(0, tile)
    z = p_ref[rows, A_GC:A_GC + 256] * p_ref[rows, A_H:A_H + 256]
    z_ext = jnp.concatenate([halo_prev, z, halo_next], axis=0)
    z_prev = _shift_rows(z_ext, 1)[POOL_HALO:POOL_HALO + tile]
    z_next = _shift_rows(z_ext, -1)[POOL_HALO:POOL_HALO + tile]
    cw = conv_ref[...]
    y = cw[0:1] * z_prev + cw[1:2] * z + cw[2:3] * z_next
    return p_ref[rows, A_GB:A_GB + 256] * y


def _pool_mixer(p_ref, halo_prev, halo_next, wpool_ref, pscale_ref, tile, seq, tile_start):
    x = p_ref[pl.ds(0, tile), D_OFF:D_OFF + 256]
    x_ext = jnp.concatenate([halo_prev, x, halo_next], axis=0)
    s2 = _shift_rows(x_ext, 1) + x_ext
    s4 = _shift_rows(s2, 1) + _shift_rows(s2, -1)
    s8 = _shift_rows(s4, 2) + _shift_rows(s4, -2)
    s16 = _shift_rows(s8, 4) + _shift_rows(s8, -4)
    grp = _lane_group(x_ext.shape, HEAD_DIM)
    s = jnp.where(grp == 0, s2, jnp.where(grp == 1, s4, jnp.where(grp == 2, s8, s16)))
    s = s[POOL_HALO:POOL_HALO + tile]
    t = tile_start + lax.broadcasted_iota(jnp.int32, (tile, 256), 0)
    half = jnp.left_shift(1, _lane_group((tile, 256), HEAD_DIM))
    cnt = jnp.minimum(t + half, seq) - jnp.maximum(t - half, 0)
    d = s / cnt.astype(F32) - x
    y = jnp.dot(d.astype(BF16), wpool_ref[...], preferred_element_type=F32)
    return y * pscale_ref[...]


def _pair_halves(a, b, half):
    low = lax.broadcasted_iota(jnp.int32, a.shape, 1) < HEAD_DIM
    if half == 0:
        return jnp.where(low, a, pltpu.roll(b, HEAD_DIM, 1))
    return jnp.where(low, pltpu.roll(a, HEAD_DIM, 1), b)


def _sgu_mixer(p_ref, sgun_ref, wsgu_ref, bsgu_ref, tile):
    rows = pl.ds(0, tile)
    nchunk = tile // CHUNK
    assert nchunk % 2 == 0
    vn = _rms(p_ref[rows, C_V:C_V + 256], sgun_ref[...])
    x = [[vn[c * CHUNK:(c + 1) * CHUNK, t * 128:(t + 1) * 128] for t in range(2)] for c in range(nchunk)]
    zg = []
    for g in range(N_SGU_GROUPS):
        t, half = divmod(g, 2)
        rhs = jnp.concatenate([_pair_halves(x[c][t], x[c + 1][t], half) for c in range(0, nchunk, 2)], axis=1)
        w_g = wsgu_ref[g * CHUNK:(g + 1) * CHUNK, :]
        zg.append(jnp.dot(w_g, rhs.astype(BF16), preferred_element_type=F32))
    bias = bsgu_ref[...]
    outs = []
    for c in range(nchunk):
        ct, half = divmod(c, 2)
        tiles = [_pair_halves(zg[2 * t][:, ct * 128:(ct + 1) * 128], zg[2 * t + 1][:, ct * 128:(ct + 1) * 128], half)
                 for t in range(2)]
        outs.append(jnp.concatenate(tiles, axis=1) + bias)
    return p_ref[rows, C_U:C_U + 256] * jnp.concatenate(outs, axis=0)


def _attention_block(qb, segs, sink):
    low = lax.broadcasted_iota(jnp.int32, (BLOCK, 128), 1) < HEAD_DIM
    t0, t1 = qb[:, 0:128], qb[:, 128:256]
    rows = [jnp.where(low, t0, 0.0), jnp.where(low, pltpu.roll(t0, HEAD_DIM, 1), 0.0),
            jnp.where(low, 0.0, pltpu.roll(t1, HEAD_DIM, 1)), jnp.where(low, 0.0, t1)]
    q4 = jnp.concatenate(rows, axis=0).astype(BF16)
    rb = lax.broadcasted_iota(jnp.int32, (4 * BLOCK, 1), 0) // BLOCK
    sink_col = jnp.where(rb == 0, sink[0], jnp.where(rb == 1, sink[1], jnp.where(rb == 2, sink[2], sink[3])))
    scores = []
    m = sink_col
    for k, _, bias in segs:
        s = lax.dot_general(q4, k, (((1,), (1,)), ((), ())), preferred_element_type=F32)
        if bias is not None:
            s = s + bias
        scores.append(s)
        m = jnp.maximum(m, jnp.max(s, axis=-1, keepdims=True))
    denom = jnp.exp(sink_col - m)
    acc = None
    for s, (_, v, _) in zip(scores, segs):
        pr = jnp.exp(s - m)
        denom = denom + jnp.sum(pr, axis=-1, keepdims=True)
        o = jnp.dot(pr.astype(BF16), v, preferred_element_type=F32)
        acc = o if acc is None else acc + o
    acc = acc / denom
    a = [acc[i * BLOCK:(i + 1) * BLOCK] for i in range(4)]
    out0 = jnp.where(low, a[0], pltpu.roll(a[1], HEAD_DIM, 1))
    out1 = jnp.where(low, pltpu.roll(a[2], HEAD_DIM, 1), a[3])
    return jnp.concatenate([out0, out1], axis=-1)


def _layer_kernel(l_ref, *refs, tile, seq, n_total, is_ctx):
    if is_ctx:
        (h_ref, mod_ref, sink_ref, nmix_ref, qn_ref, kn_ref, conv_ref, sgun_ref, wsgu_ref, bsgu_ref, wpool_ref,
         pscale_ref, nff_ref, win_ref, wout_ref, w1_ref, w2_ref, o_ref, kv_ref, p_ref, ycat_ref, hid_ref,
         h1_ref, f_ref) = refs
        modb_ref = mod_ref
        j = pl.program_id(1)
    else:
        (h_ref, hn_ref, cs_ref, csn_ref, kvc_ref, mod_ref, modb_ref, sink_ref, nmix_ref, qn_ref,
         kn_ref, conv_ref, sgun_ref, wsgu_ref, bsgu_ref, wpool_ref, pscale_ref, nff_ref, win_ref, wout_ref,
         w1_ref, w2_ref, o_ref, p_ref, ycat_ref, hid_ref, h1_ref, f_ref, bias_ref, kvprev_ref, eprev_ref) = refs
        step = pl.program_id(0)

        @pl.when(step == 0)
        def _():
            h1_ref[...] = jnp.zeros_like(h1_ref)
            f_ref[...] = jnp.zeros_like(f_ref)
            kvprev_ref[...] = jnp.zeros_like(kvprev_ref)
            eprev_ref[...] = jnp.zeros_like(eprev_ref)
            r = lax.broadcasted_iota(jnp.int32, (4 * BLOCK, 3 * BLOCK), 0) % BLOCK
            col = lax.broadcasted_iota(jnp.int32, (4 * BLOCK, 3 * BLOCK), 1)
            seg = col // BLOCK
            jj = col % BLOCK
            band_prev = (seg == 0) & (jj >= r)
            band_next = (seg == 2) & (jj <= r)
            bias_ref[0] = jnp.where((seg == 1) | band_prev | band_next, 0.0, NEG)
            bias_ref[1] = jnp.where((seg == 1) | band_next, 0.0, NEG)
            bias_ref[2] = jnp.where((seg == 1) | band_prev, 0.0, NEG)

        j = jnp.minimum(step, n_total - 1) % (seq // tile)
        has_prev = j > 0
        has_next = j < seq // tile - 1
    layer = l_ref[0]
    tile_start = j * tile
    mod = mod_ref[...]
    sh1, sc1, g1 = mod[:, 0:D_MODEL], mod[:, D_MODEL:2 * D_MODEL], mod[:, 2 * D_MODEL:3 * D_MODEL]
    rows = pl.ds(0, tile)
    nblk = tile // BLOCK
    st = {}

    def ff1(c):
        u = jnp.dot(f_ref[...], w1_ref[:, c:c + 1024], preferred_element_type=F32)
        u = jnp.maximum(u, 0.0)
        hid_ref[:, c:c + 1024] = (u * u).astype(BF16)

    half = D_MODEL // 2

    def ff2(c):
        g2 = modb_ref[:, 5 * D_MODEL + c:5 * D_MODEL + c + half]
        y = jnp.dot(hid_ref[...], w2_ref[:, c:c + half], preferred_element_type=F32)
        o_ref[:, c:c + half] = h1_ref[:, c:c + half] + g2 * y

    back = [functools.partial(ff1, c) for c in range(0, D_FF, 1024)]
    back += [functools.partial(ff2, c) for c in range(0, D_MODEL, half)]

    def norm_in():
        nmix = nmix_ref[...]
        a_main = _modulated_norm(h_ref[...], nmix, sh1, sc1)
        if is_ctx:
            st["a_ext"] = a_main
        else:
            a_next = _modulated_norm(hn_ref[...], nmix, sh1, sc1)
            st["a_ext"] = jnp.concatenate([a_main, a_next[0:EDGE]], axis=0)
            st["a_next"] = a_next

    def in_proj():
        p_ref[...] = jnp.dot(st["a_ext"], win_ref[...], preferred_element_type=F32)
        if not is_ctx:
            st["kv_next"] = jnp.dot(st["a_next"], win_ref[:, K_OFF:K_OFF + 256], preferred_element_type=F32)

    def qk_norm():
        q = _head_rms(p_ref[rows, Q_OFF:K_OFF], qn_ref[...])
        k = _head_rms(p_ref[rows, K_OFF:V_OFF], kn_ref[...])
        if not is_ctx:
            q = _rope(q, cs_ref[...])
            k = _rope(k, cs_ref[...])
        p_ref[rows, Q_OFF:K_OFF] = q * (HEAD_DIM ** -0.5)
        k_tile = k.astype(BF16)
        v_tile = p_ref[rows, V_OFF:V_OFF + 128].astype(BF16)
        if is_ctx:
            kv_ref[:, 0:128] = k
            kv_ref[:, 128:256] = p_ref[rows, V_OFF:V_OFF + 128]
            st["k_ext"], st["v_ext"] = k_tile, v_tile
        else:
            kv_next = st["kv_next"]
            k_next = _rope(_head_rms(kv_next[:, 0:128], kn_ref[...]), csn_ref[...]).astype(BF16)
            v_next = kv_next[:, 128:256].astype(BF16)
            st["k_ext"] = jnp.concatenate([kvprev_ref[:, 0:128], k_tile, k_next], axis=0)
            st["v_ext"] = jnp.concatenate([kvprev_ref[:, 128:256], v_tile, v_next], axis=0)
            kvprev_ref[:, 0:128] = k_tile[tile - BLOCK:tile]
            kvprev_ref[:, 128:256] = v_tile[tile - BLOCK:tile]
            st["k_ctx"] = kvc_ref[:, 0:128].astype(BF16)
            st["v_ctx"] = kvc_ref[:, 128:256].astype(BF16)

    def attention(i):
        sink = [sink_ref[layer, n] for n in range(N_Q_HEADS)]
        qb = p_ref[pl.ds(i * BLOCK, BLOCK), Q_OFF:K_OFF]
        if is_ctx:
            segs = [(st["k_ext"], st["v_ext"], None)]
        else:
            which = 0
            if i == 0:
                which = jnp.where(has_prev, 0, 1)
            if i == nblk - 1:
                which = jnp.where(has_next, 0, 2)
            segs = [(st["k_ext"][i * BLOCK:(i + 3) * BLOCK], st["v_ext"][i * BLOCK:(i + 3) * BLOCK],
                     bias_ref[which]),
                    (st["k_ctx"], st["v_ctx"], None)]
        ycat_ref[i * BLOCK:(i + 1) * BLOCK, 256:512] = _attention_block(qb, segs, sink).astype(BF16)

    def local_mixers():
        if is_ctx:
            zeros = jnp.zeros((POOL_HALO, 256), F32)
            z_prev = z_next = x_prev = x_next = zeros
        else:
            ne = pl.ds(tile, POOL_HALO)
            le = pl.ds(tile - POOL_HALO, POOL_HALO)
            z_prev = jnp.where(has_prev, eprev_ref[:, 0:256], 0.0)
            z_next = jnp.where(has_next, p_ref[ne, A_GC:A_GC + 256] * p_ref[ne, A_H:A_H + 256], 0.0)
            x_prev = jnp.where(has_prev, eprev_ref[:, 256:512], 0.0)
            x_next = jnp.where(has_next, p_ref[ne, D_OFF:D_OFF + 256], 0.0)
            eprev_ref[:, 0:256] = p_ref[le, A_GC:A_GC + 256] * p_ref[le, A_H:A_H + 256]
            eprev_ref[:, 256:512] = p_ref[le, D_OFF:D_OFF + 256]
        ycat_ref[:, 0:256] = _conv_mixer(p_ref, z_prev, z_next, conv_ref, tile).astype(BF16)
        ycat_ref[:, 512:768] = _sgu_mixer(p_ref, sgun_ref, wsgu_ref, bsgu_ref, tile).astype(BF16)
        ycat_ref[:, 768:1024] = _pool_mixer(p_ref, x_prev, x_next, wpool_ref, pscale_ref,
                                            tile, seq, tile_start).astype(BF16)

    def out_proj():
        h1 = h_ref[...] + g1 * jnp.dot(ycat_ref[...], wout_ref[...], preferred_element_type=F32)
        h1_ref[...] = h1
        f_ref[...] = _modulated_norm(h1, nff_ref[...], mod[:, 3 * D_MODEL:4 * D_MODEL],
                                     mod[:, 4 * D_MODEL:5 * D_MODEL])

    att = [functools.partial(attention, i) for i in range(nblk)]
    if is_ctx:
        order = [norm_in, in_proj, qk_norm] + att + [local_mixers, out_proj] + back
    else:
        assert nblk == 4 and len(back) == 6
        order = [back[0], norm_in, in_proj, back[1], qk_norm, back[2], att[0], att[1], back[3], att[2], att[3],
                 back[4], local_mixers, back[5], out_proj]
    for piece in order:
        piece()


def _layer_spec(shape):
    nd = len(shape)
    return pl.BlockSpec((None,) + shape, lambda *g: (g[-1][0],) + (0,) * nd, pipeline_mode=pl.Buffered(1))


def _layer_call(layer, h, kv_ctx, cs, mods, mod_row, lw, *, tile, is_ctx):
    bsz, seq, _ = h.shape
    n_tiles = seq // tile
    n_total = bsz * n_tiles
    if is_ctx:
        grid = (bsz, n_tiles)
        front = lambda g: (g[0], g[1])
        back = front
    else:
        grid = (n_total + 1,)
        front = lambda g: (jnp.minimum(g[0], n_total - 1) // n_tiles, jnp.minimum(g[0], n_total - 1) % n_tiles)
        back = lambda g: (jnp.maximum(g[0] - 1, 0) // n_tiles, jnp.maximum(g[0] - 1, 0) % n_tiles)
    mod_spec = lambda which: pl.BlockSpec(
        (None, None, 1, 6 * D_MODEL),
        lambda *g: (g[-1][0], which(g)[0] if mod_row is None else mod_row, 0, 0))
    in_specs = [pl.BlockSpec((None, tile, D_MODEL), lambda *g: front(g) + (0,))]
    args = [h]
    if not is_ctx:
        bpt = tile // BLOCK
        last_blk = seq // BLOCK - 1
        next_blk = lambda g: jnp.minimum((front(g)[1] + 1) * bpt, last_blk)
        in_specs += [
            pl.BlockSpec((None, BLOCK, D_MODEL), lambda *g: (front(g)[0], next_blk(g), 0)),
            pl.BlockSpec((tile, 256), lambda *g: (front(g)[1], 0)),
            pl.BlockSpec((BLOCK, 256), lambda *g: (next_blk(g), 0)),
            pl.BlockSpec((None, kv_ctx.shape[1], 256), lambda *g: (front(g)[0], 0, 0)),
        ]
        args += [h, cs, cs, kv_ctx]
    in_specs.append(mod_spec(front))
    args.append(mods)
    if not is_ctx:
        in_specs.append(mod_spec(back))
        args.append(mods)
    in_specs += [
        pl.BlockSpec(memory_space=pltpu.SMEM),
        _layer_spec((1, D_MODEL)),
        _layer_spec((1, 256)),
        _layer_spec((1, 128)),
        _layer_spec((3, 256)),
        _layer_spec((1, 256)),
        _layer_spec((N_SGU_GROUPS * CHUNK, CHUNK)),
        _layer_spec((CHUNK, 256)),
        _layer_spec((256, 256)),
        _layer_spec((1, 256)),
        _layer_spec((1, D_MODEL)),
        _layer_spec((D_MODEL, D_PROJ)),
        _layer_spec((D_MODEL, D_MODEL)),
        _layer_spec((D_MODEL, D_FF)),
        _layer_spec((D_FF, D_MODEL)),
    ]
    args += [lw["sink"], lw["norm_mix"], lw["q_norm"], lw["k_norm"], lw["conv_w"], lw["sgu_norm"], lw["w_sgu"],
             lw["b_sgu"], lw["w_pool"], lw["pool_scale"], lw["norm_ff"], lw["w_in"], lw["w_out"], lw["w_ff1"],
             lw["w_ff2"]]
    tile_spec = pl.BlockSpec((None, tile, D_MODEL), lambda *g: back(g) + (0,))
    scratch = [pltpu.VMEM((tile if is_ctx else tile + EDGE, D_PROJ), F32),
               pltpu.VMEM((tile, D_MODEL), BF16),
               pltpu.VMEM((tile, D_FF), BF16),
               pltpu.VMEM((tile, D_MODEL), F32),
               pltpu.VMEM((tile, D_MODEL), BF16)]
    if is_ctx:
        out_shape = (jax.ShapeDtypeStruct(h.shape, F32), jax.ShapeDtypeStruct((bsz, seq, 256), F32))
        out_specs = (tile_spec, pl.BlockSpec((None, tile, 256), lambda *g: back(g) + (0,)))
        semantics = ("parallel", "parallel")
    else:
        out_shape = jax.ShapeDtypeStruct(h.shape, F32)
        out_specs = tile_spec
        scratch += [pltpu.VMEM((3, 4 * BLOCK, 3 * BLOCK), F32),
                    pltpu.VMEM((BLOCK, 256), BF16),
                    pltpu.VMEM((POOL_HALO, 512), F32)]
        semantics = ("arbitrary",)
    return pl.pallas_call(
        functools.partial(_layer_kernel, tile=tile, seq=seq, n_total=n_total, is_ctx=is_ctx),
        out_shape=out_shape,
        grid_spec=pltpu.PrefetchScalarGridSpec(
            num_scalar_prefetch=1,
            grid=grid,
            in_specs=in_specs,
            out_specs=out_specs,
            scratch_shapes=scratch,
        ),
        compiler_params=pltpu.CompilerParams(dimension_semantics=semantics,
                                             vmem_limit_bytes=VMEM_LIMIT),
        name="layer_ctx" if is_ctx else "layer_lat",
    )(layer, *args)


def _ctx_kv_kernel(l_ref, h_ref, mod_ref, nmix_ref, kn_ref, w_ref, kv_ref):
    mod = mod_ref[...]
    a = _modulated_norm(h_ref[...], nmix_ref[...], mod[:, 0:D_MODEL], mod[:, D_MODEL:2 * D_MODEL])
    kv = jnp.dot(a, w_ref[...], preferred_element_type=F32)
    kv_ref[:, 0:128] = _head_rms(kv[:, 0:128], kn_ref[...])
    kv_ref[:, 128:256] = kv[:, 128:256]


def _ctx_kv_call(layer, h, mods, mod_row, lw):
    bsz, seq, _ = h.shape
    return pl.pallas_call(
        _ctx_kv_kernel,
        out_shape=jax.ShapeDtypeStruct((bsz, seq, 256), F32),
        grid_spec=pltpu.PrefetchScalarGridSpec(
            num_scalar_prefetch=1,
            grid=(bsz, 1),
            in_specs=[
                pl.BlockSpec((None, seq, D_MODEL), lambda b, j, l: (b, 0, 0)),
                pl.BlockSpec((None, None, 1, 6 * D_MODEL), lambda b, j, l: (l[0], mod_row, 0, 0)),
                _layer_spec((1, D_MODEL)),
                _layer_spec((1, 128)),
                pl.BlockSpec((None, D_MODEL, 256), lambda b, j, l: (l[0], 0, K_OFF // 256),
                             pipeline_mode=pl.Buffered(1)),
            ],
            out_specs=pl.BlockSpec((None, seq, 256), lambda b, j, l: (b, 0, 0)),
        ),
        compiler_params=pltpu.CompilerParams(dimension_semantics=("parallel", "parallel"),
                                             vmem_limit_bytes=VMEM_LIMIT),
        name="ctx_kv",
    )(layer, h, mods, lw["norm_mix"], lw["k_norm"], lw["w_in"])


def _rope_table(length):
    rows = length // GRID_W
    row = np.repeat(np.arange(rows), GRID_W).astype(np.float32)
    col = np.tile(np.arange(GRID_W), rows).astype(np.float32)
    n_freq = HEAD_DIM // 4
    inv = jnp.asarray(ROPE_THETA, F32) ** (-jnp.arange(n_freq, dtype=F32) / n_freq)
    ang_r = jnp.asarray(row)[:, None] * inv[None, :]
    ang_c = jnp.asarray(col)[:, None] * inv[None, :]
    ang = jnp.concatenate([ang_r, ang_r, ang_c, ang_c], axis=-1)
    sign = jnp.asarray(np.where(np.arange(HEAD_DIM) % 32 < 16, -1.0, 1.0), F32)
    return jnp.concatenate([jnp.tile(jnp.cos(ang), (1, 2)), jnp.tile(jnp.sin(ang) * sign, (1, 2))], axis=-1)


def kernel(x, c, ctx, c_ctx, norm_mix, norm_ff, w_ada, b_ada, w_in, w_out, conv_w, q_norm, k_norm, sink,
           sgu_norm, w_sgu, b_sgu, w_pool, pool_scale, w_ff1, w_ff2):
    bsz, seq, _ = x.shape
    ctx_len = ctx.shape[1]
    assert bsz + 1 <= MOD_ROWS and seq % LAT_TILE == 0 and ctx_len % BLOCK == 0

    cc = jnp.concatenate([c, c_ctx[None, :], jnp.zeros((MOD_ROWS - bsz - 1, D_MODEL), F32)], axis=0)
    mods = _ada_call(cc, w_ada, b_ada).reshape(DEPTH, MOD_ROWS, 1, 6 * D_MODEL)

    eye = jnp.eye(len(POOL_WINDOWS), dtype=F32)
    lw = dict(
        sink=sink,
        norm_mix=norm_mix[:, None, :],
        q_norm=jnp.tile(q_norm, (1, N_Q_HEADS))[:, None, :],
        k_norm=jnp.tile(k_norm, (1, N_KV_HEADS))[:, None, :],
        conv_w=conv_w,
        sgu_norm=sgu_norm[:, None, :],
        w_sgu=w_sgu.reshape(DEPTH, N_SGU_GROUPS * CHUNK, CHUNK).astype(BF16),
        b_sgu=jnp.repeat(jnp.swapaxes(b_sgu, 1, 2), HEAD_DIM, axis=2),
        w_pool=jnp.einsum("lgcd,gh->lgchd", w_pool, eye).reshape(DEPTH, 256, 256).astype(BF16),
        pool_scale=pool_scale[:, None, :],
        norm_ff=norm_ff[:, None, :],
        w_in=w_in.astype(BF16),
        w_out=w_out.astype(BF16),
        w_ff1=w_ff1.astype(BF16),
        w_ff2=w_ff2.astype(BF16),
    )
    cs = _rope_table(seq)

    h_lat, h_ctx = x, ctx
    for l in range(DEPTH):
        layer = jnp.full((1,), l, jnp.int32)
        if l < DEPTH - 1:
            h_ctx_next, kv_ctx = _layer_call(layer, h_ctx, None, None, mods, bsz, lw, tile=ctx_len, is_ctx=True)
        else:
            kv_ctx = _ctx_kv_call(layer, h_ctx, mods, bsz, lw)
        h_lat = _layer_call(layer, h_lat, kv_ctx, cs, mods, None, lw, tile=LAT_TILE, is_ctx=False)
        h_ctx = h_ctx_next
    return h_lat
```

```python
import functools

import jax
import jax.numpy as jnp
import numpy as np
from jax import lax
from jax.experimental import pallas as pl
from jax.experimental.pallas import tpu as pltpu

D_MODEL = 1024
DEPTH = 4
GRID_W = 64
HEAD_DIM = 64
N_Q_HEADS = 4
N_KV_HEADS = 2
BLOCK = 128
ROPE_THETA = 10000.0
CHUNK = 128
N_SGU_GROUPS = 4
POOL_WINDOWS = (2, 4, 8, 16)
POOL_HALO = 8
EDGE = 16
D_FF = 4 * D_MODEL
EPS = 1e-6
D_PROJ = 2048

A_H, A_GB, A_GC = 0, 256, 512
Q_OFF, K_OFF, V_OFF = 768, 1024, 1152
C_U, C_V = 1280, 1536
D_OFF = 1792

NEG = -1e30
MOD_ROWS = 16
LAT_TILE = 512
VMEM_LIMIT = 58 * 1024 * 1024

F32 = jnp.float32
BF16 = jnp.bfloat16


def _rms(x, g):
    ms = jnp.mean(x * x, axis=-1, keepdims=True)
    return x * lax.rsqrt(ms + EPS) * g


def _lane_group(shape, width):
    return lax.broadcasted_iota(jnp.int32, shape, len(shape) - 1) // width


def _head_rms(x, g):
    n = x.shape[-1]
    r = lax.broadcasted_iota(jnp.int32, (n, n), 0) // HEAD_DIM
    c = lax.broadcasted_iota(jnp.int32, (n, n), 1) // HEAD_DIM
    ones = jnp.where(r == c, 1.0, 0.0).astype(BF16)
    x2 = x * x
    hi = x2.astype(BF16)
    lo = (x2 - hi.astype(F32)).astype(BF16)
    ms = (jnp.dot(hi, ones, preferred_element_type=F32)
          + jnp.dot(lo, ones, preferred_element_type=F32)) * (1.0 / HEAD_DIM)
    return x * lax.rsqrt(ms + EPS) * g


def _swap_lanes(x, width):
    lane = lax.broadcasted_iota(jnp.int32, x.shape, x.ndim - 1)
    return jnp.where(lane % (2 * width) < width, pltpu.roll(x, 128 - width, x.ndim - 1),
                     pltpu.roll(x, width, x.ndim - 1))


def _rope(x, cs):
    cos, sin = cs[:, 0:128], cs[:, 128:256]
    parts = [x[:, s:s + 128] * cos + _swap_lanes(x[:, s:s + 128], 16) * sin for s in range(0, x.shape[-1], 128)]
    return parts[0] if len(parts) == 1 else jnp.concatenate(parts, axis=-1)


def _modulated_norm(x, g, shift, scale):
    return (_rms(x, g) * (1.0 + scale) + shift).astype(BF16)


def _ada_kernel(cc_ref, w_ref, b_ref, o_ref):
    cc = cc_ref[...]
    s = cc * jax.nn.sigmoid(cc)
    o_ref[...] = jnp.dot(s.astype(BF16), w_ref[...].astype(BF16), preferred_element_type=F32) + b_ref[...]


def _ada_call(cc, w_ada, b_ada):
    nblk = 4
    wb = 6 * D_MODEL // nblk
    return pl.pallas_call(
        _ada_kernel,
        out_shape=jax.ShapeDtypeStruct((DEPTH, MOD_ROWS, 6 * D_MODEL), F32),
        grid=(DEPTH, nblk),
        in_specs=[
            pl.BlockSpec((MOD_ROWS, D_MODEL), lambda l, n: (0, 0)),
            pl.BlockSpec((None, D_MODEL, wb), lambda l, n: (l, 0, n)),
            pl.BlockSpec((None, 1, wb), lambda l, n: (l, 0, n)),
        ],
        out_specs=pl.BlockSpec((None, MOD_ROWS, wb), lambda l, n: (l, 0, n)),
        compiler_params=pltpu.CompilerParams(dimension_semantics=("arbitrary", "arbitrary"),
                                             vmem_limit_bytes=VMEM_LIMIT),
        name="ada_mod",
    )(cc, w_ada, b_ada.reshape(DEPTH, 1, 6 * D_MODEL))


def _shift_rows(x, k):
    n = x.shape[0]
    return pltpu.roll(x, k % n, 0)


def _conv_mixer(p_ref, halo_prev, halo_next, conv_ref, tile):
    rows = pl.ds(0, tile)
    z = p_ref[rows, A_GC:A_GC + 256] * p_ref[rows, A_H:A_H + 256]
    z_ext = jnp.concatenate([halo_prev, z, halo_next], axis=0)
    z_prev = _shift_rows(z_ext, 1)[POOL_HALO:POOL_HALO + tile]
    z_next = _shift_rows(z_ext, -1)[POOL_HALO:POOL_HALO + tile]
    cw = conv_ref[...]
    y = cw[0:1] * z_prev + cw[1:2] * z + cw[2:3] * z_next
    return p_ref[rows, A_GB:A_GB + 256] * y


def _pool_mixer(p_ref, halo_prev, halo_next, wpool_ref, pscale_ref, tile, seq, tile_start):
    x = p_ref[pl.ds(0, tile), D_OFF:D_OFF + 256]
    x_ext = jnp.concatenate([halo_prev, x, halo_next], axis=0)
    s2 = _shift_rows(x_ext, 1) + x_ext
    s4 = _shift_rows(s2, 1) + _shift_rows(s2, -1)
    s8 = _shift_rows(s4, 2) + _shift_rows(s4, -2)
    s16 = _shift_rows(s8, 4) + _shift_rows(s8, -4)
    grp = _lane_group(x_ext.shape, HEAD_DIM)
    s = jnp.where(grp == 0, s2, jnp.where(grp == 1, s4, jnp.where(grp == 2, s8, s16)))
    s = s[POOL_HALO:POOL_HALO + tile]
    t = tile_start + lax.broadcasted_iota(jnp.int32, (tile, 256), 0)
    half = jnp.left_shift(1, _lane_group((tile, 256), HEAD_DIM))
    cnt = jnp.minimum(t + half, seq) - jnp.maximum(t - half, 0)
    d = s / cnt.astype(F32) - x
    y = jnp.dot(d.astype(BF16), wpool_ref[...], preferred_element_type=F32)
    return y * pscale_ref[...]


def _pair_halves(a, b, half):
    low = lax.broadcasted_iota(jnp.int32, a.shape, 1) < HEAD_DIM
    if half == 0:
        return jnp.where(low, a, pltpu.roll(b, HEAD_DIM, 1))
    return jnp.where(low, pltpu.roll(a, HEAD_DIM, 1), b)


def _sgu_mixer(p_ref, sgun_ref, wsgu_ref, bsgu_ref, tile):
    rows = pl.ds(0, tile)
    nchunk = tile // CHUNK
    assert nchunk % 2 == 0
    vn = _rms(p_ref[rows, C_V:C_V + 256], sgun_ref[...])
    x = [[vn[c * CHUNK:(c + 1) * CHUNK, t * 128:(t + 1) * 128] for t in range(2)] for c in range(nchunk)]
    zg = []
    for g in range(N_SGU_GROUPS):
        t, half = divmod(g, 2)
        rhs = jnp.concatenate([_pair_halves(x[c][t], x[c + 1][t], half) for c in range(0, nchunk, 2)], axis=1)
        w_g = wsgu_ref[g * CHUNK:(g + 1) * CHUNK, :]
        zg.append(jnp.dot(w_g, rhs.astype(BF16), preferred_element_type=F32))
    bias = bsgu_ref[...]
    outs = []
    for c in range(nchunk):
        ct, half = divmod(c, 2)
        tiles = [_pair_halves(zg[2 * t][:, ct * 128:(ct + 1) * 128], zg[2 * t + 1][:, ct * 128:(ct + 1) * 128], half)
                 for t in range(2)]
        outs.append(jnp.concatenate(tiles, axis=1) + bias)
    return p_ref[rows, C_U:C_U + 256] * jnp.concatenate(outs, axis=0)


def _attention_block(qb, segs, sink):
    low = lax.broadcasted_iota(jnp.int32, (BLOCK, 128), 1) < HEAD_DIM
    t0, t1 = qb[:, 0:128], qb[:, 128:256]
    rows = [jnp.where(low, t0, 0.0), jnp.where(low, pltpu.roll(t0, HEAD_DIM, 1), 0.0),
            jnp.where(low, 0.0, pltpu.roll(t1, HEAD_DIM, 1)), jnp.where(low, 0.0, t1)]
    q4 = jnp.concatenate(rows, axis=0).astype(BF16)
    rb = lax.broadcasted_iota(jnp.int32, (4 * BLOCK, 1), 0) // BLOCK
    sink_col = jnp.where(rb == 0, sink[0], jnp.where(rb == 1, sink[1], jnp.where(rb == 2, sink[2], sink[3])))
    scores = []
    m = sink_col
    for k, _, bias in segs:
        s = lax.dot_general(q4, k, (((1,), (1,)), ((), ())), preferred_element_type=F32)
        if bias is not None:
            s = s + bias
        scores.append(s)
        m = jnp.maximum(m, jnp.max(s, axis=-1, keepdims=True))
    denom = jnp.exp(sink_col - m)
    acc = None
    for s, (_, v, _) in zip(scores, segs):
        pr = jnp.exp(s - m)
        denom = denom + jnp.sum(pr, axis=-1, keepdims=True)
        o = jnp.dot(pr.astype(BF16), v, preferred_element_type=F32)
        acc = o if acc is None else acc + o
    acc = acc / denom
    a = [acc[i * BLOCK:(i + 1) * BLOCK] for i in range(4)]
    out0 = jnp.where(low, a[0], pltpu.roll(a[1], HEAD_DIM, 1))
    out1 = jnp.where(low, pltpu.roll(a[2], HEAD_DIM, 1), a[3])
    return jnp.concatenate([out0, out1], axis=-1)


def _layer_kernel(l_ref, *refs, tile, seq, n_total, is_ctx, n_cast):
    if is_ctx:
        (h_ref, mod_ref, sink_ref, nmix_ref, qn_ref, kn_ref, conv_ref, sgun_ref, wsgu_ref, bsgu_ref, wpool_ref,
         pscale_ref, nff_ref, win_ref, wout_ref, w1_ref, w2_ref, o_ref, kv_ref, p_ref, ycat_ref, hid_ref,
         h1_ref, f_ref) = refs
        modb_ref = mod_ref
        j = pl.program_id(1)
    else:
        (h_ref, hn_ref, cs_ref, csn_ref, kvc_ref, mod_ref, modb_ref, sink_ref, nmix_ref, qn_ref,
         kn_ref, conv_ref, sgun_ref, wsgu_ref, bsgu_ref, wpool_ref, pscale_ref, nff_ref, win_ref, wout_ref,
         w1_ref, w2_ref) = refs[:22]
        cast_in = refs[22:22 + n_cast]
        o_ref = refs[22 + n_cast]
        cast_out = refs[23 + n_cast:23 + 2 * n_cast]
        (p_ref, ycat_ref, hid_ref, h1_ref, f_ref, bias_ref, kvprev_ref, eprev_ref) = refs[23 + 2 * n_cast:]
        step = pl.program_id(0)

        @pl.when(step == 0)
        def _():
            h1_ref[...] = jnp.zeros_like(h1_ref)
            f_ref[...] = jnp.zeros_like(f_ref)
            kvprev_ref[...] = jnp.zeros_like(kvprev_ref)
            eprev_ref[...] = jnp.zeros_like(eprev_ref)
            r = lax.broadcasted_iota(jnp.int32, (4 * BLOCK, 3 * BLOCK), 0) % BLOCK
            col = lax.broadcasted_iota(jnp.int32, (4 * BLOCK, 3 * BLOCK), 1)
            seg = col // BLOCK
            jj = col % BLOCK
            band_prev = (seg == 0) & (jj >= r)
            band_next = (seg == 2) & (jj <= r)
            bias_ref[0] = jnp.where((seg == 1) | band_prev | band_next, 0.0, NEG)
            bias_ref[1] = jnp.where((seg == 1) | band_next, 0.0, NEG)
            bias_ref[2] = jnp.where((seg == 1) | band_prev, 0.0, NEG)

        j = jnp.minimum(step, n_total - 1) % (seq // tile)
        has_prev = j > 0
        has_next = j < seq // tile - 1
    layer = l_ref[0]
    tile_start = j * tile
    mod = mod_ref[...]
    sh1, sc1, g1 = mod[:, 0:D_MODEL], mod[:, D_MODEL:2 * D_MODEL], mod[:, 2 * D_MODEL:3 * D_MODEL]
    rows = pl.ds(0, tile)
    nblk = tile // BLOCK
    st = {}

    def ff1(c):
        u = jnp.dot(f_ref[...], w1_ref[:, c:c + 1024], preferred_element_type=F32)
        u = jnp.maximum(u, 0.0)
        hid_ref[:, c:c + 1024] = (u * u).astype(BF16)

    half = D_MODEL // 2

    def ff2(c):
        g2 = modb_ref[:, 5 * D_MODEL + c:5 * D_MODEL + c + half]
        y = jnp.dot(hid_ref[...], w2_ref[:, c:c + half], preferred_element_type=F32)
        o_ref[:, c:c + half] = h1_ref[:, c:c + half] + g2 * y

    back = [functools.partial(ff1, c) for c in range(0, D_FF, 1024)]
    back += [functools.partial(ff2, c) for c in range(0, D_MODEL, half)]

    def norm_in():
        nmix = nmix_ref[...]
        a_main = _modulated_norm(h_ref[...], nmix, sh1, sc1)
        if is_ctx:
            st["a_ext"] = a_main
        else:
            a_next = _modulated_norm(hn_ref[...], nmix, sh1, sc1)
            st["a_ext"] = jnp.concatenate([a_main, a_next[0:EDGE]], axis=0)
            st["a_next"] = a_next

    def in_proj():
        p_ref[...] = jnp.dot(st["a_ext"], win_ref[...], preferred_element_type=F32)
        if not is_ctx:
            st["kv_next"] = jnp.dot(st["a_next"], win_ref[:, K_OFF:K_OFF + 256], preferred_element_type=F32)

    def qk_norm():
        q = _head_rms(p_ref[rows, Q_OFF:K_OFF], qn_ref[...])
        k = _head_rms(p_ref[rows, K_OFF:V_OFF], kn_ref[...])
        if not is_ctx:
            q = _rope(q, cs_ref[...])
            k = _rope(k, cs_ref[...])
        p_ref[rows, Q_OFF:K_OFF] = q * (HEAD_DIM ** -0.5)
        k_tile = k.astype(BF16)
        v_tile = p_ref[rows, V_OFF:V_OFF + 128].astype(BF16)
        if is_ctx:
            kv_ref[:, 0:128] = k
            kv_ref[:, 128:256] = p_ref[rows, V_OFF:V_OFF + 128]
            st["k_ext"], st["v_ext"] = k_tile, v_tile
        else:
            kv_next = st["kv_next"]
            k_next = _rope(_head_rms(kv_next[:, 0:128], kn_ref[...]), csn_ref[...]).astype(BF16)
            v_next = kv_next[:, 128:256].astype(BF16)
            st["k_ext"] = jnp.concatenate([kvprev_ref[:, 0:128], k_tile, k_next], axis=0)
            st["v_ext"] = jnp.concatenate([kvprev_ref[:, 128:256], v_tile, v_next], axis=0)
            kvprev_ref[:, 0:128] = k_tile[tile - BLOCK:tile]
            kvprev_ref[:, 128:256] = v_tile[tile - BLOCK:tile]
            st["k_ctx"] = kvc_ref[:, 0:128].astype(BF16)
            st["v_ctx"] = kvc_ref[:, 128:256].astype(BF16)

    def attention(i):
        sink = [sink_ref[layer, n] for n in range(N_Q_HEADS)]
        qb = p_ref[pl.ds(i * BLOCK, BLOCK), Q_OFF:K_OFF]
        if is_ctx:
            segs = [(st["k_ext"], st["v_ext"], None)]
        else:
            which = 0
            if i == 0:
                which = jnp.where(has_prev, 0, 1)
            if i == nblk - 1:
                which = jnp.where(has_next, 0, 2)
            segs = [(st["k_ext"][i * BLOCK:(i + 3) * BLOCK], st["v_ext"][i * BLOCK:(i + 3) * BLOCK],
                     bias_ref[which]),
                    (st["k_ctx"], st["v_ctx"], None)]
        ycat_ref[i * BLOCK:(i + 1) * BLOCK, 256:512] = _attention_block(qb, segs, sink).astype(BF16)

    def local_mixers():
        if is_ctx:
            zeros = jnp.zeros((POOL_HALO, 256), F32)
            z_prev = z_next = x_prev = x_next = zeros
        else:
            ne = pl.ds(tile, POOL_HALO)
            le = pl.ds(tile - POOL_HALO, POOL_HALO)
            z_prev = jnp.where(has_prev, eprev_ref[:, 0:256], 0.0)
            z_next = jnp.where(has_next, p_ref[ne, A_GC:A_GC + 256] * p_ref[ne, A_H:A_H + 256], 0.0)
            x_prev = jnp.where(has_prev, eprev_ref[:, 256:512], 0.0)
            x_next = jnp.where(has_next, p_ref[ne, D_OFF:D_OFF + 256], 0.0)
            eprev_ref[:, 0:256] = p_ref[le, A_GC:A_GC + 256] * p_ref[le, A_H:A_H + 256]
            eprev_ref[:, 256:512] = p_ref[le, D_OFF:D_OFF + 256]
        ycat_ref[:, 0:256] = _conv_mixer(p_ref, z_prev, z_next, conv_ref, tile).astype(BF16)
        ycat_ref[:, 512:768] = _sgu_mixer(p_ref, sgun_ref, wsgu_ref, bsgu_ref, tile).astype(BF16)
        ycat_ref[:, 768:1024] = _pool_mixer(p_ref, x_prev, x_next, wpool_ref, pscale_ref,
                                            tile, seq, tile_start).astype(BF16)

    def out_proj():
        st["h1"] = h_ref[...] + g1 * jnp.dot(ycat_ref[...], wout_ref[...], preferred_element_type=F32)

    def norm_ff():
        h1 = st["h1"]
        h1_ref[...] = h1
        f_ref[...] = _modulated_norm(h1, nff_ref[...], mod[:, 3 * D_MODEL:4 * D_MODEL],
                                     mod[:, 4 * D_MODEL:5 * D_MODEL])

    att = [functools.partial(attention, i) for i in range(nblk)]
    if is_ctx:
        order = [norm_in, in_proj, qk_norm] + att + [local_mixers, out_proj, norm_ff] + back
    else:
        assert nblk == 4 and len(back) == 6
        def cast_next_weights():
            for src_ref, dst_ref in zip(cast_in, cast_out):
                dst_ref[...] = src_ref[...].astype(BF16)

        order = [back[0], norm_in, in_proj, cast_next_weights, back[1], qk_norm, back[2], att[0], att[1], back[3],
                 att[2], att[3], back[4], local_mixers, out_proj, back[5], norm_ff]
    for piece in order:
        piece()


def _layer_spec(shape):
    nd = len(shape)
    return pl.BlockSpec((None,) + shape, lambda *g: (g[-1][0],) + (0,) * nd, pipeline_mode=pl.Buffered(1))


def _whole_spec(shape):
    nd = len(shape)
    return pl.BlockSpec((None,) + shape, lambda *g: (0,) * (nd + 1), pipeline_mode=pl.Buffered(1))


def _layer_call(layer, h, kv_ctx, cs, mods, mod_row, lw, wts, next_w=(), *, tile, is_ctx):
    bsz, seq, _ = h.shape
    n_tiles = seq // tile
    n_total = bsz * n_tiles
    if is_ctx:
        grid = (bsz, n_tiles)
        front = lambda g: (g[0], g[1])
        back = front
    else:
        grid = (n_total + 1,)
        front = lambda g: (jnp.minimum(g[0], n_total - 1) // n_tiles, jnp.minimum(g[0], n_total - 1) % n_tiles)
        back = lambda g: (jnp.maximum(g[0] - 1, 0) // n_tiles, jnp.maximum(g[0] - 1, 0) % n_tiles)
    mod_spec = lambda which: pl.BlockSpec(
        (None, None, 1, 6 * D_MODEL),
        lambda *g: (g[-1][0], which(g)[0] if mod_row is None else mod_row, 0, 0))
    in_specs = [pl.BlockSpec((None, tile, D_MODEL), lambda *g: front(g) + (0,))]
    args = [h]
    if not is_ctx:
        bpt = tile // BLOCK
        last_blk = seq // BLOCK - 1
        next_blk = lambda g: jnp.minimum((front(g)[1] + 1) * bpt, last_blk)
        in_specs += [
            pl.BlockSpec((None, BLOCK, D_MODEL), lambda *g: (front(g)[0], next_blk(g), 0)),
            pl.BlockSpec((tile, 256), lambda *g: (front(g)[1], 0)),
            pl.BlockSpec((BLOCK, 256), lambda *g: (next_blk(g), 0)),
            pl.BlockSpec((None, kv_ctx.shape[1], 256), lambda *g: (front(g)[0], 0, 0)),
        ]
        args += [h, cs, cs, kv_ctx]
    in_specs.append(mod_spec(front))
    args.append(mods)
    if not is_ctx:
        in_specs.append(mod_spec(back))
        args.append(mods)
    in_specs += [
        pl.BlockSpec(memory_space=pltpu.SMEM),
        _layer_spec((1, D_MODEL)),
        _layer_spec((1, 256)),
        _layer_spec((1, 128)),
        _layer_spec((3, 256)),
        _layer_spec((1, 256)),
        _layer_spec((N_SGU_GROUPS * CHUNK, CHUNK)),
        _layer_spec((CHUNK, 256)),
        _layer_spec((256, 256)),
        _layer_spec((1, 256)),
        _layer_spec((1, D_MODEL)),
    ]
    args += [lw["sink"], lw["norm_mix"], lw["q_norm"], lw["k_norm"], lw["conv_w"], lw["sgu_norm"], lw["w_sgu"],
             lw["b_sgu"], lw["w_pool"], lw["pool_scale"], lw["norm_ff"]]
    in_specs += [_whole_spec(w.shape[1:]) for w in wts]
    args += list(wts)
    assert not (is_ctx and next_w)
    chunk = lambda g: jnp.minimum(g[0], n_total - 1)
    cast_shapes, cast_specs = [], []
    for w in next_w:
        rows, cols = w.shape[1] // n_total, w.shape[2]
        in_specs.append(pl.BlockSpec((None, rows, cols),
                                     lambda *g: (jnp.minimum(g[-1][0] + 1, DEPTH - 1), chunk(g), 0)))
        args.append(w)
        cast_shapes.append(jax.ShapeDtypeStruct((1,) + w.shape[1:], BF16))
        cast_specs.append(pl.BlockSpec((None, rows, cols), lambda *g: (0, chunk(g), 0)))
    tile_spec = pl.BlockSpec((None, tile, D_MODEL), lambda *g: back(g) + (0,))
    scratch = [pltpu.VMEM((tile if is_ctx else tile + EDGE, D_PROJ), F32),
               pltpu.VMEM((tile, D_MODEL), BF16),
               pltpu.VMEM((tile, D_FF), BF16),
               pltpu.VMEM((tile, D_MODEL), F32),
               pltpu.VMEM((tile, D_MODEL), BF16)]
    if is_ctx:
        out_shape = (jax.ShapeDtypeStruct(h.shape, F32), jax.ShapeDtypeStruct((bsz, seq, 256), F32))
        out_specs = (tile_spec, pl.BlockSpec((None, tile, 256), lambda *g: back(g) + (0,)))
        semantics = ("parallel", "parallel")
    else:
        out_shape = (jax.ShapeDtypeStruct(h.shape, F32),) + tuple(cast_shapes)
        out_specs = (tile_spec,) + tuple(cast_specs)
        scratch += [pltpu.VMEM((3, 4 * BLOCK, 3 * BLOCK), F32),
                    pltpu.VMEM((BLOCK, 256), BF16),
                    pltpu.VMEM((POOL_HALO, 512), F32)]
        semantics = ("arbitrary",)
    return pl.pallas_call(
        functools.partial(_layer_kernel, tile=tile, seq=seq, n_total=n_total, is_ctx=is_ctx, n_cast=len(next_w)),
        out_shape=out_shape,
        grid_spec=pltpu.PrefetchScalarGridSpec(
            num_scalar_prefetch=1,
            grid=grid,
            in_specs=in_specs,
            out_specs=out_specs,
            scratch_shapes=scratch,
        ),
        compiler_params=pltpu.CompilerParams(dimension_semantics=semantics,
                                             vmem_limit_bytes=VMEM_LIMIT),
        name="layer_ctx" if is_ctx else "layer_lat",
    )(layer, *args)


def _ctx_kv_kernel(l_ref, h_ref, mod_ref, nmix_ref, kn_ref, w_ref, kv_ref):
    mod = mod_ref[...]
    a = _modulated_norm(h_ref[...], nmix_ref[...], mod[:, 0:D_MODEL], mod[:, D_MODEL:2 * D_MODEL])
    kv = jnp.dot(a, w_ref[...], preferred_element_type=F32)
    kv_ref[:, 0:128] = _head_rms(kv[:, 0:128], kn_ref[...])
    kv_ref[:, 128:256] = kv[:, 128:256]


def _ctx_kv_call(layer, h, mods, mod_row, lw, w_in_b):
    bsz, seq, _ = h.shape
    return pl.pallas_call(
        _ctx_kv_kernel,
        out_shape=jax.ShapeDtypeStruct((bsz, seq, 256), F32),
        grid_spec=pltpu.PrefetchScalarGridSpec(
            num_scalar_prefetch=1,
            grid=(bsz, 1),
            in_specs=[
                pl.BlockSpec((None, seq, D_MODEL), lambda b, j, l: (b, 0, 0)),
                pl.BlockSpec((None, None, 1, 6 * D_MODEL), lambda b, j, l: (l[0], mod_row, 0, 0)),
                _layer_spec((1, D_MODEL)),
                _layer_spec((1, 128)),
                pl.BlockSpec((None, D_MODEL, 256), lambda b, j, l: (0, 0, K_OFF // 256),
                             pipeline_mode=pl.Buffered(1)),
            ],
            out_specs=pl.BlockSpec((None, seq, 256), lambda b, j, l: (b, 0, 0)),
        ),
        compiler_params=pltpu.CompilerParams(dimension_semantics=("parallel", "parallel"),
                                             vmem_limit_bytes=VMEM_LIMIT),
        name="ctx_kv",
    )(layer, h, mods, lw["norm_mix"], lw["k_norm"], w_in_b)


def _rope_table(length):
    rows = length // GRID_W
    row = np.repeat(np.arange(rows), GRID_W).astype(np.float32)
    col = np.tile(np.arange(GRID_W), rows).astype(np.float32)
    n_freq = HEAD_DIM // 4
    inv = jnp.asarray(ROPE_THETA, F32) ** (-jnp.arange(n_freq, dtype=F32) / n_freq)
    ang_r = jnp.asarray(row)[:, None] * inv[None, :]
    ang_c = jnp.asarray(col)[:, None] * inv[None, :]
    ang = jnp.concatenate([ang_r, ang_r, ang_c, ang_c], axis=-1)
    sign = jnp.asarray(np.where(np.arange(HEAD_DIM) % 32 < 16, -1.0, 1.0), F32)
    return jnp.concatenate([jnp.tile(jnp.cos(ang), (1, 2)), jnp.tile(jnp.sin(ang) * sign, (1, 2))], axis=-1)


def kernel(x, c, ctx, c_ctx, norm_mix, norm_ff, w_ada, b_ada, w_in, w_out, conv_w, q_norm, k_norm, sink,
           sgu_norm, w_sgu, b_sgu, w_pool, pool_scale, w_ff1, w_ff2):
    bsz, seq, _ = x.shape
    ctx_len = ctx.shape[1]
    assert bsz + 1 <= MOD_ROWS and seq % LAT_TILE == 0 and ctx_len % BLOCK == 0

    cc = jnp.concatenate([c, c_ctx[None, :], jnp.zeros((MOD_ROWS - bsz - 1, D_MODEL), F32)], axis=0)
    mods = _ada_call(cc, w_ada, b_ada).reshape(DEPTH, MOD_ROWS, 1, 6 * D_MODEL)

    eye = jnp.eye(len(POOL_WINDOWS), dtype=F32)
    lw = dict(
        sink=sink,
        norm_mix=norm_mix[:, None, :],
        q_norm=jnp.tile(q_norm, (1, N_Q_HEADS))[:, None, :],
        k_norm=jnp.tile(k_norm, (1, N_KV_HEADS))[:, None, :],
        conv_w=conv_w,
        sgu_norm=sgu_norm[:, None, :],
        w_sgu=w_sgu.reshape(DEPTH, N_SGU_GROUPS * CHUNK, CHUNK).astype(BF16),
        b_sgu=jnp.repeat(jnp.swapaxes(b_sgu, 1, 2), HEAD_DIM, axis=2),
        w_pool=jnp.einsum("lgcd,gh->lgchd", w_pool, eye).reshape(DEPTH, 256, 256).astype(BF16),
        pool_scale=pool_scale[:, None, :],
        norm_ff=norm_ff[:, None, :],
    )
    cs = _rope_table(seq)

    big_w = (w_in, w_out, w_ff1, w_ff2)
    wts = tuple(w[0:1].astype(BF16) for w in big_w)
    h_lat, h_ctx = x, ctx
    for l in range(DEPTH):
        layer = jnp.full((1,), l, jnp.int32)
        last = l == DEPTH - 1
        if last:
            kv_ctx = _ctx_kv_call(layer, h_ctx, mods, bsz, lw, wts[0])
        else:
            h_ctx, kv_ctx = _layer_call(layer, h_ctx, None, None, mods, bsz, lw, wts, tile=ctx_len, is_ctx=True)
        outs = _layer_call(layer, h_lat, kv_ctx, cs, mods, None, lw, wts, () if last else big_w,
                           tile=LAT_TILE, is_ctx=False)
        h_lat, wts = outs[0], tuple(outs[1:])
    return h_lat
```

```python
import functools

import jax
import jax.numpy as jnp
import numpy as np
from jax import lax
from jax.experimental import pallas as pl
from jax.experimental.pallas import tpu as pltpu

D_MODEL = 1024
DEPTH = 4
GRID_W = 64
HEAD_DIM = 64
N_Q_HEADS = 4
N_KV_HEADS = 2
BLOCK = 128
ROPE_THETA = 10000.0
CHUNK = 128
N_SGU_GROUPS = 4
POOL_WINDOWS = (2, 4, 8, 16)
POOL_HALO = 8
EDGE = 16
D_FF = 4 * D_MODEL
EPS = 1e-6
D_PROJ = 2048

A_H, A_GB, A_GC = 0, 256, 512
Q_OFF, K_OFF, V_OFF = 768, 1024, 1152
C_U, C_V = 1280, 1536
D_OFF = 1792

NEG = -1e30
MOD_ROWS = 16
LAT_TILE = 512
VMEM_LIMIT = 58 * 1024 * 1024

F32 = jnp.float32
BF16 = jnp.bfloat16


def _rms(x, g):
    ms = jnp.mean(x * x, axis=-1, keepdims=True)
    return x * lax.rsqrt(ms + EPS) * g


def _lane_group(shape, width):
    return lax.broadcasted_iota(jnp.int32, shape, len(shape) - 1) // width


def _head_rms(x, g):
    n = x.shape[-1]
    r = lax.broadcasted_iota(jnp.int32, (n, n), 0) // HEAD_DIM
    c = lax.broadcasted_iota(jnp.int32, (n, n), 1) // HEAD_DIM
    ones = jnp.where(r == c, 1.0, 0.0).astype(BF16)
    x2 = x * x
    hi = x2.astype(BF16)
    lo = (x2 - hi.astype(F32)).astype(BF16)
    ms = (jnp.dot(hi, ones, preferred_element_type=F32)
          + jnp.dot(lo, ones, preferred_element_type=F32)) * (1.0 / HEAD_DIM)
    return x * lax.rsqrt(ms + EPS) * g


def _swap_lanes(x, width):
    lane = lax.broadcasted_iota(jnp.int32, x.shape, x.ndim - 1)
    return jnp.where(lane % (2 * width) < width, pltpu.roll(x, 128 - width, x.ndim - 1),
                     pltpu.roll(x, width, x.ndim - 1))


def _rope(x, cs):
    cos, sin = cs[:, 0:128], cs[:, 128:256]
    parts = [x[:, s:s + 128] * cos + _swap_lanes(x[:, s:s + 128], 16) * sin for s in range(0, x.shape[-1], 128)]
    return parts[0] if len(parts) == 1 else jnp.concatenate(parts, axis=-1)


def _modulated_norm(x, g, shift, scale):
    return (_rms(x, g) * (1.0 + scale) + shift).astype(BF16)


def _ada_kernel(cc_ref, w_ref, b_ref, *refs):
    n_cast = (len(refs) - 1) // 2
    o_ref = refs[n_cast]
    cc = cc_ref[...]
    s = cc * jax.nn.sigmoid(cc)
    o_ref[...] = jnp.dot(s.astype(BF16), w_ref[...].astype(BF16), preferred_element_type=F32) + b_ref[...]
    for src_ref, dst_ref in zip(refs[:n_cast], refs[n_cast + 1:]):
        dst_ref[...] = src_ref[...].astype(BF16)


def _ada_call(cc, w_ada, b_ada, big_w):
    nblk = 4
    wb = 6 * D_MODEL // nblk
    steps = DEPTH * nblk
    chunk = lambda l, n: (0, l * nblk + n, 0)
    cast_in = [pl.BlockSpec((None, w.shape[1] // steps, w.shape[2]), chunk) for w in big_w]
    outs = pl.pallas_call(
        _ada_kernel,
        out_shape=(jax.ShapeDtypeStruct((DEPTH, MOD_ROWS, 6 * D_MODEL), F32),)
        + tuple(jax.ShapeDtypeStruct((1,) + w.shape[1:], BF16) for w in big_w),
        grid=(DEPTH, nblk),
        in_specs=[
            pl.BlockSpec((MOD_ROWS, D_MODEL), lambda l, n: (0, 0)),
            pl.BlockSpec((None, D_MODEL, wb), lambda l, n: (l, 0, n)),
            pl.BlockSpec((None, 1, wb), lambda l, n: (l, 0, n)),
        ] + cast_in,
        out_specs=(pl.BlockSpec((None, MOD_ROWS, wb), lambda l, n: (l, 0, n)),) + tuple(cast_in),
        compiler_params=pltpu.CompilerParams(dimension_semantics=("arbitrary", "arbitrary"),
                                             vmem_limit_bytes=VMEM_LIMIT),
        name="ada_mod",
    )(cc, w_ada, b_ada.reshape(DEPTH, 1, 6 * D_MODEL), *big_w)
    return outs[0], tuple(outs[1:])


def _shift_rows(x, k):
    n = x.shape[0]
    return pltpu.roll(x, k % n, 0)


def _conv_mixer(p_ref, halo_prev, halo_next, conv_ref, tile):
    rows = pl.ds(0, tile)
    z = p_ref[rows, A_GC:A_GC + 256] * p_ref[rows, A_H:A_H + 256]
    z_ext = jnp.concatenate([halo_prev, z, halo_next], axis=0)
    z_prev = _shift_rows(z_ext, 1)[POOL_HALO:POOL_HALO + tile]
    z_next = _shift_rows(z_ext, -1)[POOL_HALO:POOL_HALO + tile]
    cw = conv_ref[...]
    y = cw[0:1] * z_prev + cw[1:2] * z + cw[2:3] * z_next
    return p_ref[rows, A_GB:A_GB + 256] * y


def _pool_mixer(p_ref, halo_prev, halo_next, wpool_ref, pscale_ref, tile, seq, tile_start):
    x = p_ref[pl.ds(0, tile), D_OFF:D_OFF + 256]
    x_ext = jnp.concatenate([halo_prev, x, halo_next], axis=0)
    s2 = _shift_rows(x_ext, 1) + x_ext
    s4 = _shift_rows(s2, 1) + _shift_rows(s2, -1)
    s8 = _shift_rows(s4, 2) + _shift_rows(s4, -2)
    s16 = _shift_rows(s8, 4) + _shift_rows(s8, -4)
    grp = _lane_group(x_ext.shape, HEAD_DIM)
    s = jnp.where(grp == 0, s2, jnp.where(grp == 1, s4, jnp.where(grp == 2, s8, s16)))
    s = s[POOL_HALO:POOL_HALO + tile]
    t = tile_start + lax.broadcasted_iota(jnp.int32, (tile, 256), 0)
    half = jnp.left_shift(1, _lane_group((tile, 256), HEAD_DIM))
    cnt = jnp.minimum(t + half, seq) - jnp.maximum(t - half, 0)
    d = s / cnt.astype(F32) - x
    y = jnp.dot(d.astype(BF16), wpool_ref[...], preferred_element_type=F32)
    return y * pscale_ref[...]


def _pair_halves(a, b, half):
    low = lax.broadcasted_iota(jnp.int32, a.shape, 1) < HEAD_DIM
    if half == 0:
        return jnp.where(low, a, pltpu.roll(b, HEAD_DIM, 1))
    return jnp.where(low, pltpu.roll(a, HEAD_DIM, 1), b)


def _sgu_mixer(p_ref, sgun_ref, wsgu_ref, bsgu_ref, tile):
    rows = pl.ds(0, tile)
    nchunk = tile // CHUNK
    assert nchunk % 2 == 0
    vn = _rms(p_ref[rows, C_V:C_V + 256], sgun_ref[...])
    x = [[vn[c * CHUNK:(c + 1) * CHUNK, t * 128:(t + 1) * 128] for t in range(2)] for c in range(nchunk)]
    zg = []
    for g in range(N_SGU_GROUPS):
        t, half = divmod(g, 2)
        rhs = jnp.concatenate([_pair_halves(x[c][t], x[c + 1][t], half) for c in range(0, nchunk, 2)], axis=1)
        w_g = wsgu_ref[g * CHUNK:(g + 1) * CHUNK, :]
        zg.append(jnp.dot(w_g, rhs.astype(BF16), preferred_element_type=F32))
    bias = bsgu_ref[...]
    outs = []
    for c in range(nchunk):
        ct, half = divmod(c, 2)
        tiles = [_pair_halves(zg[2 * t][:, ct * 128:(ct + 1) * 128], zg[2 * t + 1][:, ct * 128:(ct + 1) * 128], half)
                 for t in range(2)]
        outs.append(jnp.concatenate(tiles, axis=1) + bias)
    return p_ref[rows, C_U:C_U + 256] * jnp.concatenate(outs, axis=0)


def _attention_block(qb, segs, sink):
    low = lax.broadcasted_iota(jnp.int32, (BLOCK, 128), 1) < HEAD_DIM
    t0, t1 = qb[:, 0:128], qb[:, 128:256]
    rows = [jnp.where(low, t0, 0.0), jnp.where(low, pltpu.roll(t0, HEAD_DIM, 1), 0.0),
            jnp.where(low, 0.0, pltpu.roll(t1, HEAD_DIM, 1)), jnp.where(low, 0.0, t1)]
    q4 = jnp.concatenate(rows, axis=0).astype(BF16)
    rb = lax.broadcasted_iota(jnp.int32, (4 * BLOCK, 1), 0) // BLOCK
    sink_col = jnp.where(rb == 0, sink[0], jnp.where(rb == 1, sink[1], jnp.where(rb == 2, sink[2], sink[3])))
    scores = []
    m = sink_col
    for k, _, bias in segs:
        s = lax.dot_general(q4, k, (((1,), (1,)), ((), ())), preferred_element_type=F32)
        if bias is not None:
            s = s + bias
        scores.append(s)
        m = jnp.maximum(m, jnp.max(s, axis=-1, keepdims=True))
    denom = jnp.exp(sink_col - m)
    acc = None
    for s, (_, v, _) in zip(scores, segs):
        pr = jnp.exp(s - m)
        denom = denom + jnp.sum(pr, axis=-1, keepdims=True)
        o = jnp.dot(pr.astype(BF16), v, preferred_element_type=F32)
        acc = o if acc is None else acc + o
    acc = acc / denom
    a = [acc[i * BLOCK:(i + 1) * BLOCK] for i in range(4)]
    out0 = jnp.where(low, a[0], pltpu.roll(a[1], HEAD_DIM, 1))
    out1 = jnp.where(low, pltpu.roll(a[2], HEAD_DIM, 1), a[3])
    return jnp.concatenate([out0, out1], axis=-1)


def _vec_views(vec_ref):
    return dict(conv=vec_ref.at[0:3, 0:256], nmix=vec_ref.at[3:4, :], nff=vec_ref.at[4:5, :],
                qn=vec_ref.at[5:6, 0:256], kn=vec_ref.at[5:6, 256:384], sgun=vec_ref.at[5:6, 512:768],
                pscale=vec_ref.at[5:6, 768:1024])


def _layer_kernel(*refs, tile, seq, n_total, is_ctx, n_cast):
    if is_ctx:
        (l_ref, h_ref, mod_ref, sink_ref, vec_ref, wsgu_ref, bsgu_ref, wpool_ref, win_ref, wout_ref, w1_ref, w2_ref,
         o_ref, kv_ref, p_ref, ycat_ref, hid_ref, h1_ref, f_ref) = refs
        modb_ref = mod_ref
        j = pl.program_id(1)
    else:
        (sched_ref, l_ref, h_ref, hn_ref, cs_ref, csn_ref, kvc_ref, mod_ref, modb_ref, sink_ref, vec_ref, wsgu_ref,
         bsgu_ref, wpool_ref, win_ref, wout_ref, w1_ref, w2_ref) = refs[:18]
        cast_in = refs[18:18 + n_cast]
        o_ref = refs[18 + n_cast]
        cast_out = refs[19 + n_cast:19 + 2 * n_cast]
        (p_ref, ycat_ref, hid_ref, h1_ref, f_ref, bias_ref, kvprev_ref, eprev_ref) = refs[19 + 2 * n_cast:]
        step = pl.program_id(0)
    vv = _vec_views(vec_ref)
    nmix_ref, nff_ref, qn_ref, kn_ref = vv["nmix"], vv["nff"], vv["qn"], vv["kn"]
    conv_ref, sgun_ref, pscale_ref = vv["conv"], vv["sgun"], vv["pscale"]
    if not is_ctx:

        @pl.when(step == 0)
        def _():
            h1_ref[...] = jnp.zeros_like(h1_ref)
            f_ref[...] = jnp.zeros_like(f_ref)
            kvprev_ref[...] = jnp.zeros_like(kvprev_ref)
            eprev_ref[...] = jnp.zeros_like(eprev_ref)
            r = lax.broadcasted_iota(jnp.int32, (4 * BLOCK, 3 * BLOCK), 0) % BLOCK
            col = lax.broadcasted_iota(jnp.int32, (4 * BLOCK, 3 * BLOCK), 1)
            seg = col // BLOCK
            jj = col % BLOCK
            band_prev = (seg == 0) & (jj >= r)
            band_next = (seg == 2) & (jj <= r)
            bias_ref[0] = jnp.where((seg == 1) | band_prev | band_next, 0.0, NEG)
            bias_ref[1] = jnp.where((seg == 1) | band_next, 0.0, NEG)
            bias_ref[2] = jnp.where((seg == 1) | band_prev, 0.0, NEG)

        j = sched_ref[1, step]
        has_prev = j > 0
        has_next = j < seq // tile - 1
    layer = l_ref[0]
    tile_start = j * tile
    mod = mod_ref[...]
    sh1, sc1, g1 = mod[:, 0:D_MODEL], mod[:, D_MODEL:2 * D_MODEL], mod[:, 2 * D_MODEL:3 * D_MODEL]
    rows = pl.ds(0, tile)
    nblk = tile // BLOCK
    st = {}

    def ff1(c):
        u = jnp.dot(f_ref[...], w1_ref[:, c:c + 1024], preferred_element_type=F32)
        u = jnp.maximum(u, 0.0)
        hid_ref[:, c:c + 1024] = (u * u).astype(BF16)

    half = D_MODEL // 2

    def ff2(c):
        g2 = modb_ref[:, 5 * D_MODEL + c:5 * D_MODEL + c + half]
        y = jnp.dot(hid_ref[...], w2_ref[:, c:c + half], preferred_element_type=F32)
        o_ref[:, c:c + half] = h1_ref[:, c:c + half] + g2 * y

    back = [functools.partial(ff1, c) for c in range(0, D_FF, 1024)]
    back += [functools.partial(ff2, c) for c in range(0, D_MODEL, half)]

    def norm_in():
        nmix = nmix_ref[...]
        a_main = _modulated_norm(h_ref[...], nmix, sh1, sc1)
        if is_ctx:
            st["a_ext"] = a_main
        else:
            a_next = _modulated_norm(hn_ref[...], nmix, sh1, sc1)
            st["a_ext"] = jnp.concatenate([a_main, a_next[0:EDGE]], axis=0)
            st["a_next"] = a_next

    def in_proj():
        p_ref[...] = jnp.dot(st["a_ext"], win_ref[...], preferred_element_type=F32)
        if not is_ctx:
            st["kv_next"] = jnp.dot(st["a_next"], win_ref[:, K_OFF:K_OFF + 256], preferred_element_type=F32)

    def qk_norm():
        q = _head_rms(p_ref[rows, Q_OFF:K_OFF], qn_ref[...])
        k = _head_rms(p_ref[rows, K_OFF:V_OFF], kn_ref[...])
        if not is_ctx:
            q = _rope(q, cs_ref[...])
            k = _rope(k, cs_ref[...])
        p_ref[rows, Q_OFF:K_OFF] = q * (HEAD_DIM ** -0.5)
        k_tile = k.astype(BF16)
        v_tile = p_ref[rows, V_OFF:V_OFF + 128].astype(BF16)
        if is_ctx:
            kv_ref[:, 0:128] = k
            kv_ref[:, 128:256] = p_ref[rows, V_OFF:V_OFF + 128]
            st["k_ext"], st["v_ext"] = k_tile, v_tile
        else:
            kv_next = st["kv_next"]
            k_next = _rope(_head_rms(kv_next[:, 0:128], kn_ref[...]), csn_ref[...]).astype(BF16)
            v_next = kv_next[:, 128:256].astype(BF16)
            st["k_ext"] = jnp.concatenate([kvprev_ref[:, 0:128], k_tile, k_next], axis=0)
            st["v_ext"] = jnp.concatenate([kvprev_ref[:, 128:256], v_tile, v_next], axis=0)
            kvprev_ref[:, 0:128] = k_tile[tile - BLOCK:tile]
            kvprev_ref[:, 128:256] = v_tile[tile - BLOCK:tile]
            st["k_ctx"] = kvc_ref[:, 0:128].astype(BF16)
            st["v_ctx"] = kvc_ref[:, 128:256].astype(BF16)

    def attention(i):
        sink = [sink_ref[layer, n] for n in range(N_Q_HEADS)]
        qb = p_ref[pl.ds(i * BLOCK, BLOCK), Q_OFF:K_OFF]
        if is_ctx:
            segs = [(st["k_ext"], st["v_ext"], None)]
        else:
            which = 0
            if i == 0:
                which = jnp.where(has_prev, 0, 1)
            if i == nblk - 1:
                which = jnp.where(has_next, 0, 2)
            segs = [(st["k_ext"][i * BLOCK:(i + 3) * BLOCK], st["v_ext"][i * BLOCK:(i + 3) * BLOCK],
                     bias_ref[which]),
                    (st["k_ctx"], st["v_ctx"], None)]
        ycat_ref[i * BLOCK:(i + 1) * BLOCK, 256:512] = _attention_block(qb, segs, sink).astype(BF16)

    def local_mixers():
        if is_ctx:
            zeros = jnp.zeros((POOL_HALO, 256), F32)
            z_prev = z_next = x_prev = x_next = zeros
        else:
            ne = pl.ds(tile, POOL_HALO)
            le = pl.ds(tile - POOL_HALO, POOL_HALO)
            z_prev = jnp.where(has_prev, eprev_ref[:, 0:256], 0.0)
            z_next = jnp.where(has_next, p_ref[ne, A_GC:A_GC + 256] * p_ref[ne, A_H:A_H + 256], 0.0)
            x_prev = jnp.where(has_prev, eprev_ref[:, 256:512], 0.0)
            x_next = jnp.where(has_next, p_ref[ne, D_OFF:D_OFF + 256], 0.0)
            eprev_ref[:, 0:256] = p_ref[le, A_GC:A_GC + 256] * p_ref[le, A_H:A_H + 256]
            eprev_ref[:, 256:512] = p_ref[le, D_OFF:D_OFF + 256]
        ycat_ref[:, 0:256] = _conv_mixer(p_ref, z_prev, z_next, conv_ref, tile).astype(BF16)
        ycat_ref[:, 512:768] = _sgu_mixer(p_ref, sgun_ref, wsgu_ref, bsgu_ref, tile).astype(BF16)
        ycat_ref[:, 768:1024] = _pool_mixer(p_ref, x_prev, x_next, wpool_ref, pscale_ref,
                                            tile, seq, tile_start).astype(BF16)

    def out_proj():
        st["h1"] = h_ref[...] + g1 * jnp.dot(ycat_ref[...], wout_ref[...], preferred_element_type=F32)

    def norm_ff():
        h1 = st["h1"]
        h1_ref[...] = h1
        f_ref[...] = _modulated_norm(h1, nff_ref[...], mod[:, 3 * D_MODEL:4 * D_MODEL],
                                     mod[:, 4 * D_MODEL:5 * D_MODEL])

    att = [functools.partial(attention, i) for i in range(nblk)]
    if is_ctx:
        order = [norm_in, in_proj, qk_norm] + att + [local_mixers, out_proj, norm_ff] + back
    else:
        assert nblk == 4 and len(back) == 6
        def cast_next_weights():
            for src_ref, dst_ref in zip(cast_in, cast_out):
                dst_ref[...] = src_ref[...].astype(BF16)

        order = [back[0], norm_in, in_proj, cast_next_weights, back[1], qk_norm, back[2], att[0], att[1], back[3],
                 att[2], att[3], back[4], local_mixers, out_proj, back[5], norm_ff]
    for piece in order:
        piece()


def _layer_spec(shape):
    nd = len(shape)
    return pl.BlockSpec((None,) + shape, lambda *g: (g[-1][0],) + (0,) * nd, pipeline_mode=pl.Buffered(1))


def _whole_spec(shape):
    nd = len(shape)
    return pl.BlockSpec((None,) + shape, lambda *g: (0,) * (nd + 1), pipeline_mode=pl.Buffered(1))


def _layer_call(layer, h, kv_ctx, cs, mods, mod_row, lw, wts, next_w=(), *, tile, is_ctx):
    bsz, seq, _ = h.shape
    n_tiles = seq // tile
    n_total = bsz * n_tiles
    if is_ctx:
        grid = (bsz, n_tiles)
        front = lambda g: (g[0], g[1])
        back = front
        prefetch = (layer,)
    else:
        grid = (n_total + 1,)
        steps = np.arange(n_total + 1)
        ft = np.minimum(steps, n_total - 1)
        bt = np.maximum(steps - 1, 0)
        table = np.stack([ft // n_tiles, ft % n_tiles,
                          np.minimum((ft % n_tiles + 1) * (tile // BLOCK), seq // BLOCK - 1),
                          bt // n_tiles, bt % n_tiles, ft]).astype(np.int32)
        front = lambda g: (g[-2][0, g[0]], g[-2][1, g[0]])
        back = lambda g: (g[-2][3, g[0]], g[-2][4, g[0]])
        prefetch = (jnp.asarray(table), layer)
    mod_spec = lambda which: pl.BlockSpec(
        (None, None, 1, 6 * D_MODEL),
        lambda *g: (g[-1][0], which(g)[0] if mod_row is None else mod_row, 0, 0))
    in_specs = [pl.BlockSpec((None, tile, D_MODEL), lambda *g: front(g) + (0,))]
    args = [h]
    if not is_ctx:
        next_blk = lambda g: g[-2][2, g[0]]
        in_specs += [
            pl.BlockSpec((None, BLOCK, D_MODEL), lambda *g: (front(g)[0], next_blk(g), 0)),
            pl.BlockSpec((tile, 256), lambda *g: (front(g)[1], 0)),
            pl.BlockSpec((BLOCK, 256), lambda *g: (next_blk(g), 0)),
            pl.BlockSpec((None, kv_ctx.shape[1], 256), lambda *g: (front(g)[0], 0, 0)),
        ]
        args += [h, cs, cs, kv_ctx]
    in_specs.append(mod_spec(front))
    args.append(mods)
    if not is_ctx:
        in_specs.append(mod_spec(back))
        args.append(mods)
    in_specs += [
        pl.BlockSpec(memory_space=pltpu.SMEM),
        _layer_spec(lw["vecs"].shape[1:]),
        _layer_spec((N_SGU_GROUPS * CHUNK, CHUNK)),
        _layer_spec((CHUNK, 256)),
        _layer_spec((256, 256)),
    ]
    args += [lw["sink"], lw["vecs"], lw["w_sgu"], lw["b_sgu"], lw["w_pool"]]
    in_specs += [_whole_spec(w.shape[1:]) for w in wts]
    args += list(wts)
    assert not (is_ctx and next_w)
    chunk = lambda g: g[-2][5, g[0]]
    cast_shapes, cast_specs = [], []
    for w in next_w:
        rows, cols = w.shape[1] // n_total, w.shape[2]
        in_specs.append(pl.BlockSpec((None, rows, cols),
                                     lambda *g: (jnp.minimum(g[-1][0] + 1, DEPTH - 1), chunk(g), 0)))
        args.append(w)
        cast_shapes.append(jax.ShapeDtypeStruct((1,) + w.shape[1:], BF16))
        cast_specs.append(pl.BlockSpec((None, rows, cols), lambda *g: (0, chunk(g), 0)))
    tile_spec = pl.BlockSpec((None, tile, D_MODEL), lambda *g: back(g) + (0,))
    scratch = [pltpu.VMEM((tile if is_ctx else tile + EDGE, D_PROJ), F32),
               pltpu.VMEM((tile, D_MODEL), BF16),
               pltpu.VMEM((tile, D_FF), BF16),
               pltpu.VMEM((tile, D_MODEL), F32),
               pltpu.VMEM((tile, D_MODEL), BF16)]
    if is_ctx:
        out_shape = (jax.ShapeDtypeStruct(h.shape, F32), jax.ShapeDtypeStruct((bsz, seq, 256), F32))
        out_specs = (tile_spec, pl.BlockSpec((None, tile, 256), lambda *g: back(g) + (0,)))
        semantics = ("parallel", "parallel")
    else:
        out_shape = (jax.ShapeDtypeStruct(h.shape, F32),) + tuple(cast_shapes)
        out_specs = (tile_spec,) + tuple(cast_specs)
        scratch += [pltpu.VMEM((3, 4 * BLOCK, 3 * BLOCK), F32),
                    pltpu.VMEM((BLOCK, 256), BF16),
                    pltpu.VMEM((POOL_HALO, 512), F32)]
        semantics = ("arbitrary",)
    return pl.pallas_call(
        functools.partial(_layer_kernel, tile=tile, seq=seq, n_total=n_total, is_ctx=is_ctx, n_cast=len(next_w)),
        out_shape=out_shape,
        grid_spec=pltpu.PrefetchScalarGridSpec(
            num_scalar_prefetch=len(prefetch),
            grid=grid,
            in_specs=in_specs,
            out_specs=out_specs,
            scratch_shapes=scratch,
        ),
        compiler_params=pltpu.CompilerParams(dimension_semantics=semantics,
                                             vmem_limit_bytes=VMEM_LIMIT),
        name="layer_ctx" if is_ctx else "layer_lat",
    )(*prefetch, *args)


def _ctx_kv_kernel(l_ref, h_ref, mod_ref, vec_ref, w_ref, kv_ref):
    mod = mod_ref[...]
    vv = _vec_views(vec_ref)
    a = _modulated_norm(h_ref[...], vv["nmix"][...], mod[:, 0:D_MODEL], mod[:, D_MODEL:2 * D_MODEL])
    kv = jnp.dot(a, w_ref[...], preferred_element_type=F32)
    kv_ref[:, 0:128] = _head_rms(kv[:, 0:128], vv["kn"][...])
    kv_ref[:, 128:256] = kv[:, 128:256]


def _ctx_kv_call(layer, h, mods, mod_row, lw, w_in_b):
    bsz, seq, _ = h.shape
    return pl.pallas_call(
        _ctx_kv_kernel,
        out_shape=jax.ShapeDtypeStruct((bsz, seq, 256), F32),
        grid_spec=pltpu.PrefetchScalarGridSpec(
            num_scalar_prefetch=1,
            grid=(bsz, 1),
            in_specs=[
                pl.BlockSpec((None, seq, D_MODEL), lambda b, j, l: (b, 0, 0)),
                pl.BlockSpec((None, None, 1, 6 * D_MODEL), lambda b, j, l: (l[0], mod_row, 0, 0)),
                _layer_spec(lw["vecs"].shape[1:]),
                pl.BlockSpec((None, D_MODEL, 256), lambda b, j, l: (0, 0, K_OFF // 256),
                             pipeline_mode=pl.Buffered(1)),
            ],
            out_specs=pl.BlockSpec((None, seq, 256), lambda b, j, l: (b, 0, 0)),
        ),
        compiler_params=pltpu.CompilerParams(dimension_semantics=("parallel", "parallel"),
                                             vmem_limit_bytes=VMEM_LIMIT),
        name="ctx_kv",
    )(layer, h, mods, lw["vecs"], w_in_b)


def _rope_table(length):
    rows = length // GRID_W
    row = np.repeat(np.arange(rows), GRID_W).astype(np.float32)
    col = np.tile(np.arange(GRID_W), rows).astype(np.float32)
    n_freq = HEAD_DIM // 4
    inv = jnp.asarray(ROPE_THETA, F32) ** (-jnp.arange(n_freq, dtype=F32) / n_freq)
    ang_r = jnp.asarray(row)[:, None] * inv[None, :]
    ang_c = jnp.asarray(col)[:, None] * inv[None, :]
    ang = jnp.concatenate([ang_r, ang_r, ang_c, ang_c], axis=-1)
    sign = jnp.asarray(np.where(np.arange(HEAD_DIM) % 32 < 16, -1.0, 1.0), F32)
    return jnp.concatenate([jnp.tile(jnp.cos(ang), (1, 2)), jnp.tile(jnp.sin(ang) * sign, (1, 2))], axis=-1)


def _pack_vectors(conv_w, norm_mix, norm_ff, q_norm, k_norm, sgu_norm, pool_scale):
    depth = conv_w.shape[0]
    pad = lambda a: jnp.pad(a, ((0, 0), (0, 0), (0, D_MODEL - a.shape[-1])))
    row5 = jnp.concatenate([jnp.tile(q_norm, (1, N_Q_HEADS)), jnp.tile(k_norm, (1, N_KV_HEADS)),
                            jnp.zeros((depth, 128), F32), sgu_norm, pool_scale], axis=-1)
    return jnp.concatenate([pad(conv_w), norm_mix[:, None, :], norm_ff[:, None, :], row5[:, None, :],
                            jnp.zeros((depth, 2, D_MODEL), F32)], axis=1)


def kernel(x, c, ctx, c_ctx, norm_mix, norm_ff, w_ada, b_ada, w_in, w_out, conv_w, q_norm, k_norm, sink,
           sgu_norm, w_sgu, b_sgu, w_pool, pool_scale, w_ff1, w_ff2):
    bsz, seq, _ = x.shape
    ctx_len = ctx.shape[1]
    assert bsz + 1 <= MOD_ROWS and seq % LAT_TILE == 0 and ctx_len % BLOCK == 0

    big_w = (w_in, w_out, w_ff1, w_ff2)
    cc = jnp.concatenate([c, c_ctx[None, :], jnp.zeros((MOD_ROWS - bsz - 1, D_MODEL), F32)], axis=0)
    mods, wts = _ada_call(cc, w_ada, b_ada, big_w)
    mods = mods.reshape(DEPTH, MOD_ROWS, 1, 6 * D_MODEL)

    eye = jnp.eye(len(POOL_WINDOWS), dtype=F32)
    lw = dict(
        sink=sink,
        vecs=_pack_vectors(conv_w, norm_mix, norm_ff, q_norm, k_norm, sgu_norm, pool_scale),
        w_sgu=w_sgu.reshape(DEPTH, N_SGU_GROUPS * CHUNK, CHUNK).astype(BF16),
        b_sgu=jnp.repeat(jnp.swapaxes(b_sgu, 1, 2), HEAD_DIM, axis=2),
        w_pool=jnp.einsum("lgcd,gh->lgchd", w_pool, eye).reshape(DEPTH, 256, 256).astype(BF16),
    )
    cs = _rope_table(seq)

    h_lat, h_ctx = x, ctx
    for l in range(DEPTH):
        layer = jnp.full((1,), l, jnp.int32)
        last = l == DEPTH - 1
        if last:
            kv_ctx = _ctx_kv_call(layer, h_ctx, mods, bsz, lw, wts[0])
        else:
            h_ctx, kv_ctx = _layer_call(layer, h_ctx, None, None, mods, bsz, lw, wts, tile=ctx_len, is_ctx=True)
        outs = _layer_call(layer, h_lat, kv_ctx, cs, mods, None, lw, wts, () if last else big_w,
                           tile=LAT_TILE, is_ctx=False)
        h_lat, wts = outs[0], tuple(outs[1:])
    return h_lat
```

```python
import functools

import jax
import jax.numpy as jnp
import numpy as np
from jax import lax
from jax.experimental import pallas as pl
from jax.experimental.pallas import tpu as pltpu

D_MODEL = 1024
DEPTH = 4
GRID_W = 64
HEAD_DIM = 64
N_Q_HEADS = 4
N_KV_HEADS = 2
BLOCK = 128
ROPE_THETA = 10000.0
CHUNK = 128
N_SGU_GROUPS = 4
POOL_WINDOWS = (2, 4, 8, 16)
POOL_HALO = 8
EDGE = 16
D_FF = 4 * D_MODEL
EPS = 1e-6
D_PROJ = 2048

A_H, A_GB, A_GC = 0, 256, 512
Q_OFF, K_OFF, V_OFF = 768, 1024, 1152
C_U, C_V = 1280, 1536
D_OFF = 1792

NEG = -1e30
MOD_ROWS = 16
LAT_TILE = 512
VMEM_LIMIT = 58 * 1024 * 1024

F32 = jnp.float32
BF16 = jnp.bfloat16


def _rms(x, g):
    ms = jnp.mean(x * x, axis=-1, keepdims=True)
    return x * lax.rsqrt(ms + EPS) * g


def _lane_group(shape, width):
    return lax.broadcasted_iota(jnp.int32, shape, len(shape) - 1) // width


def _head_rms(x, g):
    n = x.shape[-1]
    r = lax.broadcasted_iota(jnp.int32, (n, n), 0) // HEAD_DIM
    c = lax.broadcasted_iota(jnp.int32, (n, n), 1) // HEAD_DIM
    ones = jnp.where(r == c, 1.0, 0.0).astype(BF16)
    x2 = x * x
    hi = x2.astype(BF16)
    lo = (x2 - hi.astype(F32)).astype(BF16)
    ms = (jnp.dot(hi, ones, preferred_element_type=F32)
          + jnp.dot(lo, ones, preferred_element_type=F32)) * (1.0 / HEAD_DIM)
    return x * lax.rsqrt(ms + EPS) * g


def _swap_lanes(x, width):
    lane = lax.broadcasted_iota(jnp.int32, x.shape, x.ndim - 1)
    return jnp.where(lane % (2 * width) < width, pltpu.roll(x, 128 - width, x.ndim - 1),
                     pltpu.roll(x, width, x.ndim - 1))


def _rope(x, cs):
    cos, sin = cs[:, 0:128], cs[:, 128:256]
    parts = [x[:, s:s + 128] * cos + _swap_lanes(x[:, s:s + 128], 16) * sin for s in range(0, x.shape[-1], 128)]
    return parts[0] if len(parts) == 1 else jnp.concatenate(parts, axis=-1)


def _modulated_norm(x, g, shift, scale):
    return (_rms(x, g) * (1.0 + scale) + shift).astype(BF16)


def _ada_kernel(cc_ref, w_ref, b_ref, *refs):
    n_cast = (len(refs) - 1) // 2
    o_ref = refs[n_cast]
    cc = cc_ref[...]
    s = cc * jax.nn.sigmoid(cc)
    o_ref[...] = jnp.dot(s.astype(BF16), w_ref[...].astype(BF16), preferred_element_type=F32) + b_ref[...]
    for src_ref, dst_ref in zip(refs[:n_cast], refs[n_cast + 1:]):
        dst_ref[...] = src_ref[...].astype(BF16)


def _ada_call(cc, w_ada, b_ada, big_w):
    nblk = 4
    wb = 6 * D_MODEL // nblk
    steps = DEPTH * nblk
    chunk = lambda l, n: (0, l * nblk + n, 0)
    cast_in = [pl.BlockSpec((None, w.shape[1] // steps, w.shape[2]), chunk) for w in big_w]
    outs = pl.pallas_call(
        _ada_kernel,
        out_shape=(jax.ShapeDtypeStruct((DEPTH, MOD_ROWS, 6 * D_MODEL), F32),)
        + tuple(jax.ShapeDtypeStruct((1,) + w.shape[1:], BF16) for w in big_w),
        grid=(DEPTH, nblk),
        in_specs=[
            pl.BlockSpec((MOD_ROWS, D_MODEL), lambda l, n: (0, 0)),
            pl.BlockSpec((None, D_MODEL, wb), lambda l, n: (l, 0, n)),
            pl.BlockSpec((None, 1, wb), lambda l, n: (l, 0, n)),
        ] + cast_in,
        out_specs=(pl.BlockSpec((None, MOD_ROWS, wb), lambda l, n: (l, 0, n)),) + tuple(cast_in),
        compiler_params=pltpu.CompilerParams(dimension_semantics=("arbitrary", "arbitrary"),
                                             vmem_limit_bytes=VMEM_LIMIT),
        name="ada_mod",
    )(cc, w_ada, b_ada.reshape(DEPTH, 1, 6 * D_MODEL), *big_w)
    return outs[0], tuple(outs[1:])


def _shift_rows(x, k):
    n = x.shape[0]
    return pltpu.roll(x, k % n, 0)


def _conv_mixer(p_ref, halo_prev, halo_next, conv_ref, tile):
    rows = pl.ds(0, tile)
    z = p_ref[rows, A_GC:A_GC + 256] * p_ref[rows, A_H:A_H + 256]
    z_ext = jnp.concatenate([halo_prev, z, halo_next], axis=0)
    z_prev = _shift_rows(z_ext, 1)[POOL_HALO:POOL_HALO + tile]
    z_next = _shift_rows(z_ext, -1)[POOL_HALO:POOL_HALO + tile]
    cw = conv_ref[...]
    y = cw[0:1] * z_prev + cw[1:2] * z + cw[2:3] * z_next
    return p_ref[rows, A_GB:A_GB + 256] * y


def _pool_mixer(p_ref, halo_prev, halo_next, wpool_ref, pscale_ref, tile, seq, tile_start):
    x = p_ref[pl.ds(0, tile), D_OFF:D_OFF + 256]
    x_ext = jnp.concatenate([halo_prev, x, halo_next], axis=0)
    s2 = _shift_rows(x_ext, 1) + x_ext
    s4 = _shift_rows(s2, 1) + _shift_rows(s2, -1)
    s8 = _shift_rows(s4, 2) + _shift_rows(s4, -2)
    s16 = _shift_rows(s8, 4) + _shift_rows(s8, -4)
    grp = _lane_group(x_ext.shape, HEAD_DIM)
    s = jnp.where(grp == 0, s2, jnp.where(grp == 1, s4, jnp.where(grp == 2, s8, s16)))
    s = s[POOL_HALO:POOL_HALO + tile]
    t = tile_start + lax.broadcasted_iota(jnp.int32, (tile, 256), 0)
    half = jnp.left_shift(1, _lane_group((tile, 256), HEAD_DIM))
    cnt = jnp.minimum(t + half, seq) - jnp.maximum(t - half, 0)
    d = s / cnt.astype(F32) - x
    y = jnp.dot(d.astype(BF16), wpool_ref[...], preferred_element_type=F32)
    return y * pscale_ref[...]


def _pair_halves(a, b, half):
    low = lax.broadcasted_iota(jnp.int32, a.shape, 1) < HEAD_DIM
    if half == 0:
        return jnp.where(low, a, pltpu.roll(b, HEAD_DIM, 1))
    return jnp.where(low, pltpu.roll(a, HEAD_DIM, 1), b)


def _sgu_mixer(p_ref, sgun_ref, wsgu_ref, bsgu_ref, tile):
    rows = pl.ds(0, tile)
    nchunk = tile // CHUNK
    assert nchunk % 2 == 0
    vn = _rms(p_ref[rows, C_V:C_V + 256], sgun_ref[...])
    x = [[vn[c * CHUNK:(c + 1) * CHUNK, t * 128:(t + 1) * 128] for t in range(2)] for c in range(nchunk)]
    zg = []
    for g in range(N_SGU_GROUPS):
        t, half = divmod(g, 2)
        rhs = jnp.concatenate([_pair_halves(x[c][t], x[c + 1][t], half) for c in range(0, nchunk, 2)], axis=1)
        w_g = wsgu_ref[g * CHUNK:(g + 1) * CHUNK, :]
        zg.append(jnp.dot(w_g, rhs.astype(BF16), preferred_element_type=F32))
    bias = bsgu_ref[...]
    outs = []
    for c in range(nchunk):
        ct, half = divmod(c, 2)
        tiles = [_pair_halves(zg[2 * t][:, ct * 128:(ct + 1) * 128], zg[2 * t + 1][:, ct * 128:(ct + 1) * 128], half)
                 for t in range(2)]
        outs.append(jnp.concatenate(tiles, axis=1) + bias)
    return p_ref[rows, C_U:C_U + 256] * jnp.concatenate(outs, axis=0)


def _attention_block(qb, segs, sink):
    low = lax.broadcasted_iota(jnp.int32, (BLOCK, 128), 1) < HEAD_DIM
    t0, t1 = qb[:, 0:128], qb[:, 128:256]
    rows = [jnp.where(low, t0, 0.0), jnp.where(low, pltpu.roll(t0, HEAD_DIM, 1), 0.0),
            jnp.where(low, 0.0, pltpu.roll(t1, HEAD_DIM, 1)), jnp.where(low, 0.0, t1)]
    q4 = jnp.concatenate(rows, axis=0).astype(BF16)
    rb = lax.broadcasted_iota(jnp.int32, (4 * BLOCK, 1), 0) // BLOCK
    sink_col = jnp.where(rb == 0, sink[0], jnp.where(rb == 1, sink[1], jnp.where(rb == 2, sink[2], sink[3])))
    scores = []
    m = sink_col
    for k, _, bias in segs:
        s = lax.dot_general(q4, k, (((1,), (1,)), ((), ())), preferred_element_type=F32)
        if bias is not None:
            s = s + bias
        scores.append(s)
        m = jnp.maximum(m, jnp.max(s, axis=-1, keepdims=True))
    denom = jnp.exp(sink_col - m)
    acc = None
    for s, (_, v, _) in zip(scores, segs):
        pr = jnp.exp(s - m)
        denom = denom + jnp.sum(pr, axis=-1, keepdims=True)
        o = jnp.dot(pr.astype(BF16), v, preferred_element_type=F32)
        acc = o if acc is None else acc + o
    acc = acc / denom
    a = [acc[i * BLOCK:(i + 1) * BLOCK] for i in range(4)]
    out0 = jnp.where(low, a[0], pltpu.roll(a[1], HEAD_DIM, 1))
    out1 = jnp.where(low, pltpu.roll(a[2], HEAD_DIM, 1), a[3])
    return jnp.concatenate([out0, out1], axis=-1)


def _vec_views(vec_ref):
    return dict(conv=vec_ref.at[0:3, 0:256], nmix=vec_ref.at[3:4, :], nff=vec_ref.at[4:5, :],
                qn=vec_ref.at[5:6, 0:256], kn=vec_ref.at[5:6, 256:384], sgun=vec_ref.at[5:6, 512:768],
                pscale=vec_ref.at[5:6, 768:1024])


def _layer_kernel(*refs, tile, seq, n_total, is_ctx, n_cast):
    if is_ctx:
        (l_ref, h_ref, mod_ref, sink_ref, vec_ref, wsgu_ref, bsgu_ref, wpool_ref, win_ref, wout_ref, w1_ref, w2_ref,
         o_ref, kv_ref, p_ref, ycat_ref, hid_ref, h1_ref, f_ref) = refs
        modb_ref = mod_ref
        j = pl.program_id(1)
    else:
        (sched_ref, l_ref, h_ref, hn_ref, cs_ref, csn_ref, kvc_ref, mod_ref, modb_ref, sink_ref, vec_ref, wsgu_ref,
         bsgu_ref, wpool_ref, win_ref, wout_ref, w1_ref, w2_ref) = refs[:18]
        cast_in = refs[18:18 + n_cast]
        o_ref = refs[18 + n_cast]
        cast_out = refs[19 + n_cast:19 + 2 * n_cast]
        (p_ref, ycat_ref, hid_ref, h1_ref, f_ref, bias_ref, kvprev_ref, eprev_ref) = refs[19 + 2 * n_cast:]
        step = pl.program_id(0)
    vv = _vec_views(vec_ref)
    nmix_ref, nff_ref, qn_ref, kn_ref = vv["nmix"], vv["nff"], vv["qn"], vv["kn"]
    conv_ref, sgun_ref, pscale_ref = vv["conv"], vv["sgun"], vv["pscale"]
    if not is_ctx:

        @pl.when(step == 0)
        def _():
            h1_ref[...] = jnp.zeros_like(h1_ref)
            f_ref[...] = jnp.zeros_like(f_ref)
            kvprev_ref[...] = jnp.zeros_like(kvprev_ref)
            eprev_ref[...] = jnp.zeros_like(eprev_ref)
            r = lax.broadcasted_iota(jnp.int32, (4 * BLOCK, 3 * BLOCK), 0) % BLOCK
            col = lax.broadcasted_iota(jnp.int32, (4 * BLOCK, 3 * BLOCK), 1)
            seg = col // BLOCK
            jj = col % BLOCK
            band_prev = (seg == 0) & (jj >= r)
            band_next = (seg == 2) & (jj <= r)
            bias_ref[0] = jnp.where((seg == 1) | band_prev | band_next, 0.0, NEG)
            bias_ref[1] = jnp.where((seg == 1) | band_next, 0.0, NEG)
            bias_ref[2] = jnp.where((seg == 1) | band_prev, 0.0, NEG)

        j = sched_ref[1, step]
        has_prev = j > 0
        has_next = j < seq // tile - 1
    layer = l_ref[0]
    tile_start = j * tile
    mod = mod_ref[...]
    sh1, sc1, g1 = mod[:, 0:D_MODEL], mod[:, D_MODEL:2 * D_MODEL], mod[:, 2 * D_MODEL:3 * D_MODEL]
    rows = pl.ds(0, tile)
    nblk = tile // BLOCK
    st = {}

    ff1_cols = 512

    def ff1(c):
        u = jnp.dot(f_ref[...], w1_ref[:, c:c + ff1_cols], preferred_element_type=F32)
        u = jnp.maximum(u, 0.0)
        hid_ref[:, c:c + ff1_cols] = (u * u).astype(BF16)

    half = D_MODEL // 2

    def ff2(c):
        g2 = modb_ref[:, 5 * D_MODEL + c:5 * D_MODEL + c + half]
        y = jnp.dot(hid_ref[...], w2_ref[:, c:c + half], preferred_element_type=F32)
        o_ref[:, c:c + half] = h1_ref[:, c:c + half] + g2 * y

    back = [functools.partial(ff1, c) for c in range(0, D_FF, ff1_cols)]
    back += [functools.partial(ff2, c) for c in range(0, D_MODEL, half)]

    def norm_in():
        nmix = nmix_ref[...]
        a_main = _modulated_norm(h_ref[...], nmix, sh1, sc1)
        if is_ctx:
            st["a_ext"] = a_main
        else:
            a_next = _modulated_norm(hn_ref[...], nmix, sh1, sc1)
            st["a_ext"] = jnp.concatenate([a_main, a_next[0:EDGE]], axis=0)
            st["a_next"] = a_next

    def in_proj():
        p_ref[...] = jnp.dot(st["a_ext"], win_ref[...], preferred_element_type=F32)
        if not is_ctx:
            st["kv_next"] = jnp.dot(st["a_next"], win_ref[:, K_OFF:K_OFF + 256], preferred_element_type=F32)

    def qk_norm():
        q = _head_rms(p_ref[rows, Q_OFF:K_OFF], qn_ref[...])
        k = _head_rms(p_ref[rows, K_OFF:V_OFF], kn_ref[...])
        if not is_ctx:
            q = _rope(q, cs_ref[...])
            k = _rope(k, cs_ref[...])
        p_ref[rows, Q_OFF:K_OFF] = q * (HEAD_DIM ** -0.5)
        k_tile = k.astype(BF16)
        v_tile = p_ref[rows, V_OFF:V_OFF + 128].astype(BF16)
        if is_ctx:
            kv_ref[:, 0:128] = k
            kv_ref[:, 128:256] = p_ref[rows, V_OFF:V_OFF + 128]
            st["k_ext"], st["v_ext"] = k_tile, v_tile
        else:
            kv_next = st["kv_next"]
            k_next = _rope(_head_rms(kv_next[:, 0:128], kn_ref[...]), csn_ref[...]).astype(BF16)
            v_next = kv_next[:, 128:256].astype(BF16)
            st["k_ext"] = jnp.concatenate([kvprev_ref[:, 0:128], k_tile, k_next], axis=0)
            st["v_ext"] = jnp.concatenate([kvprev_ref[:, 128:256], v_tile, v_next], axis=0)
            kvprev_ref[:, 0:128] = k_tile[tile - BLOCK:tile]
            kvprev_ref[:, 128:256] = v_tile[tile - BLOCK:tile]
            st["k_ctx"] = kvc_ref[:, 0:128].astype(BF16)
            st["v_ctx"] = kvc_ref[:, 128:256].astype(BF16)

    def attention(i):
        sink = [sink_ref[layer, n] for n in range(N_Q_HEADS)]
        qb = p_ref[pl.ds(i * BLOCK, BLOCK), Q_OFF:K_OFF]
        if is_ctx:
            segs = [(st["k_ext"], st["v_ext"], None)]
        else:
            which = 0
            if i == 0:
                which = jnp.where(has_prev, 0, 1)
            if i == nblk - 1:
                which = jnp.where(has_next, 0, 2)
            segs = [(st["k_ext"][i * BLOCK:(i + 3) * BLOCK], st["v_ext"][i * BLOCK:(i + 3) * BLOCK],
                     bias_ref[which]),
                    (st["k_ctx"], st["v_ctx"], None)]
        ycat_ref[i * BLOCK:(i + 1) * BLOCK, 256:512] = _attention_block(qb, segs, sink).astype(BF16)

    def local_mixers():
        if is_ctx:
            zeros = jnp.zeros((POOL_HALO, 256), F32)
            z_prev = z_next = x_prev = x_next = zeros
        else:
            ne = pl.ds(tile, POOL_HALO)
            le = pl.ds(tile - POOL_HALO, POOL_HALO)
            z_prev = jnp.where(has_prev, eprev_ref[:, 0:256], 0.0)
            z_next = jnp.where(has_next, p_ref[ne, A_GC:A_GC + 256] * p_ref[ne, A_H:A_H + 256], 0.0)
            x_prev = jnp.where(has_prev, eprev_ref[:, 256:512], 0.0)
            x_next = jnp.where(has_next, p_ref[ne, D_OFF:D_OFF + 256], 0.0)
            eprev_ref[:, 0:256] = p_ref[le, A_GC:A_GC + 256] * p_ref[le, A_H:A_H + 256]
            eprev_ref[:, 256:512] = p_ref[le, D_OFF:D_OFF + 256]
        ycat_ref[:, 0:256] = _conv_mixer(p_ref, z_prev, z_next, conv_ref, tile).astype(BF16)
        ycat_ref[:, 512:768] = _sgu_mixer(p_ref, sgun_ref, wsgu_ref, bsgu_ref, tile).astype(BF16)
        ycat_ref[:, 768:1024] = _pool_mixer(p_ref, x_prev, x_next, wpool_ref, pscale_ref,
                                            tile, seq, tile_start).astype(BF16)

    def out_proj():
        st["h1"] = h_ref[...] + g1 * jnp.dot(ycat_ref[...], wout_ref[...], preferred_element_type=F32)

    def norm_ff():
        h1 = st["h1"]
        h1_ref[...] = h1
        f_ref[...] = _modulated_norm(h1, nff_ref[...], mod[:, 3 * D_MODEL:4 * D_MODEL],
                                     mod[:, 4 * D_MODEL:5 * D_MODEL])

    att = [functools.partial(attention, i) for i in range(nblk)]
    if is_ctx:
        order = [norm_in, in_proj, qk_norm] + att + [local_mixers, out_proj, norm_ff] + back
    else:
        assert nblk == 4 and len(back) == 10
        def cast_next_weights():
            for src_ref, dst_ref in zip(cast_in, cast_out):
                dst_ref[...] = src_ref[...].astype(BF16)

        order = [back[0], back[1], norm_in, in_proj, cast_next_weights, back[2], qk_norm, back[3], att[0], back[4],
                 att[1], back[5], att[2], back[6], att[3], back[7], local_mixers, back[8], out_proj, back[9], norm_ff]
    for piece in order:
        piece()


def _layer_spec(shape):
    nd = len(shape)
    return pl.BlockSpec((None,) + shape, lambda *g: (g[-1][0],) + (0,) * nd, pipeline_mode=pl.Buffered(1))


def _whole_spec(shape):
    nd = len(shape)
    return pl.BlockSpec((None,) + shape, lambda *g: (0,) * (nd + 1), pipeline_mode=pl.Buffered(1))


def _layer_call(layer, h, kv_ctx, cs, mods, mod_row, lw, wts, next_w=(), *, tile, is_ctx):
    bsz, seq, _ = h.shape
    n_tiles = seq // tile
    n_total = bsz * n_tiles
    if is_ctx:
        grid = (bsz, n_tiles)
        front = lambda g: (g[0], g[1])
        back = front
        prefetch = (layer,)
    else:
        grid = (n_total + 1,)
        steps = np.arange(n_total + 1)
        ft = np.minimum(steps, n_total - 1)
        bt = np.maximum(steps - 1, 0)
        table = np.stack([ft // n_tiles, ft % n_tiles,
                          np.minimum((ft % n_tiles + 1) * (tile // BLOCK), seq // BLOCK - 1),
                          bt // n_tiles, bt % n_tiles, ft]).astype(np.int32)
        front = lambda g: (g[-2][0, g[0]], g[-2][1, g[0]])
        back = lambda g: (g[-2][3, g[0]], g[-2][4, g[0]])
        prefetch = (jnp.asarray(table), layer)
    mod_spec = lambda which: pl.BlockSpec(
        (None, None, 1, 6 * D_MODEL),
        lambda *g: (g[-1][0], which(g)[0] if mod_row is None else mod_row, 0, 0))
    in_specs = [pl.BlockSpec((None, tile, D_MODEL), lambda *g: front(g) + (0,))]
    args = [h]
    if not is_ctx:
        next_blk = lambda g: g[-2][2, g[0]]
        in_specs += [
            pl.BlockSpec((None, BLOCK, D_MODEL), lambda *g: (front(g)[0], next_blk(g), 0)),
            pl.BlockSpec((tile, 256), lambda *g: (front(g)[1], 0)),
            pl.BlockSpec((BLOCK, 256), lambda *g: (next_blk(g), 0)),
            pl.BlockSpec((None, kv_ctx.shape[1], 256), lambda *g: (front(g)[0], 0, 0)),
        ]
        args += [h, cs, cs, kv_ctx]
    in_specs.append(mod_spec(front))
    args.append(mods)
    if not is_ctx:
        in_specs.append(mod_spec(back))
        args.append(mods)
    in_specs += [
        pl.BlockSpec(memory_space=pltpu.SMEM),
        _layer_spec(lw["vecs"].shape[1:]),
        _layer_spec((N_SGU_GROUPS * CHUNK, CHUNK)),
        _layer_spec((CHUNK, 256)),
        _layer_spec((256, 256)),
    ]
    args += [lw["sink"], lw["vecs"], lw["w_sgu"], lw["b_sgu"], lw["w_pool"]]
    in_specs += [_whole_spec(w.shape[1:]) for w in wts]
    args += list(wts)
    assert not (is_ctx and next_w)
    chunk = lambda g: g[-2][5, g[0]]
    cast_shapes, cast_specs = [], []
    for w in next_w:
        rows, cols = w.shape[1] // n_total, w.shape[2]
        in_specs.append(pl.BlockSpec((None, rows, cols),
                                     lambda *g: (jnp.minimum(g[-1][0] + 1, DEPTH - 1), chunk(g), 0)))
        args.append(w)
        cast_shapes.append(jax.ShapeDtypeStruct((1,) + w.shape[1:], BF16))
        cast_specs.append(pl.BlockSpec((None, rows, cols), lambda *g: (0, chunk(g), 0)))
    tile_spec = pl.BlockSpec((None, tile, D_MODEL), lambda *g: back(g) + (0,))
    scratch = [pltpu.VMEM((tile if is_ctx else tile + EDGE, D_PROJ), F32),
               pltpu.VMEM((tile, D_MODEL), BF16),
               pltpu.VMEM((tile, D_FF), BF16),
               pltpu.VMEM((tile, D_MODEL), F32),
               pltpu.VMEM((tile, D_MODEL), BF16)]
    if is_ctx:
        out_shape = (jax.ShapeDtypeStruct(h.shape, F32), jax.ShapeDtypeStruct((bsz, seq, 256), F32))
        out_specs = (tile_spec, pl.BlockSpec((None, tile, 256), lambda *g: back(g) + (0,)))
        semantics = ("parallel", "parallel")
    else:
        out_shape = (jax.ShapeDtypeStruct(h.shape, F32),) + tuple(cast_shapes)
        out_specs = (tile_spec,) + tuple(cast_specs)
        scratch += [pltpu.VMEM((3, 4 * BLOCK, 3 * BLOCK), F32),
                    pltpu.VMEM((BLOCK, 256), BF16),
                    pltpu.VMEM((POOL_HALO, 512), F32)]
        semantics = ("arbitrary",)
    return pl.pallas_call(
        functools.partial(_layer_kernel, tile=tile, seq=seq, n_total=n_total, is_ctx=is_ctx, n_cast=len(next_w)),
        out_shape=out_shape,
        grid_spec=pltpu.PrefetchScalarGridSpec(
            num_scalar_prefetch=len(prefetch),
            grid=grid,
            in_specs=in_specs,
            out_specs=out_specs,
            scratch_shapes=scratch,
        ),
        compiler_params=pltpu.CompilerParams(dimension_semantics=semantics,
                                             vmem_limit_bytes=VMEM_LIMIT),
        name="layer_ctx" if is_ctx else "layer_lat",
    )(*prefetch, *args)


def _ctx_kv_kernel(l_ref, h_ref, mod_ref, vec_ref, w_ref, kv_ref):
    mod = mod_ref[...]
    vv = _vec_views(vec_ref)
    a = _modulated_norm(h_ref[...], vv["nmix"][...], mod[:, 0:D_MODEL], mod[:, D_MODEL:2 * D_MODEL])
    kv = jnp.dot(a, w_ref[...], preferred_element_type=F32)
    kv_ref[:, 0:128] = _head_rms(kv[:, 0:128], vv["kn"][...])
    kv_ref[:, 128:256] = kv[:, 128:256]


def _ctx_kv_call(layer, h, mods, mod_row, lw, w_in_b):
    bsz, seq, _ = h.shape
    return pl.pallas_call(
        _ctx_kv_kernel,
        out_shape=jax.ShapeDtypeStruct((bsz, seq, 256), F32),
        grid_spec=pltpu.PrefetchScalarGridSpec(
            num_scalar_prefetch=1,
            grid=(bsz, 1),
            in_specs=[
                pl.BlockSpec((None, seq, D_MODEL), lambda b, j, l: (b, 0, 0)),
                pl.BlockSpec((None, None, 1, 6 * D_MODEL), lambda b, j, l: (l[0], mod_row, 0, 0)),
                _layer_spec(lw["vecs"].shape[1:]),
                pl.BlockSpec((None, D_MODEL, 256), lambda b, j, l: (0, 0, K_OFF // 256),
                             pipeline_mode=pl.Buffered(1)),
            ],
            out_specs=pl.BlockSpec((None, seq, 256), lambda b, j, l: (b, 0, 0)),
        ),
        compiler_params=pltpu.CompilerParams(dimension_semantics=("parallel", "parallel"),
                                             vmem_limit_bytes=VMEM_LIMIT),
        name="ctx_kv",
    )(layer, h, mods, lw["vecs"], w_in_b)


def _rope_table(length):
    rows = length // GRID_W
    row = np.repeat(np.arange(rows), GRID_W).astype(np.float32)
    col = np.tile(np.arange(GRID_W), rows).astype(np.float32)
    n_freq = HEAD_DIM // 4
    inv = jnp.asarray(ROPE_THETA, F32) ** (-jnp.arange(n_freq, dtype=F32) / n_freq)
    ang_r = jnp.asarray(row)[:, None] * inv[None, :]
    ang_c = jnp.asarray(col)[:, None] * inv[None, :]
    ang = jnp.concatenate([ang_r, ang_r, ang_c, ang_c], axis=-1)
    sign = jnp.asarray(np.where(np.arange(HEAD_DIM) % 32 < 16, -1.0, 1.0), F32)
    return jnp.concatenate([jnp.tile(jnp.cos(ang), (1, 2)), jnp.tile(jnp.sin(ang) * sign, (1, 2))], axis=-1)


def _pack_vectors(conv_w, norm_mix, norm_ff, q_norm, k_norm, sgu_norm, pool_scale):
    depth = conv_w.shape[0]
    pad = lambda a: jnp.pad(a, ((0, 0), (0, 0), (0, D_MODEL - a.shape[-1])))
    row5 = jnp.concatenate([jnp.tile(q_norm, (1, N_Q_HEADS)), jnp.tile(k_norm, (1, N_KV_HEADS)),
                            jnp.zeros((depth, 128), F32), sgu_norm, pool_scale], axis=-1)
    return jnp.concatenate([pad(conv_w), norm_mix[:, None, :], norm_ff[:, None, :], row5[:, None, :],
                            jnp.zeros((depth, 2, D_MODEL), F32)], axis=1)


def kernel(x, c, ctx, c_ctx, norm_mix, norm_ff, w_ada, b_ada, w_in, w_out, conv_w, q_norm, k_norm, sink,
           sgu_norm, w_sgu, b_sgu, w_pool, pool_scale, w_ff1, w_ff2):
    bsz, seq, _ = x.shape
    ctx_len = ctx.shape[1]
    assert bsz + 1 <= MOD_ROWS and seq % LAT_TILE == 0 and ctx_len % BLOCK == 0

    big_w = (w_in, w_out, w_ff1, w_ff2)
    cc = jnp.concatenate([c, c_ctx[None, :], jnp.zeros((MOD_ROWS - bsz - 1, D_MODEL), F32)], axis=0)
    mods, wts = _ada_call(cc, w_ada, b_ada, big_w)
    mods = mods.reshape(DEPTH, MOD_ROWS, 1, 6 * D_MODEL)

    eye = jnp.eye(len(POOL_WINDOWS), dtype=F32)
    lw = dict(
        sink=sink,
        vecs=_pack_vectors(conv_w, norm_mix, norm_ff, q_norm, k_norm, sgu_norm, pool_scale),
        w_sgu=w_sgu.reshape(DEPTH, N_SGU_GROUPS * CHUNK, CHUNK).astype(BF16),
        b_sgu=jnp.repeat(jnp.swapaxes(b_sgu, 1, 2), HEAD_DIM, axis=2),
        w_pool=jnp.einsum("lgcd,gh->lgchd", w_pool, eye).reshape(DEPTH, 256, 256).astype(BF16),
    )
    cs = _rope_table(seq)

    h_lat, h_ctx = x, ctx
    for l in range(DEPTH):
        layer = jnp.full((1,), l, jnp.int32)
        last = l == DEPTH - 1
        if last:
            kv_ctx = _ctx_kv_call(layer, h_ctx, mods, bsz, lw, wts[0])
        else:
            h_ctx, kv_ctx = _layer_call(layer, h_ctx, None, None, mods, bsz, lw, wts, tile=ctx_len, is_ctx=True)
        outs = _layer_call(layer, h_lat, kv_ctx, cs, mods, None, lw, wts, () if last else big_w,
                           tile=LAT_TILE, is_ctx=False)
        h_lat, wts = outs[0], tuple(outs[1:])
    return h_lat
```

```python
import functools

import jax
import jax.numpy as jnp
import numpy as np
from jax import lax
from jax.experimental import pallas as pl
from jax.experimental.pallas import tpu as pltpu

D_MODEL = 1024
DEPTH = 4
GRID_W = 64
HEAD_DIM = 64
N_Q_HEADS = 4
N_KV_HEADS = 2
BLOCK = 128
ROPE_THETA = 10000.0
CHUNK = 128
N_SGU_GROUPS = 4
POOL_WINDOWS = (2, 4, 8, 16)
POOL_HALO = 8
EDGE = 16
D_FF = 4 * D_MODEL
EPS = 1e-6
D_PROJ = 2048

A_H, A_GB, A_GC = 0, 256, 512
Q_OFF, K_OFF, V_OFF = 768, 1024, 1152
C_U, C_V = 1280, 1536
D_OFF = 1792

NEG = -1e30
MOD_ROWS = 16
LAT_TILE = 512
VMEM_LIMIT = 58 * 1024 * 1024

F32 = jnp.float32
BF16 = jnp.bfloat16


def _rms(x, g):
    ms = jnp.mean(x * x, axis=-1, keepdims=True)
    return x * lax.rsqrt(ms + EPS) * g


def _lane_group(shape, width):
    return lax.broadcasted_iota(jnp.int32, shape, len(shape) - 1) // width


def _head_rms(x, g):
    n = x.shape[-1]
    r = lax.broadcasted_iota(jnp.int32, (n, n), 0) // HEAD_DIM
    c = lax.broadcasted_iota(jnp.int32, (n, n), 1) // HEAD_DIM
    ones = jnp.where(r == c, 1.0, 0.0).astype(BF16)
    x2 = x * x
    hi = x2.astype(BF16)
    lo = (x2 - hi.astype(F32)).astype(BF16)
    ms = (jnp.dot(hi, ones, preferred_element_type=F32)
          + jnp.dot(lo, ones, preferred_element_type=F32)) * (1.0 / HEAD_DIM)
    return x * lax.rsqrt(ms + EPS) * g


def _swap_lanes(x, width):
    lane = lax.broadcasted_iota(jnp.int32, x.shape, x.ndim - 1)
    return jnp.where(lane % (2 * width) < width, pltpu.roll(x, 128 - width, x.ndim - 1),
                     pltpu.roll(x, width, x.ndim - 1))


def _rope(x, cs):
    cos, sin = cs[:, 0:128], cs[:, 128:256]
    parts = [x[:, s:s + 128] * cos + _swap_lanes(x[:, s:s + 128], 16) * sin for s in range(0, x.shape[-1], 128)]
    return parts[0] if len(parts) == 1 else jnp.concatenate(parts, axis=-1)


def _modulated_norm(x, g, shift, scale):
    return (_rms(x, g) * (1.0 + scale) + shift).astype(BF16)


def _ada_kernel(cc_ref, w_ref, b_ref, *refs):
    n_cast = (len(refs) - 1) // 2
    o_ref = refs[n_cast]
    cc = cc_ref[...]
    s = cc * jax.nn.sigmoid(cc)
    o_ref[...] = jnp.dot(s.astype(BF16), w_ref[...].astype(BF16), preferred_element_type=F32) + b_ref[...]
    for src_ref, dst_ref in zip(refs[:n_cast], refs[n_cast + 1:]):
        dst_ref[...] = src_ref[...].astype(BF16)


def _ada_call(cc, w_ada, b_ada, big_w):
    nblk = 4
    wb = 6 * D_MODEL // nblk
    steps = DEPTH * nblk
    chunk = lambda l, n: (0, l * nblk + n, 0)
    cast_in = [pl.BlockSpec((None, w.shape[1] // steps, w.shape[2]), chunk) for w in big_w]
    outs = pl.pallas_call(
        _ada_kernel,
        out_shape=(jax.ShapeDtypeStruct((DEPTH, MOD_ROWS, 6 * D_MODEL), F32),)
        + tuple(jax.ShapeDtypeStruct((1,) + w.shape[1:], BF16) for w in big_w),
        grid=(DEPTH, nblk),
        in_specs=[
            pl.BlockSpec((MOD_ROWS, D_MODEL), lambda l, n: (0, 0)),
            pl.BlockSpec((None, D_MODEL, wb), lambda l, n: (l, 0, n)),
            pl.BlockSpec((None, 1, wb), lambda l, n: (l, 0, n)),
        ] + cast_in,
        out_specs=(pl.BlockSpec((None, MOD_ROWS, wb), lambda l, n: (l, 0, n)),) + tuple(cast_in),
        compiler_params=pltpu.CompilerParams(dimension_semantics=("arbitrary", "arbitrary"),
                                             vmem_limit_bytes=VMEM_LIMIT),
        name="ada_mod",
    )(cc, w_ada, b_ada.reshape(DEPTH, 1, 6 * D_MODEL), *big_w)
    return outs[0], tuple(outs[1:])


def _shift_rows(x, k):
    n = x.shape[0]
    return pltpu.roll(x, k % n, 0)


def _conv_mixer(p_ref, halo_prev, halo_next, conv_ref, tile):
    rows = pl.ds(0, tile)
    z = p_ref[rows, A_GC:A_GC + 256] * p_ref[rows, A_H:A_H + 256]
    z_ext = jnp.concatenate([halo_prev, z, halo_next], axis=0)
    z_prev = _shift_rows(z_ext, 1)[POOL_HALO:POOL_HALO + tile]
    z_next = _shift_rows(z_ext, -1)[POOL_HALO:POOL_HALO + tile]
    cw = conv_ref[...]
    y = cw[0:1] * z_prev + cw[1:2] * z + cw[2:3] * z_next
    return p_ref[rows, A_GB:A_GB + 256] * y


def _pool_mixer(p_ref, halo_prev, halo_next, wpool_ref, pscale_ref, tile, seq, tile_start):
    x = p_ref[pl.ds(0, tile), D_OFF:D_OFF + 256]
    x_ext = jnp.concatenate([halo_prev, x, halo_next], axis=0)
    s2 = _shift_rows(x_ext, 1) + x_ext
    s4 = _shift_rows(s2, 1) + _shift_rows(s2, -1)
    s8 = _shift_rows(s4, 2) + _shift_rows(s4, -2)
    s16 = _shift_rows(s8, 4) + _shift_rows(s8, -4)
    grp = _lane_group(x_ext.shape, HEAD_DIM)
    s = jnp.where(grp == 0, s2, jnp.where(grp == 1, s4, jnp.where(grp == 2, s8, s16)))
    s = s[POOL_HALO:POOL_HALO + tile]
    t = tile_start + lax.broadcasted_iota(jnp.int32, (tile, 256), 0)
    half = jnp.left_shift(1, _lane_group((tile, 256), HEAD_DIM))
    cnt = jnp.minimum(t + half, seq) - jnp.maximum(t - half, 0)
    d = s / cnt.astype(F32) - x
    y = jnp.dot(d.astype(BF16), wpool_ref[...], preferred_element_type=F32)
    return y * pscale_ref[...]


def _pair_halves(a, b, half):
    low = lax.broadcasted_iota(jnp.int32, a.shape, 1) < HEAD_DIM
    if half == 0:
        return jnp.where(low, a, pltpu.roll(b, HEAD_DIM, 1))
    return jnp.where(low, pltpu.roll(a, HEAD_DIM, 1), b)


def _sgu_mixer(p_ref, sgun_ref, wsgu_ref, bsgu_ref, tile):
    rows = pl.ds(0, tile)
    nchunk = tile // CHUNK
    assert nchunk % 2 == 0
    vn = _rms(p_ref[rows, C_V:C_V + 256], sgun_ref[...])
    x = [[vn[c * CHUNK:(c + 1) * CHUNK, t * 128:(t + 1) * 128] for t in range(2)] for c in range(nchunk)]
    zg = []
    for g in range(N_SGU_GROUPS):
        t, half = divmod(g, 2)
        rhs = jnp.concatenate([_pair_halves(x[c][t], x[c + 1][t], half) for c in range(0, nchunk, 2)], axis=1)
        w_g = wsgu_ref[g * CHUNK:(g + 1) * CHUNK, :]
        zg.append(jnp.dot(w_g, rhs.astype(BF16), preferred_element_type=F32))
    bias = bsgu_ref[...]
    outs = []
    for c in range(nchunk):
        ct, half = divmod(c, 2)
        tiles = [_pair_halves(zg[2 * t][:, ct * 128:(ct + 1) * 128], zg[2 * t + 1][:, ct * 128:(ct + 1) * 128], half)
                 for t in range(2)]
        outs.append(jnp.concatenate(tiles, axis=1) + bias)
    return p_ref[rows, C_U:C_U + 256] * jnp.concatenate(outs, axis=0)


def _attention_scores(qb, segs, sink):
    low = lax.broadcasted_iota(jnp.int32, (BLOCK, 128), 1) < HEAD_DIM
    t0, t1 = qb[:, 0:128], qb[:, 128:256]
    rows = [jnp.where(low, t0, 0.0), jnp.where(low, pltpu.roll(t0, HEAD_DIM, 1), 0.0),
            jnp.where(low, 0.0, pltpu.roll(t1, HEAD_DIM, 1)), jnp.where(low, 0.0, t1)]
    q4 = jnp.concatenate(rows, axis=0).astype(BF16)
    rb = lax.broadcasted_iota(jnp.int32, (4 * BLOCK, 1), 0) // BLOCK
    sink_col = jnp.where(rb == 0, sink[0], jnp.where(rb == 1, sink[1], jnp.where(rb == 2, sink[2], sink[3])))
    scores = []
    for k, _, bias in segs:
        s = lax.dot_general(q4, k, (((1,), (1,)), ((), ())), preferred_element_type=F32)
        scores.append(s if bias is None else s + bias)
    return scores, sink_col


def _attention_output(scores, sink_col, segs):
    low = lax.broadcasted_iota(jnp.int32, (BLOCK, 128), 1) < HEAD_DIM
    m = sink_col
    for s in scores:
        m = jnp.maximum(m, jnp.max(s, axis=-1, keepdims=True))
    denom = jnp.exp(sink_col - m)
    acc = None
    for s, (_, v, _) in zip(scores, segs):
        pr = jnp.exp(s - m)
        denom = denom + jnp.sum(pr, axis=-1, keepdims=True)
        o = jnp.dot(pr.astype(BF16), v, preferred_element_type=F32)
        acc = o if acc is None else acc + o
    acc = acc / denom
    a = [acc[i * BLOCK:(i + 1) * BLOCK] for i in range(4)]
    out0 = jnp.where(low, a[0], pltpu.roll(a[1], HEAD_DIM, 1))
    out1 = jnp.where(low, pltpu.roll(a[2], HEAD_DIM, 1), a[3])
    return jnp.concatenate([out0, out1], axis=-1)


def _vec_views(vec_ref):
    return dict(conv=vec_ref.at[0:3, 0:256], nmix=vec_ref.at[3:4, :], nff=vec_ref.at[4:5, :],
                qn=vec_ref.at[5:6, 0:256], kn=vec_ref.at[5:6, 256:384], sgun=vec_ref.at[5:6, 512:768],
                pscale=vec_ref.at[5:6, 768:1024])


def _layer_kernel(*refs, tile, seq, n_total, is_ctx, n_cast):
    if is_ctx:
        (l_ref, h_ref, mod_ref, sink_ref, vec_ref, wsgu_ref, bsgu_ref, wpool_ref, win_ref, wout_ref, w1_ref, w2_ref,
         o_ref, kv_ref, p_ref, ycat_ref, hid_ref, h1_ref, f_ref) = refs
        modb_ref = mod_ref
        j = pl.program_id(1)
    else:
        (sched_ref, l_ref, h_ref, hn_ref, cs_ref, csn_ref, kvc_ref, mod_ref, modb_ref, sink_ref, vec_ref, wsgu_ref,
         bsgu_ref, wpool_ref, win_ref, wout_ref, w1_ref, w2_ref) = refs[:18]
        cast_in = refs[18:18 + n_cast]
        o_ref = refs[18 + n_cast]
        cast_out = refs[19 + n_cast:19 + 2 * n_cast]
        (p_ref, ycat_ref, hid_ref, h1_ref, f_ref, bias_ref, kvprev_ref, eprev_ref) = refs[19 + 2 * n_cast:]
        step = pl.program_id(0)
    vv = _vec_views(vec_ref)
    nmix_ref, nff_ref, qn_ref, kn_ref = vv["nmix"], vv["nff"], vv["qn"], vv["kn"]
    conv_ref, sgun_ref, pscale_ref = vv["conv"], vv["sgun"], vv["pscale"]
    if not is_ctx:

        @pl.when(step == 0)
        def _():
            h1_ref[...] = jnp.zeros_like(h1_ref)
            f_ref[...] = jnp.zeros_like(f_ref)
            kvprev_ref[...] = jnp.zeros_like(kvprev_ref)
            eprev_ref[...] = jnp.zeros_like(eprev_ref)
            r = lax.broadcasted_iota(jnp.int32, (4 * BLOCK, 3 * BLOCK), 0) % BLOCK
            col = lax.broadcasted_iota(jnp.int32, (4 * BLOCK, 3 * BLOCK), 1)
            seg = col // BLOCK
            jj = col % BLOCK
            band_prev = (seg == 0) & (jj >= r)
            band_next = (seg == 2) & (jj <= r)
            bias_ref[0] = jnp.where((seg == 1) | band_prev | band_next, 0.0, NEG)
            bias_ref[1] = jnp.where((seg == 1) | band_next, 0.0, NEG)
            bias_ref[2] = jnp.where((seg == 1) | band_prev, 0.0, NEG)

        j = sched_ref[1, step]
        has_prev = j > 0
        has_next = j < seq // tile - 1
    layer = l_ref[0]
    tile_start = j * tile
    mod = mod_ref[...]
    sh1, sc1, g1 = mod[:, 0:D_MODEL], mod[:, D_MODEL:2 * D_MODEL], mod[:, 2 * D_MODEL:3 * D_MODEL]
    rows = pl.ds(0, tile)
    nblk = tile // BLOCK
    st = {}

    ff1_cols = 512

    def ff1(c):
        u = jnp.dot(f_ref[...], w1_ref[:, c:c + ff1_cols], preferred_element_type=F32)
        u = jnp.maximum(u, 0.0)
        hid_ref[:, c:c + ff1_cols] = (u * u).astype(BF16)

    half = D_MODEL // 2

    def ff2(c):
        g2 = modb_ref[:, 5 * D_MODEL + c:5 * D_MODEL + c + half]
        y = jnp.dot(hid_ref[...], w2_ref[:, c:c + half], preferred_element_type=F32)
        o_ref[:, c:c + half] = h1_ref[:, c:c + half] + g2 * y

    back = [functools.partial(ff1, c) for c in range(0, D_FF, ff1_cols)]
    back += [functools.partial(ff2, c) for c in range(0, D_MODEL, half)]

    def norm_in():
        nmix = nmix_ref[...]
        a_main = _modulated_norm(h_ref[...], nmix, sh1, sc1)
        if is_ctx:
            st["a_ext"] = a_main
        else:
            a_next = _modulated_norm(hn_ref[...], nmix, sh1, sc1)
            st["a_ext"] = jnp.concatenate([a_main, a_next[0:EDGE]], axis=0)
            st["a_next"] = a_next

    def in_proj():
        p_ref[...] = jnp.dot(st["a_ext"], win_ref[...], preferred_element_type=F32)
        if not is_ctx:
            st["kv_next"] = jnp.dot(st["a_next"], win_ref[:, K_OFF:K_OFF + 256], preferred_element_type=F32)

    def qk_norm():
        q = _head_rms(p_ref[rows, Q_OFF:K_OFF], qn_ref[...])
        k = _head_rms(p_ref[rows, K_OFF:V_OFF], kn_ref[...])
        if not is_ctx:
            q = _rope(q, cs_ref[...])
            k = _rope(k, cs_ref[...])
        p_ref[rows, Q_OFF:K_OFF] = q * (HEAD_DIM ** -0.5)
        k_tile = k.astype(BF16)
        v_tile = p_ref[rows, V_OFF:V_OFF + 128].astype(BF16)
        if is_ctx:
            kv_ref[:, 0:128] = k
            kv_ref[:, 128:256] = p_ref[rows, V_OFF:V_OFF + 128]
            st["k_ext"], st["v_ext"] = k_tile, v_tile
        else:
            kv_next = st["kv_next"]
            k_next = _rope(_head_rms(kv_next[:, 0:128], kn_ref[...]), csn_ref[...]).astype(BF16)
            v_next = kv_next[:, 128:256].astype(BF16)
            st["k_ext"] = jnp.concatenate([kvprev_ref[:, 0:128], k_tile, k_next], axis=0)
            st["v_ext"] = jnp.concatenate([kvprev_ref[:, 128:256], v_tile, v_next], axis=0)
            kvprev_ref[:, 0:128] = k_tile[tile - BLOCK:tile]
            kvprev_ref[:, 128:256] = v_tile[tile - BLOCK:tile]
            st["k_ctx"] = kvc_ref[:, 0:128].astype(BF16)
            st["v_ctx"] = kvc_ref[:, 128:256].astype(BF16)

    def att_scores(i):
        sink = [sink_ref[layer, n] for n in range(N_Q_HEADS)]
        qb = p_ref[pl.ds(i * BLOCK, BLOCK), Q_OFF:K_OFF]
        if is_ctx:
            segs = [(st["k_ext"], st["v_ext"], None)]
        else:
            which = 0
            if i == 0:
                which = jnp.where(has_prev, 0, 1)
            if i == nblk - 1:
                which = jnp.where(has_next, 0, 2)
            segs = [(st["k_ext"][i * BLOCK:(i + 3) * BLOCK], st["v_ext"][i * BLOCK:(i + 3) * BLOCK],
                     bias_ref[which]),
                    (st["k_ctx"], st["v_ctx"], None)]
        st["att", i] = _attention_scores(qb, segs, sink) + (segs,)

    def att_output(i):
        scores, sink_col, segs = st.pop(("att", i))
        ycat_ref[i * BLOCK:(i + 1) * BLOCK, 256:512] = _attention_output(scores, sink_col, segs).astype(BF16)

    def local_mixers():
        if is_ctx:
            zeros = jnp.zeros((POOL_HALO, 256), F32)
            z_prev = z_next = x_prev = x_next = zeros
        else:
            ne = pl.ds(tile, POOL_HALO)
            le = pl.ds(tile - POOL_HALO, POOL_HALO)
            z_prev = jnp.where(has_prev, eprev_ref[:, 0:256], 0.0)
            z_next = jnp.where(has_next, p_ref[ne, A_GC:A_GC + 256] * p_ref[ne, A_H:A_H + 256], 0.0)
            x_prev = jnp.where(has_prev, eprev_ref[:, 256:512], 0.0)
            x_next = jnp.where(has_next, p_ref[ne, D_OFF:D_OFF + 256], 0.0)
            eprev_ref[:, 0:256] = p_ref[le, A_GC:A_GC + 256] * p_ref[le, A_H:A_H + 256]
            eprev_ref[:, 256:512] = p_ref[le, D_OFF:D_OFF + 256]
        ycat_ref[:, 0:256] = _conv_mixer(p_ref, z_prev, z_next, conv_ref, tile).astype(BF16)
        ycat_ref[:, 512:768] = _sgu_mixer(p_ref, sgun_ref, wsgu_ref, bsgu_ref, tile).astype(BF16)
        ycat_ref[:, 768:1024] = _pool_mixer(p_ref, x_prev, x_next, wpool_ref, pscale_ref,
                                            tile, seq, tile_start).astype(BF16)

    def out_proj():
        st["h1"] = h_ref[...] + g1 * jnp.dot(ycat_ref[...], wout_ref[...], preferred_element_type=F32)

    def norm_ff():
        h1 = st["h1"]
        h1_ref[...] = h1
        f_ref[...] = _modulated_norm(h1, nff_ref[...], mod[:, 3 * D_MODEL:4 * D_MODEL],
                                     mod[:, 4 * D_MODEL:5 * D_MODEL])

    sc = [functools.partial(att_scores, i) for i in range(nblk)]
    av = [functools.partial(att_output, i) for i in range(nblk)]
    if is_ctx:
        att = [piece for pair in zip(sc, av) for piece in pair]
        order = [norm_in, in_proj, qk_norm] + att + [local_mixers, out_proj, norm_ff] + back
    else:
        assert nblk == 4 and len(back) == 10

        def cast_next_weights():
            for src_ref, dst_ref in zip(cast_in, cast_out):
                dst_ref[...] = src_ref[...].astype(BF16)

        order = [back[0], back[1], norm_in, in_proj, cast_next_weights, back[2], qk_norm, sc[0], sc[1], back[3], av[0],
                 sc[2], back[4], av[1], sc[3], back[5], av[2], back[6], av[3], back[7], local_mixers, back[8],
                 out_proj, back[9], norm_ff]
    for piece in order:
        piece()


def _layer_spec(shape):
    nd = len(shape)
    return pl.BlockSpec((None,) + shape, lambda *g: (g[-1][0],) + (0,) * nd, pipeline_mode=pl.Buffered(1))


def _whole_spec(shape):
    nd = len(shape)
    return pl.BlockSpec((None,) + shape, lambda *g: (0,) * (nd + 1), pipeline_mode=pl.Buffered(1))


def _layer_call(layer, h, kv_ctx, cs, mods, mod_row, lw, wts, next_w=(), *, tile, is_ctx):
    bsz, seq, _ = h.shape
    n_tiles = seq // tile
    n_total = bsz * n_tiles
    if is_ctx:
        grid = (bsz, n_tiles)
        front = lambda g: (g[0], g[1])
        back = front
        prefetch = (layer,)
    else:
        grid = (n_total + 1,)
        steps = np.arange(n_total + 1)
        ft = np.minimum(steps, n_total - 1)
        bt = np.maximum(steps - 1, 0)
        table = np.stack([ft // n_tiles, ft % n_tiles,
                          np.minimum((ft % n_tiles + 1) * (tile // BLOCK), seq // BLOCK - 1),
                          bt // n_tiles, bt % n_tiles, ft]).astype(np.int32)
        front = lambda g: (g[-2][0, g[0]], g[-2][1, g[0]])
        back = lambda g: (g[-2][3, g[0]], g[-2][4, g[0]])
        prefetch = (jnp.asarray(table), layer)
    mod_spec = lambda which: pl.BlockSpec(
        (None, None, 1, 6 * D_MODEL),
        lambda *g: (g[-1][0], which(g)[0] if mod_row is None else mod_row, 0, 0))
    in_specs = [pl.BlockSpec((None, tile, D_MODEL), lambda *g: front(g) + (0,))]
    args = [h]
    if not is_ctx:
        next_blk = lambda g: g[-2][2, g[0]]
        in_specs += [
            pl.BlockSpec((None, BLOCK, D_MODEL), lambda *g: (front(g)[0], next_blk(g), 0)),
            pl.BlockSpec((tile, 256), lambda *g: (front(g)[1], 0)),
            pl.BlockSpec((BLOCK, 256), lambda *g: (next_blk(g), 0)),
            pl.BlockSpec((None, kv_ctx.shape[1], 256), lambda *g: (front(g)[0], 0, 0)),
        ]
        args += [h, cs, cs, kv_ctx]
    in_specs.append(mod_spec(front))
    args.append(mods)
    if not is_ctx:
        in_specs.append(mod_spec(back))
        args.append(mods)
    in_specs += [
        pl.BlockSpec(memory_space=pltpu.SMEM),
        _layer_spec(lw["vecs"].shape[1:]),
        _layer_spec((N_SGU_GROUPS * CHUNK, CHUNK)),
        _layer_spec((CHUNK, 256)),
        _layer_spec((256, 256)),
    ]
    args += [lw["sink"], lw["vecs"], lw["w_sgu"], lw["b_sgu"], lw["w_pool"]]
    in_specs += [_whole_spec(w.shape[1:]) for w in wts]
    args += list(wts)
    assert not (is_ctx and next_w)
    chunk = lambda g: g[-2][5, g[0]]
    cast_shapes, cast_specs = [], []
    for w in next_w:
        rows, cols = w.shape[1] // n_total, w.shape[2]
        in_specs.append(pl.BlockSpec((None, rows, cols),
                                     lambda *g: (jnp.minimum(g[-1][0] + 1, DEPTH - 1), chunk(g), 0)))
        args.append(w)
        cast_shapes.append(jax.ShapeDtypeStruct((1,) + w.shape[1:], BF16))
        cast_specs.append(pl.BlockSpec((None, rows, cols), lambda *g: (0, chunk(g), 0)))
    tile_spec = pl.BlockSpec((None, tile, D_MODEL), lambda *g: back(g) + (0,))
    scratch = [pltpu.VMEM((tile if is_ctx else tile + EDGE, D_PROJ), F32),
               pltpu.VMEM((tile, D_MODEL), BF16),
               pltpu.VMEM((tile, D_FF), BF16),
               pltpu.VMEM((tile, D_MODEL), F32),
               pltpu.VMEM((tile, D_MODEL), BF16)]
    if is_ctx:
        out_shape = (jax.ShapeDtypeStruct(h.shape, F32), jax.ShapeDtypeStruct((bsz, seq, 256), F32))
        out_specs = (tile_spec, pl.BlockSpec((None, tile, 256), lambda *g: back(g) + (0,)))
        semantics = ("parallel", "parallel")
    else:
        out_shape = (jax.ShapeDtypeStruct(h.shape, F32),) + tuple(cast_shapes)
        out_specs = (tile_spec,) + tuple(cast_specs)
        scratch += [pltpu.VMEM((3, 4 * BLOCK, 3 * BLOCK), F32),
                    pltpu.VMEM((BLOCK, 256), BF16),
                    pltpu.VMEM((POOL_HALO, 512), F32)]
        semantics = ("arbitrary",)
    return pl.pallas_call(
        functools.partial(_layer_kernel, tile=tile, seq=seq, n_total=n_total, is_ctx=is_ctx, n_cast=len(next_w)),
        out_shape=out_shape,
        grid_spec=pltpu.PrefetchScalarGridSpec(
            num_scalar_prefetch=len(prefetch),
            grid=grid,
            in_specs=in_specs,
            out_specs=out_specs,
            scratch_shapes=scratch,
        ),
        compiler_params=pltpu.CompilerParams(dimension_semantics=semantics,
                                             vmem_limit_bytes=VMEM_LIMIT),
        name="layer_ctx" if is_ctx else "layer_lat",
    )(*prefetch, *args)


def _ctx_kv_kernel(l_ref, h_ref, mod_ref, vec_ref, w_ref, kv_ref):
    mod = mod_ref[...]
    vv = _vec_views(vec_ref)
    a = _modulated_norm(h_ref[...], vv["nmix"][...], mod[:, 0:D_MODEL], mod[:, D_MODEL:2 * D_MODEL])
    kv = jnp.dot(a, w_ref[...], preferred_element_type=F32)
    kv_ref[:, 0:128] = _head_rms(kv[:, 0:128], vv["kn"][...])
    kv_ref[:, 128:256] = kv[:, 128:256]


def _ctx_kv_call(layer, h, mods, mod_row, lw, w_in_b):
    bsz, seq, _ = h.shape
    return pl.pallas_call(
        _ctx_kv_kernel,
        out_shape=jax.ShapeDtypeStruct((bsz, seq, 256), F32),
        grid_spec=pltpu.PrefetchScalarGridSpec(
            num_scalar_prefetch=1,
            grid=(bsz, 1),
            in_specs=[
                pl.BlockSpec((None, seq, D_MODEL), lambda b, j, l: (b, 0, 0)),
                pl.BlockSpec((None, None, 1, 6 * D_MODEL), lambda b, j, l: (l[0], mod_row, 0, 0)),
                _layer_spec(lw["vecs"].shape[1:]),
                pl.BlockSpec((None, D_MODEL, 256), lambda b, j, l: (0, 0, K_OFF // 256),
                             pipeline_mode=pl.Buffered(1)),
            ],
            out_specs=pl.BlockSpec((None, seq, 256), lambda b, j, l: (b, 0, 0)),
        ),
        compiler_params=pltpu.CompilerParams(dimension_semantics=("parallel", "parallel"),
                                             vmem_limit_bytes=VMEM_LIMIT),
        name="ctx_kv",
    )(layer, h, mods, lw["vecs"], w_in_b)


def _rope_table(length):
    rows = length // GRID_W
    row = np.repeat(np.arange(rows), GRID_W).astype(np.float32)
    col = np.tile(np.arange(GRID_W), rows).astype(np.float32)
    n_freq = HEAD_DIM // 4
    inv = jnp.asarray(ROPE_THETA, F32) ** (-jnp.arange(n_freq, dtype=F32) / n_freq)
    ang_r = jnp.asarray(row)[:, None] * inv[None, :]
    ang_c = jnp.asarray(col)[:, None] * inv[None, :]
    ang = jnp.concatenate([ang_r, ang_r, ang_c, ang_c], axis=-1)
    sign = jnp.asarray(np.where(np.arange(HEAD_DIM) % 32 < 16, -1.0, 1.0), F32)
    return jnp.concatenate([jnp.tile(jnp.cos(ang), (1, 2)), jnp.tile(jnp.sin(ang) * sign, (1, 2))], axis=-1)


def _pack_vectors(conv_w, norm_mix, norm_ff, q_norm, k_norm, sgu_norm, pool_scale):
    depth = conv_w.shape[0]
    pad = lambda a: jnp.pad(a, ((0, 0), (0, 0), (0, D_MODEL - a.shape[-1])))
    row5 = jnp.concatenate([jnp.tile(q_norm, (1, N_Q_HEADS)), jnp.tile(k_norm, (1, N_KV_HEADS)),
                            jnp.zeros((depth, 128), F32), sgu_norm, pool_scale], axis=-1)
    return jnp.concatenate([pad(conv_w), norm_mix[:, None, :], norm_ff[:, None, :], row5[:, None, :],
                            jnp.zeros((depth, 2, D_MODEL), F32)], axis=1)


def kernel(x, c, ctx, c_ctx, norm_mix, norm_ff, w_ada, b_ada, w_in, w_out, conv_w, q_norm, k_norm, sink,
           sgu_norm, w_sgu, b_sgu, w_pool, pool_scale, w_ff1, w_ff2):
    bsz, seq, _ = x.shape
    ctx_len = ctx.shape[1]
    assert bsz + 1 <= MOD_ROWS and seq % LAT_TILE == 0 and ctx_len % BLOCK == 0

    big_w = (w_in, w_out, w_ff1, w_ff2)
    cc = jnp.concatenate([c, c_ctx[None, :], jnp.zeros((MOD_ROWS - bsz - 1, D_MODEL), F32)], axis=0)
    mods, wts = _ada_call(cc, w_ada, b_ada, big_w)
    mods = mods.reshape(DEPTH, MOD_ROWS, 1, 6 * D_MODEL)

    eye = jnp.eye(len(POOL_WINDOWS), dtype=F32)
    lw = dict(
        sink=sink,
        vecs=_pack_vectors(conv_w, norm_mix, norm_ff, q_norm, k_norm, sgu_norm, pool_scale),
        w_sgu=w_sgu.reshape(DEPTH, N_SGU_GROUPS * CHUNK, CHUNK).astype(BF16),
        b_sgu=jnp.repeat(jnp.swapaxes(b_sgu, 1, 2), HEAD_DIM, axis=2),
        w_pool=jnp.einsum("lgcd,gh->lgchd", w_pool, eye).reshape(DEPTH, 256, 256).astype(BF16),
    )
    cs = _rope_table(seq)

    h_lat, h_ctx = x, ctx
    for l in range(DEPTH):
        layer = jnp.full((1,), l, jnp.int32)
        last = l == DEPTH - 1
        if last:
            kv_ctx = _ctx_kv_call(layer, h_ctx, mods, bsz, lw, wts[0])
        else:
            h_ctx, kv_ctx = _layer_call(layer, h_ctx, None, None, mods, bsz, lw, wts, tile=ctx_len, is_ctx=True)
        outs = _layer_call(layer, h_lat, kv_ctx, cs, mods, None, lw, wts, () if last else big_w,
                           tile=LAT_TILE, is_ctx=False)
        h_lat, wts = outs[0], tuple(outs[1:])
    return h_lat
```

```python
import functools

import jax
import jax.numpy as jnp
import numpy as np
from jax import lax
from jax.experimental import pallas as pl
from jax.experimental.pallas import tpu as pltpu

D_MODEL = 1024
DEPTH = 4
GRID_W = 64
HEAD_DIM = 64
N_Q_HEADS = 4
N_KV_HEADS = 2
BLOCK = 128
ROPE_THETA = 10000.0
CHUNK = 128
N_SGU_GROUPS = 4
POOL_WINDOWS = (2, 4, 8, 16)
POOL_HALO = 8
EDGE = 16
D_FF = 4 * D_MODEL
EPS = 1e-6
D_PROJ = 2048

A_H, A_GB, A_GC = 0, 256, 512
Q_OFF, K_OFF, V_OFF = 768, 1024, 1152
C_U, C_V = 1280, 1536
D_OFF = 1792

NEG = -1e30
MOD_ROWS = 16
LAT_TILE = 512
VMEM_LIMIT = 58 * 1024 * 1024

F32 = jnp.float32
BF16 = jnp.bfloat16


def _rms(x, g):
    ms = jnp.mean(x * x, axis=-1, keepdims=True)
    return x * lax.rsqrt(ms + EPS) * g


def _lane_group(shape, width):
    return lax.broadcasted_iota(jnp.int32, shape, len(shape) - 1) // width


def _head_rms(x, g):
    n = x.shape[-1]
    r = lax.broadcasted_iota(jnp.int32, (n, n), 0) // HEAD_DIM
    c = lax.broadcasted_iota(jnp.int32, (n, n), 1) // HEAD_DIM
    ones = jnp.where(r == c, 1.0, 0.0).astype(BF16)
    x2 = x * x
    hi = x2.astype(BF16)
    lo = (x2 - hi.astype(F32)).astype(BF16)
    ms = (jnp.dot(hi, ones, preferred_element_type=F32)
          + jnp.dot(lo, ones, preferred_element_type=F32)) * (1.0 / HEAD_DIM)
    return x * lax.rsqrt(ms + EPS) * g


def _swap_lanes(x, width):
    lane = lax.broadcasted_iota(jnp.int32, x.shape, x.ndim - 1)
    return jnp.where(lane % (2 * width) < width, pltpu.roll(x, 128 - width, x.ndim - 1),
                     pltpu.roll(x, width, x.ndim - 1))


def _rope(x, cs):
    cos, sin = cs[:, 0:128], cs[:, 128:256]
    parts = [x[:, s:s + 128] * cos + _swap_lanes(x[:, s:s + 128], 16) * sin for s in range(0, x.shape[-1], 128)]
    return parts[0] if len(parts) == 1 else jnp.concatenate(parts, axis=-1)


def _modulated_norm(x, g, shift, scale):
    return (_rms(x, g) * (1.0 + scale) + shift).astype(BF16)


def _ada_kernel(cc_ref, w_ref, b_ref, *refs):
    n_cast = (len(refs) - 1) // 2
    o_ref = refs[n_cast]
    cc = cc_ref[...]
    s = cc * jax.nn.sigmoid(cc)
    o_ref[...] = jnp.dot(s.astype(BF16), w_ref[...].astype(BF16), preferred_element_type=F32) + b_ref[...]
    for src_ref, dst_ref in zip(refs[:n_cast], refs[n_cast + 1:]):
        dst_ref[...] = src_ref[...].astype(BF16)


def _ada_call(cc, w_ada, b_ada, big_w):
    nblk = 4
    wb = 6 * D_MODEL // nblk
    steps = DEPTH * nblk
    chunk = lambda l, n: (0, l * nblk + n, 0)
    cast_in = [pl.BlockSpec((None, w.shape[1] // steps, w.shape[2]), chunk) for w in big_w]
    outs = pl.pallas_call(
        _ada_kernel,
        out_shape=(jax.ShapeDtypeStruct((DEPTH, MOD_ROWS, 6 * D_MODEL), F32),)
        + tuple(jax.ShapeDtypeStruct((1,) + w.shape[1:], BF16) for w in big_w),
        grid=(DEPTH, nblk),
        in_specs=[
            pl.BlockSpec((MOD_ROWS, D_MODEL), lambda l, n: (0, 0)),
            pl.BlockSpec((None, D_MODEL, wb), lambda l, n: (l, 0, n)),
            pl.BlockSpec((None, 1, wb), lambda l, n: (l, 0, n)),
        ] + cast_in,
        out_specs=(pl.BlockSpec((None, MOD_ROWS, wb), lambda l, n: (l, 0, n)),) + tuple(cast_in),
        compiler_params=pltpu.CompilerParams(dimension_semantics=("arbitrary", "arbitrary"),
                                             vmem_limit_bytes=VMEM_LIMIT),
        name="ada_mod",
    )(cc, w_ada, b_ada.reshape(DEPTH, 1, 6 * D_MODEL), *big_w)
    return outs[0], tuple(outs[1:])


def _shift_rows(x, k):
    n = x.shape[0]
    return pltpu.roll(x, k % n, 0)


def _conv_mixer(p_ref, halo_prev, halo_next, conv_ref, tile):
    rows = pl.ds(0, tile)
    z = p_ref[rows, A_GC:A_GC + 256] * p_ref[rows, A_H:A_H + 256]
    z_ext = jnp.concatenate([halo_prev, z, halo_next], axis=0)
    z_prev = _shift_rows(z_ext, 1)[POOL_HALO:POOL_HALO + tile]
    z_next = _shift_rows(z_ext, -1)[POOL_HALO:POOL_HALO + tile]
    cw = conv_ref[...]
    y = cw[0:1] * z_prev + cw[1:2] * z + cw[2:3] * z_next
    return p_ref[rows, A_GB:A_GB + 256] * y


def _pool_mixer(p_ref, halo_prev, halo_next, wpool_ref, pscale_ref, tile, seq, tile_start):
    x = p_ref[pl.ds(0, tile), D_OFF:D_OFF + 256]
    x_ext = jnp.concatenate([halo_prev, x, halo_next], axis=0)
    s2 = _shift_rows(x_ext, 1) + x_ext
    s4 = _shift_rows(s2, 1) + _shift_rows(s2, -1)
    s8 = _shift_rows(s4, 2) + _shift_rows(s4, -2)
    s16 = _shift_rows(s8, 4) + _shift_rows(s8, -4)
    grp = _lane_group(x_ext.shape, HEAD_DIM)
    s = jnp.where(grp == 0, s2, jnp.where(grp == 1, s4, jnp.where(grp == 2, s8, s16)))
    s = s[POOL_HALO:POOL_HALO + tile]
    t = tile_start + lax.broadcasted_iota(jnp.int32, (tile, 256), 0)
    half = jnp.left_shift(1, _lane_group((tile, 256), HEAD_DIM))
    cnt = jnp.minimum(t + half, seq) - jnp.maximum(t - half, 0)
    d = s / cnt.astype(F32) - x
    y = jnp.dot(d.astype(BF16), wpool_ref[...], preferred_element_type=F32)
    return y * pscale_ref[...]


def _pair_halves(a, b, half):
    low = lax.broadcasted_iota(jnp.int32, a.shape, 1) < HEAD_DIM
    if half == 0:
        return jnp.where(low, a, pltpu.roll(b, HEAD_DIM, 1))
    return jnp.where(low, pltpu.roll(a, HEAD_DIM, 1), b)


def _sgu_mixer(p_ref, sgun_ref, wsgu_ref, bsgu_ref, tile):
    rows = pl.ds(0, tile)
    nchunk = tile // CHUNK
    assert nchunk % 2 == 0
    vn = _rms(p_ref[rows, C_V:C_V + 256], sgun_ref[...])
    x = [[vn[c * CHUNK:(c + 1) * CHUNK, t * 128:(t + 1) * 128] for t in range(2)] for c in range(nchunk)]
    zg = []
    for g in range(N_SGU_GROUPS):
        t, half = divmod(g, 2)
        rhs = jnp.concatenate([_pair_halves(x[c][t], x[c + 1][t], half) for c in range(0, nchunk, 2)], axis=1)
        w_g = wsgu_ref[g * CHUNK:(g + 1) * CHUNK, :]
        zg.append(jnp.dot(w_g, rhs.astype(BF16), preferred_element_type=F32))
    bias = bsgu_ref[...]
    outs = []
    for c in range(nchunk):
        ct, half = divmod(c, 2)
        tiles = [_pair_halves(zg[2 * t][:, ct * 128:(ct + 1) * 128], zg[2 * t + 1][:, ct * 128:(ct + 1) * 128], half)
                 for t in range(2)]
        outs.append(jnp.concatenate(tiles, axis=1) + bias)
    return p_ref[rows, C_U:C_U + 256] * jnp.concatenate(outs, axis=0)


def _attention_scores(qb, segs, sink):
    low = lax.broadcasted_iota(jnp.int32, (BLOCK, 128), 1) < HEAD_DIM
    t0, t1 = qb[:, 0:128], qb[:, 128:256]
    rows = [jnp.where(low, t0, 0.0), jnp.where(low, pltpu.roll(t0, HEAD_DIM, 1), 0.0),
            jnp.where(low, 0.0, pltpu.roll(t1, HEAD_DIM, 1)), jnp.where(low, 0.0, t1)]
    q4 = jnp.concatenate(rows, axis=0).astype(BF16)
    rb = lax.broadcasted_iota(jnp.int32, (4 * BLOCK, 1), 0) // BLOCK
    sink_col = jnp.where(rb == 0, sink[0], jnp.where(rb == 1, sink[1], jnp.where(rb == 2, sink[2], sink[3])))
    scores = []
    for k, _, bias in segs:
        s = lax.dot_general(q4, k, (((1,), (1,)), ((), ())), preferred_element_type=F32)
        scores.append(s if bias is None else s + bias)
    return scores, sink_col


def _attention_output(scores, sink_col, segs):
    low = lax.broadcasted_iota(jnp.int32, (BLOCK, 128), 1) < HEAD_DIM
    m = sink_col
    for s in scores:
        m = jnp.maximum(m, jnp.max(s, axis=-1, keepdims=True))
    denom = jnp.exp(sink_col - m)
    acc = None
    for s, (_, v, _) in zip(scores, segs):
        pr = jnp.exp(s - m)
        denom = denom + jnp.sum(pr, axis=-1, keepdims=True)
        o = jnp.dot(pr.astype(BF16), v, preferred_element_type=F32)
        acc = o if acc is None else acc + o
    acc = acc / denom
    a = [acc[i * BLOCK:(i + 1) * BLOCK] for i in range(4)]
    out0 = jnp.where(low, a[0], pltpu.roll(a[1], HEAD_DIM, 1))
    out1 = jnp.where(low, pltpu.roll(a[2], HEAD_DIM, 1), a[3])
    return jnp.concatenate([out0, out1], axis=-1)


def _vec_views(vec_ref):
    return dict(conv=vec_ref.at[0:3, 0:256], nmix=vec_ref.at[3:4, :], nff=vec_ref.at[4:5, :],
                qn=vec_ref.at[5:6, 0:256], kn=vec_ref.at[5:6, 256:384], sgun=vec_ref.at[5:6, 512:768],
                pscale=vec_ref.at[5:6, 768:1024])


def _layer_kernel(*refs, tile, seq, n_total, is_ctx, n_cast):
    if is_ctx:
        (l_ref, h_ref, mod_ref, sink_ref, vec_ref, wsgu_ref, bsgu_ref, wpool_ref, win_ref, wout_ref, w1_ref, w2_ref,
         o_ref, kv_ref, p_ref, ycat_ref, hid_ref, h1_ref, f_ref) = refs
        modb_ref = mod_ref
        j = pl.program_id(1)
    else:
        (sched_ref, l_ref, h_ref, hn_ref, cs_ref, csn_ref, kvc_ref, mod_ref, modb_ref, sink_ref, vec_ref, wsgu_ref,
         bsgu_ref, wpool_ref, win_ref, wout_ref, w1_ref, w2_ref) = refs[:18]
        cast_in = refs[18:18 + n_cast]
        o_ref = refs[18 + n_cast]
        cast_out = refs[19 + n_cast:19 + 2 * n_cast]
        (p_ref, ycat_ref, hid_ref, h1_ref, f_ref, bias_ref, kvprev_ref, eprev_ref) = refs[19 + 2 * n_cast:]
        step = pl.program_id(0)
    vv = _vec_views(vec_ref)
    nmix_ref, nff_ref, qn_ref, kn_ref = vv["nmix"], vv["nff"], vv["qn"], vv["kn"]
    conv_ref, sgun_ref, pscale_ref = vv["conv"], vv["sgun"], vv["pscale"]
    if not is_ctx:

        @pl.when(step == 0)
        def _():
            h1_ref[...] = jnp.zeros_like(h1_ref)
            f_ref[...] = jnp.zeros_like(f_ref)
            kvprev_ref[...] = jnp.zeros_like(kvprev_ref)
            eprev_ref[...] = jnp.zeros_like(eprev_ref)
            r = lax.broadcasted_iota(jnp.int32, (4 * BLOCK, 3 * BLOCK), 0) % BLOCK
            col = lax.broadcasted_iota(jnp.int32, (4 * BLOCK, 3 * BLOCK), 1)
            seg = col // BLOCK
            jj = col % BLOCK
            band_prev = (seg == 0) & (jj >= r)
            band_next = (seg == 2) & (jj <= r)
            bias_ref[0] = jnp.where((seg == 1) | band_prev | band_next, 0.0, NEG)
            bias_ref[1] = jnp.where((seg == 1) | band_next, 0.0, NEG)
            bias_ref[2] = jnp.where((seg == 1) | band_prev, 0.0, NEG)

        j = sched_ref[1, step]
        has_prev = j > 0
        has_next = j < seq // tile - 1
    layer = l_ref[0]
    tile_start = j * tile
    mod = mod_ref[...]
    sh1, sc1, g1 = mod[:, 0:D_MODEL], mod[:, D_MODEL:2 * D_MODEL], mod[:, 2 * D_MODEL:3 * D_MODEL]
    rows = pl.ds(0, tile)
    nblk = tile // BLOCK
    st = {}

    ff1_cols = 512

    def ff1(c):
        u = jnp.dot(f_ref[...], w1_ref[:, c:c + ff1_cols], preferred_element_type=F32)
        u = jnp.maximum(u, 0.0)
        hid_ref[:, c:c + ff1_cols] = (u * u).astype(BF16)

    half = D_MODEL // 2

    def ff2(c):
        g2 = modb_ref[:, 5 * D_MODEL + c:5 * D_MODEL + c + half]
        y = jnp.dot(hid_ref[...], w2_ref[:, c:c + half], preferred_element_type=F32)
        o_ref[:, c:c + half] = h1_ref[:, c:c + half] + g2 * y

    back = [functools.partial(ff1, c) for c in range(0, D_FF, ff1_cols)]
    back += [functools.partial(ff2, c) for c in range(0, D_MODEL, half)]

    def norm_in():
        nmix = nmix_ref[...]
        a_main = _modulated_norm(h_ref[...], nmix, sh1, sc1)
        if is_ctx:
            st["a_ext"] = a_main
        else:
            a_next = _modulated_norm(hn_ref[...], nmix, sh1, sc1)
            st["a_ext"] = jnp.concatenate([a_main, a_next[0:EDGE]], axis=0)
            st["a_next"] = a_next

    def in_proj():
        p_ref[...] = jnp.dot(st["a_ext"], win_ref[...], preferred_element_type=F32)
        if not is_ctx:
            st["kv_next"] = jnp.dot(st["a_next"], win_ref[:, K_OFF:K_OFF + 256], preferred_element_type=F32)

    def qk_norm():
        q = _head_rms(p_ref[rows, Q_OFF:K_OFF], qn_ref[...])
        k = _head_rms(p_ref[rows, K_OFF:V_OFF], kn_ref[...])
        if not is_ctx:
            q = _rope(q, cs_ref[...])
            k = _rope(k, cs_ref[...])
        p_ref[rows, Q_OFF:K_OFF] = q * (HEAD_DIM ** -0.5)
        k_tile = k.astype(BF16)
        v_tile = p_ref[rows, V_OFF:V_OFF + 128].astype(BF16)
        if is_ctx:
            kv_ref[:, 0:128] = k
            kv_ref[:, 128:256] = p_ref[rows, V_OFF:V_OFF + 128]
            st["k_ext"], st["v_ext"] = k_tile, v_tile
        else:
            kv_next = st["kv_next"]
            k_next = _rope(_head_rms(kv_next[:, 0:128], kn_ref[...]), csn_ref[...]).astype(BF16)
            v_next = kv_next[:, 128:256].astype(BF16)
            st["k_ext"] = jnp.concatenate([kvprev_ref[:, 0:128], k_tile, k_next], axis=0)
            st["v_ext"] = jnp.concatenate([kvprev_ref[:, 128:256], v_tile, v_next], axis=0)
            kvprev_ref[:, 0:128] = k_tile[tile - BLOCK:tile]
            kvprev_ref[:, 128:256] = v_tile[tile - BLOCK:tile]
            st["k_ctx"] = kvc_ref[:, 0:128].astype(BF16)
            st["v_ctx"] = kvc_ref[:, 128:256].astype(BF16)

    def att_scores(i):
        sink = [sink_ref[layer, n] for n in range(N_Q_HEADS)]
        qb = p_ref[pl.ds(i * BLOCK, BLOCK), Q_OFF:K_OFF]
        if is_ctx:
            segs = [(st["k_ext"], st["v_ext"], None)]
        else:
            which = 0
            if i == 0:
                which = jnp.where(has_prev, 0, 1)
            if i == nblk - 1:
                which = jnp.where(has_next, 0, 2)
            segs = [(st["k_ext"][i * BLOCK:(i + 3) * BLOCK], st["v_ext"][i * BLOCK:(i + 3) * BLOCK],
                     bias_ref[which]),
                    (st["k_ctx"], st["v_ctx"], None)]
        st["att", i] = _attention_scores(qb, segs, sink) + (segs,)

    def att_output(i):
        scores, sink_col, segs = st.pop(("att", i))
        ycat_ref[i * BLOCK:(i + 1) * BLOCK, 256:512] = _attention_output(scores, sink_col, segs).astype(BF16)

    def local_mixers():
        if is_ctx:
            zeros = jnp.zeros((POOL_HALO, 256), F32)
            z_prev = z_next = x_prev = x_next = zeros
        else:
            ne = pl.ds(tile, POOL_HALO)
            le = pl.ds(tile - POOL_HALO, POOL_HALO)
            z_prev = jnp.where(has_prev, eprev_ref[:, 0:256], 0.0)
            z_next = jnp.where(has_next, p_ref[ne, A_GC:A_GC + 256] * p_ref[ne, A_H:A_H + 256], 0.0)
            x_prev = jnp.where(has_prev, eprev_ref[:, 256:512], 0.0)
            x_next = jnp.where(has_next, p_ref[ne, D_OFF:D_OFF + 256], 0.0)
            eprev_ref[:, 0:256] = p_ref[le, A_GC:A_GC + 256] * p_ref[le, A_H:A_H + 256]
            eprev_ref[:, 256:512] = p_ref[le, D_OFF:D_OFF + 256]
        ycat_ref[:, 0:256] = _conv_mixer(p_ref, z_prev, z_next, conv_ref, tile).astype(BF16)
        ycat_ref[:, 512:768] = _sgu_mixer(p_ref, sgun_ref, wsgu_ref, bsgu_ref, tile).astype(BF16)
        ycat_ref[:, 768:1024] = _pool_mixer(p_ref, x_prev, x_next, wpool_ref, pscale_ref,
                                            tile, seq, tile_start).astype(BF16)

    def out_proj():
        st["h1"] = h_ref[...] + g1 * jnp.dot(ycat_ref[...], wout_ref[...], preferred_element_type=F32)

    def norm_ff():
        h1 = st["h1"]
        h1_ref[...] = h1
        f_ref[...] = _modulated_norm(h1, nff_ref[...], mod[:, 3 * D_MODEL:4 * D_MODEL],
                                     mod[:, 4 * D_MODEL:5 * D_MODEL])

    sc = [functools.partial(att_scores, i) for i in range(nblk)]
    av = [functools.partial(att_output, i) for i in range(nblk)]
    if is_ctx:
        order = [norm_in, in_proj, qk_norm] + sc + [local_mixers] + av + [out_proj, norm_ff] + back
    else:
        assert nblk == 4 and len(back) == 10

        def cast_next_weights():
            for src_ref, dst_ref in zip(cast_in, cast_out):
                dst_ref[...] = src_ref[...].astype(BF16)

        order = [back[0], back[1], norm_in, in_proj, cast_next_weights, qk_norm, back[2], sc[0], sc[1], back[3], av[0],
                 sc[2], back[4], av[1], sc[3], back[5], av[2], back[6], av[3], local_mixers, back[7], back[8],
                 out_proj, back[9], norm_ff]
    for piece in order:
        piece()


def _layer_spec(shape):
    nd = len(shape)
    return pl.BlockSpec((None,) + shape, lambda *g: (g[-1][0],) + (0,) * nd, pipeline_mode=pl.Buffered(1))


def _whole_spec(shape):
    nd = len(shape)
    return pl.BlockSpec((None,) + shape, lambda *g: (0,) * (nd + 1), pipeline_mode=pl.Buffered(1))


def _layer_call(layer, h, kv_ctx, cs, mods, mod_row, lw, wts, next_w=(), *, tile, is_ctx):
    bsz, seq, _ = h.shape
    n_tiles = seq // tile
    n_total = bsz * n_tiles
    if is_ctx:
        grid = (bsz, n_tiles)
        front = lambda g: (g[0], g[1])
        back = front
        prefetch = (layer,)
    else:
        grid = (n_total + 1,)
        steps = np.arange(n_total + 1)
        ft = np.minimum(steps, n_total - 1)
        bt = np.maximum(steps - 1, 0)
        table = np.stack([ft // n_tiles, ft % n_tiles,
                          np.minimum((ft % n_tiles + 1) * (tile // BLOCK), seq // BLOCK - 1),
                          bt // n_tiles, bt % n_tiles, ft]).astype(np.int32)
        front = lambda g: (g[-2][0, g[0]], g[-2][1, g[0]])
        back = lambda g: (g[-2][3, g[0]], g[-2][4, g[0]])
        prefetch = (jnp.asarray(table), layer)
    mod_spec = lambda which: pl.BlockSpec(
        (None, None, 1, 6 * D_MODEL),
        lambda *g: (g[-1][0], which(g)[0] if mod_row is None else mod_row, 0, 0))
    in_specs = [pl.BlockSpec((None, tile, D_MODEL), lambda *g: front(g) + (0,))]
    args = [h]
    if not is_ctx:
        next_blk = lambda g: g[-2][2, g[0]]
        in_specs += [
            pl.BlockSpec((None, BLOCK, D_MODEL), lambda *g: (front(g)[0], next_blk(g), 0)),
            pl.BlockSpec((tile, 256), lambda *g: (front(g)[1], 0)),
            pl.BlockSpec((BLOCK, 256), lambda *g: (next_blk(g), 0)),
            pl.BlockSpec((None, kv_ctx.shape[1], 256), lambda *g: (front(g)[0], 0, 0)),
        ]
        args += [h, cs, cs, kv_ctx]
    in_specs.append(mod_spec(front))
    args.append(mods)
    if not is_ctx:
        in_specs.append(mod_spec(back))
        args.append(mods)
    in_specs += [
        pl.BlockSpec(memory_space=pltpu.SMEM),
        _layer_spec(lw["vecs"].shape[1:]),
        _layer_spec((N_SGU_GROUPS * CHUNK, CHUNK)),
        _layer_spec((CHUNK, 256)),
        _layer_spec((256, 256)),
    ]
    args += [lw["sink"], lw["vecs"], lw["w_sgu"], lw["b_sgu"], lw["w_pool"]]
    in_specs += [_whole_spec(w.shape[1:]) for w in wts]
    args += list(wts)
    assert not (is_ctx and next_w)
    chunk = lambda g: g[-2][5, g[0]]
    cast_shapes, cast_specs = [], []
    for w in next_w:
        rows, cols = w.shape[1] // n_total, w.shape[2]
        in_specs.append(pl.BlockSpec((None, rows, cols),
                                     lambda *g: (jnp.minimum(g[-1][0] + 1, DEPTH - 1), chunk(g), 0)))
        args.append(w)
        cast_shapes.append(jax.ShapeDtypeStruct((1,) + w.shape[1:], BF16))
        cast_specs.append(pl.BlockSpec((None, rows, cols), lambda *g: (0, chunk(g), 0)))
    tile_spec = pl.BlockSpec((None, tile, D_MODEL), lambda *g: back(g) + (0,))
    scratch = [pltpu.VMEM((tile if is_ctx else tile + EDGE, D_PROJ), F32),
               pltpu.VMEM((tile, D_MODEL), BF16),
               pltpu.VMEM((tile, D_FF), BF16),
               pltpu.VMEM((tile, D_MODEL), F32),
               pltpu.VMEM((tile, D_MODEL), BF16)]
    if is_ctx:
        out_shape = (jax.ShapeDtypeStruct(h.shape, F32), jax.ShapeDtypeStruct((bsz, seq, 256), F32))
        out_specs = (tile_spec, pl.BlockSpec((None, tile, 256), lambda *g: back(g) + (0,)))
        semantics = ("parallel", "parallel")
    else:
        out_shape = (jax.ShapeDtypeStruct(h.shape, F32),) + tuple(cast_shapes)
        out_specs = (tile_spec,) + tuple(cast_specs)
        scratch += [pltpu.VMEM((3, 4 * BLOCK, 3 * BLOCK), F32),
                    pltpu.VMEM((BLOCK, 256), BF16),
                    pltpu.VMEM((POOL_HALO, 512), F32)]
        semantics = ("arbitrary",)
    return pl.pallas_call(
        functools.partial(_layer_kernel, tile=tile, seq=seq, n_total=n_total, is_ctx=is_ctx, n_cast=len(next_w)),
        out_shape=out_shape,
        grid_spec=pltpu.PrefetchScalarGridSpec(
            num_scalar_prefetch=len(prefetch),
            grid=grid,
            in_specs=in_specs,
            out_specs=out_specs,
            scratch_shapes=scratch,
        ),
        compiler_params=pltpu.CompilerParams(dimension_semantics=semantics,
                                             vmem_limit_bytes=VMEM_LIMIT),
        name="layer_ctx" if is_ctx else "layer_lat",
    )(*prefetch, *args)


def _ctx_kv_kernel(l_ref, h_ref, mod_ref, vec_ref, w_ref, kv_ref):
    mod = mod_ref[...]
    vv = _vec_views(vec_ref)
    a = _modulated_norm(h_ref[...], vv["nmix"][...], mod[:, 0:D_MODEL], mod[:, D_MODEL:2 * D_MODEL])
    kv = jnp.dot(a, w_ref[...], preferred_element_type=F32)
    kv_ref[:, 0:128] = _head_rms(kv[:, 0:128], vv["kn"][...])
    kv_ref[:, 128:256] = kv[:, 128:256]


def _ctx_kv_call(layer, h, mods, mod_row, lw, w_in_b):
    bsz, seq, _ = h.shape
    return pl.pallas_call(
        _ctx_kv_kernel,
        out_shape=jax.ShapeDtypeStruct((bsz, seq, 256), F32),
        grid_spec=pltpu.PrefetchScalarGridSpec(
            num_scalar_prefetch=1,
            grid=(bsz, 1),
            in_specs=[
                pl.BlockSpec((None, seq, D_MODEL), lambda b, j, l: (b, 0, 0)),
                pl.BlockSpec((None, None, 1, 6 * D_MODEL), lambda b, j, l: (l[0], mod_row, 0, 0)),
                _layer_spec(lw["vecs"].shape[1:]),
                pl.BlockSpec((None, D_MODEL, 256), lambda b, j, l: (0, 0, K_OFF // 256),
                             pipeline_mode=pl.Buffered(1)),
            ],
            out_specs=pl.BlockSpec((None, seq, 256), lambda b, j, l: (b, 0, 0)),
        ),
        compiler_params=pltpu.CompilerParams(dimension_semantics=("parallel", "parallel"),
                                             vmem_limit_bytes=VMEM_LIMIT),
        name="ctx_kv",
    )(layer, h, mods, lw["vecs"], w_in_b)


def _rope_table(length):
    rows = length // GRID_W
    row = np.repeat(np.arange(rows), GRID_W).astype(np.float32)
    col = np.tile(np.arange(GRID_W), rows).astype(np.float32)
    n_freq = HEAD_DIM // 4
    inv = jnp.asarray(ROPE_THETA, F32) ** (-jnp.arange(n_freq, dtype=F32) / n_freq)
    ang_r = jnp.asarray(row)[:, None] * inv[None, :]
    ang_c = jnp.asarray(col)[:, None] * inv[None, :]
    ang = jnp.concatenate([ang_r, ang_r, ang_c, ang_c], axis=-1)
    sign = jnp.asarray(np.where(np.arange(HEAD_DIM) % 32 < 16, -1.0, 1.0), F32)
    return jnp.concatenate([jnp.tile(jnp.cos(ang), (1, 2)), jnp.tile(jnp.sin(ang) * sign, (1, 2))], axis=-1)


def _pack_vectors(conv_w, norm_mix, norm_ff, q_norm, k_norm, sgu_norm, pool_scale):
    depth = conv_w.shape[0]
    pad = lambda a: jnp.pad(a, ((0, 0), (0, 0), (0, D_MODEL - a.shape[-1])))
    row5 = jnp.concatenate([jnp.tile(q_norm, (1, N_Q_HEADS)), jnp.tile(k_norm, (1, N_KV_HEADS)),
                            jnp.zeros((depth, 128), F32), sgu_norm, pool_scale], axis=-1)
    return jnp.concatenate([pad(conv_w), norm_mix[:, None, :], norm_ff[:, None, :], row5[:, None, :],
                            jnp.zeros((depth, 2, D_MODEL), F32)], axis=1)


def kernel(x, c, ctx, c_ctx, norm_mix, norm_ff, w_ada, b_ada, w_in, w_out, conv_w, q_norm, k_norm, sink,
           sgu_norm, w_sgu, b_sgu, w_pool, pool_scale, w_ff1, w_ff2):
    bsz, seq, _ = x.shape
    ctx_len = ctx.shape[1]
    assert bsz + 1 <= MOD_ROWS and seq % LAT_TILE == 0 and ctx_len % BLOCK == 0

    big_w = (w_in, w_out, w_ff1, w_ff2)
    cc = jnp.concatenate([c, c_ctx[None, :], jnp.zeros((MOD_ROWS - bsz - 1, D_MODEL), F32)], axis=0)
    mods, wts = _ada_call(cc, w_ada, b_ada, big_w)
    mods = mods.reshape(DEPTH, MOD_ROWS, 1, 6 * D_MODEL)

    eye = jnp.eye(len(POOL_WINDOWS), dtype=F32)
    lw = dict(
        sink=sink,
        vecs=_pack_vectors(conv_w, norm_mix, norm_ff, q_norm, k_norm, sgu_norm, pool_scale),
        w_sgu=w_sgu.reshape(DEPTH, N_SGU_GROUPS * CHUNK, CHUNK).astype(BF16),
        b_sgu=jnp.repeat(jnp.swapaxes(b_sgu, 1, 2), HEAD_DIM, axis=2),
        w_pool=jnp.einsum("lgcd,gh->lgchd", w_pool, eye).reshape(DEPTH, 256, 256).astype(BF16),
    )
    cs = _rope_table(seq)

    h_lat, h_ctx = x, ctx
    for l in range(DEPTH):
        layer = jnp.full((1,), l, jnp.int32)
        last = l == DEPTH - 1
        if last:
            kv_ctx = _ctx_kv_call(layer, h_ctx, mods, bsz, lw, wts[0])
        else:
            h_ctx, kv_ctx = _layer_call(layer, h_ctx, None, None, mods, bsz, lw, wts, tile=ctx_len, is_ctx=True)
        outs = _layer_call(layer, h_lat, kv_ctx, cs, mods, None, lw, wts, () if last else big_w,
                           tile=LAT_TILE, is_ctx=False)
        h_lat, wts = outs[0], tuple(outs[1:])
    return h_lat
```

```python
import functools

import jax
import jax.numpy as jnp
import numpy as np
from jax import lax
from jax.experimental import pallas as pl
from jax.experimental.pallas import tpu as pltpu

D_MODEL = 1024
DEPTH = 4
GRID_W = 64
HEAD_DIM = 64
N_Q_HEADS = 4
N_KV_HEADS = 2
BLOCK = 128
ROPE_THETA = 10000.0
CHUNK = 128
N_SGU_GROUPS = 4
POOL_WINDOWS = (2, 4, 8, 16)
POOL_HALO = 8
EDGE = 16
D_FF = 4 * D_MODEL
EPS = 1e-6
D_PROJ = 2048

A_H, A_GB, A_GC = 0, 256, 512
Q_OFF, K_OFF, V_OFF = 768, 1024, 1152
C_U, C_V = 1280, 1536
D_OFF = 1792

NEG = -1e30
MOD_ROWS = 16
LAT_TILE = 512
VMEM_LIMIT = 58 * 1024 * 1024

F32 = jnp.float32
BF16 = jnp.bfloat16


def _rms(x, g):
    ms = jnp.mean(x * x, axis=-1, keepdims=True)
    return x * lax.rsqrt(ms + EPS) * g


def _lane_group(shape, width):
    return lax.broadcasted_iota(jnp.int32, shape, len(shape) - 1) // width


def _head_rms(x, g):
    n = x.shape[-1]
    r = lax.broadcasted_iota(jnp.int32, (n, n), 0) // HEAD_DIM
    c = lax.broadcasted_iota(jnp.int32, (n, n), 1) // HEAD_DIM
    ones = jnp.where(r == c, 1.0, 0.0).astype(BF16)
    x2 = x * x
    hi = x2.astype(BF16)
    lo = (x2 - hi.astype(F32)).astype(BF16)
    ms = (jnp.dot(hi, ones, preferred_element_type=F32)
          + jnp.dot(lo, ones, preferred_element_type=F32)) * (1.0 / HEAD_DIM)
    return x * lax.rsqrt(ms + EPS) * g


def _swap_lanes(x, width):
    lane = lax.broadcasted_iota(jnp.int32, x.shape, x.ndim - 1)
    return jnp.where(lane % (2 * width) < width, pltpu.roll(x, 128 - width, x.ndim - 1),
                     pltpu.roll(x, width, x.ndim - 1))


def _rope(x, cs):
    cos, sin = cs[:, 0:128], cs[:, 128:256]
    parts = [x[:, s:s + 128] * cos + _swap_lanes(x[:, s:s + 128], 16) * sin for s in range(0, x.shape[-1], 128)]
    return parts[0] if len(parts) == 1 else jnp.concatenate(parts, axis=-1)


def _modulated_norm(x, g, shift, scale):
    return (_rms(x, g) * (1.0 + scale) + shift).astype(BF16)


def _ada_kernel(cc_ref, w_ref, b_ref, *refs):
    n_cast = (len(refs) - 1) // 2
    o_ref = refs[n_cast]
    cc = cc_ref[...]
    s = cc * jax.nn.sigmoid(cc)
    o_ref[...] = jnp.dot(s.astype(BF16), w_ref[...].astype(BF16), preferred_element_type=F32) + b_ref[...]
    for src_ref, dst_ref in zip(refs[:n_cast], refs[n_cast + 1:]):
        dst_ref[...] = src_ref[...].astype(BF16)


def _ada_call(cc, w_ada, b_ada, big_w):
    nblk = 4
    wb = 6 * D_MODEL // nblk
    steps = DEPTH * nblk
    chunk = lambda l, n: (0, l * nblk + n, 0)
    cast_in = [pl.BlockSpec((None, w.shape[1] // steps, w.shape[2]), chunk) for w in big_w]
    outs = pl.pallas_call(
        _ada_kernel,
        out_shape=(jax.ShapeDtypeStruct((DEPTH, MOD_ROWS, 6 * D_MODEL), F32),)
        + tuple(jax.ShapeDtypeStruct((1,) + w.shape[1:], BF16) for w in big_w),
        grid=(DEPTH, nblk),
        in_specs=[
            pl.BlockSpec((MOD_ROWS, D_MODEL), lambda l, n: (0, 0)),
            pl.BlockSpec((None, D_MODEL, wb), lambda l, n: (l, 0, n)),
            pl.BlockSpec((None, 1, wb), lambda l, n: (l, 0, n)),
        ] + cast_in,
        out_specs=(pl.BlockSpec((None, MOD_ROWS, wb), lambda l, n: (l, 0, n)),) + tuple(cast_in),
        compiler_params=pltpu.CompilerParams(dimension_semantics=("arbitrary", "arbitrary"),
                                             vmem_limit_bytes=VMEM_LIMIT),
        name="ada_mod",
    )(cc, w_ada, b_ada.reshape(DEPTH, 1, 6 * D_MODEL), *big_w)
    return outs[0], tuple(outs[1:])


def _shift_rows(x, k):
    n = x.shape[0]
    return pltpu.roll(x, k % n, 0)


def _conv_mixer(p_ref, halo_prev, halo_next, conv_ref, tile):
    rows = pl.ds(0, tile)
    z = p_ref[rows, A_GC:A_GC + 256] * p_ref[rows, A_H:A_H + 256]
    z_ext = jnp.concatenate([halo_prev, z, halo_next], axis=0)
    z_prev = _shift_rows(z_ext, 1)[POOL_HALO:POOL_HALO + tile]
    z_next = _shift_rows(z_ext, -1)[POOL_HALO:POOL_HALO + tile]
    cw = conv_ref[...]
    y = cw[0:1] * z_prev + cw[1:2] * z + cw[2:3] * z_next
    return p_ref[rows, A_GB:A_GB + 256] * y


def _pool_mixer(p_ref, halo_prev, halo_next, wpool_ref, pscale_ref, tile, seq, tile_start):
    x = p_ref[pl.ds(0, tile), D_OFF:D_OFF + 256]
    x_ext = jnp.concatenate([halo_prev, x, halo_next], axis=0)
    s2 = _shift_rows(x_ext, 1) + x_ext
    s4 = _shift_rows(s2, 1) + _shift_rows(s2, -1)
    s8 = _shift_rows(s4, 2) + _shift_rows(s4, -2)
    s16 = _shift_rows(s8, 4) + _shift_rows(s8, -4)
    grp = _lane_group(x_ext.shape, HEAD_DIM)
    s = jnp.where(grp == 0, s2, jnp.where(grp == 1, s4, jnp.where(grp == 2, s8, s16)))
    s = s[POOL_HALO:POOL_HALO + tile]
    t = tile_start + lax.broadcasted_iota(jnp.int32, (tile, 256), 0)
    half = jnp.left_shift(1, _lane_group((tile, 256), HEAD_DIM))
    cnt = jnp.minimum(t + half, seq) - jnp.maximum(t - half, 0)
    d = s / cnt.astype(F32) - x
    y = jnp.dot(d.astype(BF16), wpool_ref[...], preferred_element_type=F32)
    return y * pscale_ref[...]


def _pair_halves(a, b, half):
    low = lax.broadcasted_iota(jnp.int32, a.shape, 1) < HEAD_DIM
    if half == 0:
        return jnp.where(low, a, pltpu.roll(b, HEAD_DIM, 1))
    return jnp.where(low, pltpu.roll(a, HEAD_DIM, 1), b)


def _sgu_mixer(p_ref, sgun_ref, wsgu_ref, bsgu_ref, tile):
    rows = pl.ds(0, tile)
    nchunk = tile // CHUNK
    assert nchunk % 2 == 0
    vn = _rms(p_ref[rows, C_V:C_V + 256], sgun_ref[...])
    x = [[vn[c * CHUNK:(c + 1) * CHUNK, t * 128:(t + 1) * 128] for t in range(2)] for c in range(nchunk)]
    zg = []
    for g in range(N_SGU_GROUPS):
        t, half = divmod(g, 2)
        rhs = jnp.concatenate([_pair_halves(x[c][t], x[c + 1][t], half) for c in range(0, nchunk, 2)], axis=1)
        w_g = wsgu_ref[g * CHUNK:(g + 1) * CHUNK, :]
        zg.append(jnp.dot(w_g, rhs.astype(BF16), preferred_element_type=F32))
    bias = bsgu_ref[...]
    outs = []
    for c in range(nchunk):
        ct, half = divmod(c, 2)
        tiles = [_pair_halves(zg[2 * t][:, ct * 128:(ct + 1) * 128], zg[2 * t + 1][:, ct * 128:(ct + 1) * 128], half)
                 for t in range(2)]
        outs.append(jnp.concatenate(tiles, axis=1) + bias)
    return p_ref[rows, C_U:C_U + 256] * jnp.concatenate(outs, axis=0)


def _attention_scores(qb, segs, sink):
    low = lax.broadcasted_iota(jnp.int32, (BLOCK, 128), 1) < HEAD_DIM
    t0, t1 = qb[:, 0:128], qb[:, 128:256]
    rows = [jnp.where(low, t0, 0.0), jnp.where(low, pltpu.roll(t0, HEAD_DIM, 1), 0.0),
            jnp.where(low, 0.0, pltpu.roll(t1, HEAD_DIM, 1)), jnp.where(low, 0.0, t1)]
    q4 = jnp.concatenate(rows, axis=0).astype(BF16)
    rb = lax.broadcasted_iota(jnp.int32, (4 * BLOCK, 1), 0) // BLOCK
    sink_col = jnp.where(rb == 0, sink[0], jnp.where(rb == 1, sink[1], jnp.where(rb == 2, sink[2], sink[3])))
    scores = []
    for k, _, bias in segs:
        s = lax.dot_general(q4, k, (((1,), (1,)), ((), ())), preferred_element_type=F32)
        scores.append(s if bias is None else s + bias)
    return scores, sink_col


def _attention_output(scores, sink_col, segs):
    low = lax.broadcasted_iota(jnp.int32, (BLOCK, 128), 1) < HEAD_DIM
    m = sink_col
    for s in scores:
        m = jnp.maximum(m, jnp.max(s, axis=-1, keepdims=True))
    denom = jnp.exp(sink_col - m)
    acc = None
    for s, (_, v, _) in zip(scores, segs):
        pr = jnp.exp(s - m)
        denom = denom + jnp.sum(pr, axis=-1, keepdims=True)
        o = jnp.dot(pr.astype(BF16), v, preferred_element_type=F32)
        acc = o if acc is None else acc + o
    acc = acc / denom
    a = [acc[i * BLOCK:(i + 1) * BLOCK] for i in range(4)]
    out0 = jnp.where(low, a[0], pltpu.roll(a[1], HEAD_DIM, 1))
    out1 = jnp.where(low, pltpu.roll(a[2], HEAD_DIM, 1), a[3])
    return jnp.concatenate([out0, out1], axis=-1)


def _vec_views(vec_ref):
    return dict(conv=vec_ref.at[0:3, 0:256], nmix=vec_ref.at[3:4, :], nff=vec_ref.at[4:5, :],
                qn=vec_ref.at[5:6, 0:256], kn=vec_ref.at[5:6, 256:384], sgun=vec_ref.at[5:6, 512:768],
                pscale=vec_ref.at[5:6, 768:1024])


def _layer_kernel(*refs, tile, seq, n_total, is_ctx, n_cast):
    if is_ctx:
        (l_ref, h_ref, mod_ref, sink_ref, vec_ref, wsgu_ref, bsgu_ref, wpool_ref, win_ref, wout_ref, w1_ref, w2_ref,
         o_ref, kv_ref, p_ref, ycat_ref, hid_ref, h1_ref, f_ref) = refs
        modb_ref = mod_ref
        j = pl.program_id(1)
    else:
        (sched_ref, l_ref, h_ref, hn_ref, cs_ref, csn_ref, kvc_ref, mod_ref, modb_ref, sink_ref, vec_ref, wsgu_ref,
         bsgu_ref, wpool_ref, win_ref, wout_ref, w1_ref, w2_ref) = refs[:18]
        cast_in = refs[18:18 + n_cast]
        o_ref = refs[18 + n_cast]
        cast_out = refs[19 + n_cast:19 + 2 * n_cast]
        (p_ref, ycat_ref, hid_ref, h1_ref, f_ref, bias_ref, kvprev_ref, eprev_ref) = refs[19 + 2 * n_cast:]
        step = pl.program_id(0)
    vv = _vec_views(vec_ref)
    nmix_ref, nff_ref, qn_ref, kn_ref = vv["nmix"], vv["nff"], vv["qn"], vv["kn"]
    conv_ref, sgun_ref, pscale_ref = vv["conv"], vv["sgun"], vv["pscale"]
    if not is_ctx:

        @pl.when(step == 0)
        def _():
            h1_ref[...] = jnp.zeros_like(h1_ref)
            f_ref[...] = jnp.zeros_like(f_ref)
            kvprev_ref[...] = jnp.zeros_like(kvprev_ref)
            eprev_ref[...] = jnp.zeros_like(eprev_ref)
            r = lax.broadcasted_iota(jnp.int32, (4 * BLOCK, 3 * BLOCK), 0) % BLOCK
            col = lax.broadcasted_iota(jnp.int32, (4 * BLOCK, 3 * BLOCK), 1)
            seg = col // BLOCK
            jj = col % BLOCK
            band_prev = (seg == 0) & (jj >= r)
            band_next = (seg == 2) & (jj <= r)
            bias_ref[0] = jnp.where((seg == 1) | band_prev | band_next, 0.0, NEG)
            bias_ref[1] = jnp.where((seg == 1) | band_next, 0.0, NEG)
            bias_ref[2] = jnp.where((seg == 1) | band_prev, 0.0, NEG)

        j = sched_ref[1, step]
        has_prev = j > 0
        has_next = j < seq // tile - 1
    layer = l_ref[0]
    tile_start = j * tile
    mod = mod_ref[...]
    sh1, sc1, g1 = mod[:, 0:D_MODEL], mod[:, D_MODEL:2 * D_MODEL], mod[:, 2 * D_MODEL:3 * D_MODEL]
    rows = pl.ds(0, tile)
    nblk = tile // BLOCK
    st = {}

    ff1_cols = 512

    def ff1(c):
        u = jnp.dot(f_ref[...], w1_ref[:, c:c + ff1_cols], preferred_element_type=F32)
        u = jnp.maximum(u, 0.0)
        hid_ref[:, c:c + ff1_cols] = (u * u).astype(BF16)

    half = D_MODEL // 2

    def ff2(c):
        g2 = modb_ref[:, 5 * D_MODEL + c:5 * D_MODEL + c + half]
        y = jnp.dot(hid_ref[...], w2_ref[:, c:c + half], preferred_element_type=F32)
        o_ref[:, c:c + half] = h1_ref[:, c:c + half] + g2 * y

    back = [functools.partial(ff1, c) for c in range(0, D_FF, ff1_cols)]
    back += [functools.partial(ff2, c) for c in range(0, D_MODEL, half)]

    def norm_in():
        nmix = nmix_ref[...]
        a_main = _modulated_norm(h_ref[...], nmix, sh1, sc1)
        if is_ctx:
            st["a_ext"] = a_main
        else:
            a_next = _modulated_norm(hn_ref[...], nmix, sh1, sc1)
            st["a_ext"] = jnp.concatenate([a_main, a_next[0:EDGE]], axis=0)
            st["a_next"] = a_next

    def in_proj():
        p_ref[...] = jnp.dot(st["a_ext"], win_ref[...], preferred_element_type=F32)
        if not is_ctx:
            st["kv_next"] = jnp.dot(st["a_next"], win_ref[:, K_OFF:K_OFF + 256], preferred_element_type=F32)

    def qk_norm():
        q = _head_rms(p_ref[rows, Q_OFF:K_OFF], qn_ref[...])
        k = _head_rms(p_ref[rows, K_OFF:V_OFF], kn_ref[...])
        if not is_ctx:
            q = _rope(q, cs_ref[...])
            k = _rope(k, cs_ref[...])
        p_ref[rows, Q_OFF:K_OFF] = q * (HEAD_DIM ** -0.5)
        k_tile = k.astype(BF16)
        v_tile = p_ref[rows, V_OFF:V_OFF + 128].astype(BF16)
        if is_ctx:
            kv_ref[:, 0:128] = k
            kv_ref[:, 128:256] = p_ref[rows, V_OFF:V_OFF + 128]
            st["k_ext"], st["v_ext"] = k_tile, v_tile
        else:
            kv_next = st["kv_next"]
            k_next = _rope(_head_rms(kv_next[:, 0:128], kn_ref[...]), csn_ref[...]).astype(BF16)
            v_next = kv_next[:, 128:256].astype(BF16)
            st["k_ext"] = jnp.concatenate([kvprev_ref[:, 0:128], k_tile, k_next], axis=0)
            st["v_ext"] = jnp.concatenate([kvprev_ref[:, 128:256], v_tile, v_next], axis=0)
            kvprev_ref[:, 0:128] = k_tile[tile - BLOCK:tile]
            kvprev_ref[:, 128:256] = v_tile[tile - BLOCK:tile]
            st["k_ctx"] = kvc_ref[:, 0:128].astype(BF16)
            st["v_ctx"] = kvc_ref[:, 128:256].astype(BF16)

    def att_scores(i):
        sink = [sink_ref[layer, n] for n in range(N_Q_HEADS)]
        qb = p_ref[pl.ds(i * BLOCK, BLOCK), Q_OFF:K_OFF]
        if is_ctx:
            segs = [(st["k_ext"], st["v_ext"], None)]
        else:
            which = 0
            if i == 0:
                which = jnp.where(has_prev, 0, 1)
            if i == nblk - 1:
                which = jnp.where(has_next, 0, 2)
            segs = [(st["k_ext"][i * BLOCK:(i + 3) * BLOCK], st["v_ext"][i * BLOCK:(i + 3) * BLOCK],
                     bias_ref[which]),
                    (st["k_ctx"], st["v_ctx"], None)]
        st["att", i] = _attention_scores(qb, segs, sink) + (segs,)

    def att_output(i):
        scores, sink_col, segs = st.pop(("att", i))
        ycat_ref[i * BLOCK:(i + 1) * BLOCK, 256:512] = _attention_output(scores, sink_col, segs).astype(BF16)

    def local_mixers():
        if is_ctx:
            zeros = jnp.zeros((POOL_HALO, 256), F32)
            z_prev = z_next = x_prev = x_next = zeros
        else:
            ne = pl.ds(tile, POOL_HALO)
            le = pl.ds(tile - POOL_HALO, POOL_HALO)
            z_prev = jnp.where(has_prev, eprev_ref[:, 0:256], 0.0)
            z_next = jnp.where(has_next, p_ref[ne, A_GC:A_GC + 256] * p_ref[ne, A_H:A_H + 256], 0.0)
            x_prev = jnp.where(has_prev, eprev_ref[:, 256:512], 0.0)
            x_next = jnp.where(has_next, p_ref[ne, D_OFF:D_OFF + 256], 0.0)
            eprev_ref[:, 0:256] = p_ref[le, A_GC:A_GC + 256] * p_ref[le, A_H:A_H + 256]
            eprev_ref[:, 256:512] = p_ref[le, D_OFF:D_OFF + 256]
        ycat_ref[:, 0:256] = _conv_mixer(p_ref, z_prev, z_next, conv_ref, tile).astype(BF16)
        ycat_ref[:, 512:768] = _sgu_mixer(p_ref, sgun_ref, wsgu_ref, bsgu_ref, tile).astype(BF16)
        ycat_ref[:, 768:1024] = _pool_mixer(p_ref, x_prev, x_next, wpool_ref, pscale_ref,
                                            tile, seq, tile_start).astype(BF16)

    def out_proj():
        st["h1"] = h_ref[...] + g1 * jnp.dot(ycat_ref[...], wout_ref[...], preferred_element_type=F32)

    def norm_ff_f():
        f_ref[...] = _modulated_norm(st["h1"], nff_ref[...], mod[:, 3 * D_MODEL:4 * D_MODEL],
                                     mod[:, 4 * D_MODEL:5 * D_MODEL])

    def norm_ff():
        h1_ref[...] = st["h1"]

    sc = [functools.partial(att_scores, i) for i in range(nblk)]
    av = [functools.partial(att_output, i) for i in range(nblk)]
    if is_ctx:
        order = [norm_in, in_proj, qk_norm] + sc + [local_mixers] + av + [out_proj, norm_ff_f, norm_ff] + back
    else:
        assert nblk == 4 and len(back) == 10

        def cast_next_weights():
            for src_ref, dst_ref in zip(cast_in, cast_out):
                dst_ref[...] = src_ref[...].astype(BF16)

        order = [back[0], back[1], norm_in, in_proj, cast_next_weights, qk_norm, back[2], sc[0], sc[1], back[3], av[0],
                 sc[2], back[4], av[1], sc[3], back[5], av[2], back[6], av[3], local_mixers, out_proj, back[7], back[8],
                 norm_ff_f, back[9], norm_ff]
    for piece in order:
        piece()


def _layer_spec(shape):
    nd = len(shape)
    return pl.BlockSpec((None,) + shape, lambda *g: (g[-1][0],) + (0,) * nd, pipeline_mode=pl.Buffered(1))


def _whole_spec(shape):
    nd = len(shape)
    return pl.BlockSpec((None,) + shape, lambda *g: (0,) * (nd + 1), pipeline_mode=pl.Buffered(1))


def _layer_call(layer, h, kv_ctx, cs, mods, mod_row, lw, wts, next_w=(), *, tile, is_ctx):
    bsz, seq, _ = h.shape
    n_tiles = seq // tile
    n_total = bsz * n_tiles
    if is_ctx:
        grid = (bsz, n_tiles)
        front = lambda g: (g[0], g[1])
        back = front
        prefetch = (layer,)
    else:
        grid = (n_total + 1,)
        steps = np.arange(n_total + 1)
        ft = np.minimum(steps, n_total - 1)
        bt = np.maximum(steps - 1, 0)
        table = np.stack([ft // n_tiles, ft % n_tiles,
                          np.minimum((ft % n_tiles + 1) * (tile // BLOCK), seq // BLOCK - 1),
                          bt // n_tiles, bt % n_tiles, ft]).astype(np.int32)
        front = lambda g: (g[-2][0, g[0]], g[-2][1, g[0]])
        back = lambda g: (g[-2][3, g[0]], g[-2][4, g[0]])
        prefetch = (jnp.asarray(table), layer)
    mod_spec = lambda which: pl.BlockSpec(
        (None, None, 1, 6 * D_MODEL),
        lambda *g: (g[-1][0], which(g)[0] if mod_row is None else mod_row, 0, 0))
    in_specs = [pl.BlockSpec((None, tile, D_MODEL), lambda *g: front(g) + (0,))]
    args = [h]
    if not is_ctx:
        next_blk = lambda g: g[-2][2, g[0]]
        in_specs += [
            pl.BlockSpec((None, BLOCK, D_MODEL), lambda *g: (front(g)[0], next_blk(g), 0)),
            pl.BlockSpec((tile, 256), lambda *g: (front(g)[1], 0)),
            pl.BlockSpec((BLOCK, 256), lambda *g: (next_blk(g), 0)),
            pl.BlockSpec((None, kv_ctx.shape[1], 256), lambda *g: (front(g)[0], 0, 0)),
        ]
        args += [h, cs, cs, kv_ctx]
    in_specs.append(mod_spec(front))
    args.append(mods)
    if not is_ctx:
        in_specs.append(mod_spec(back))
        args.append(mods)
    in_specs += [
        pl.BlockSpec(memory_space=pltpu.SMEM),
        _layer_spec(lw["vecs"].shape[1:]),
        _layer_spec((N_SGU_GROUPS * CHUNK, CHUNK)),
        _layer_spec((CHUNK, 256)),
        _layer_spec((256, 256)),
    ]
    args += [lw["sink"], lw["vecs"], lw["w_sgu"], lw["b_sgu"], lw["w_pool"]]
    in_specs += [_whole_spec(w.shape[1:]) for w in wts]
    args += list(wts)
    assert not (is_ctx and next_w)
    chunk = lambda g: g[-2][5, g[0]]
    cast_shapes, cast_specs = [], []
    for w in next_w:
        rows, cols = w.shape[1] // n_total, w.shape[2]
        in_specs.append(pl.BlockSpec((None, rows, cols),
                                     lambda *g: (jnp.minimum(g[-1][0] + 1, DEPTH - 1), chunk(g), 0)))
        args.append(w)
        cast_shapes.append(jax.ShapeDtypeStruct((1,) + w.shape[1:], BF16))
        cast_specs.append(pl.BlockSpec((None, rows, cols), lambda *g: (0, chunk(g), 0)))
    tile_spec = pl.BlockSpec((None, tile, D_MODEL), lambda *g: back(g) + (0,))
    scratch = [pltpu.VMEM((tile if is_ctx else tile + EDGE, D_PROJ), F32),
               pltpu.VMEM((tile, D_MODEL), BF16),
               pltpu.VMEM((tile, D_FF), BF16),
               pltpu.VMEM((tile, D_MODEL), F32),
               pltpu.VMEM((tile, D_MODEL), BF16)]
    if is_ctx:
        out_shape = (jax.ShapeDtypeStruct(h.shape, F32), jax.ShapeDtypeStruct((bsz, seq, 256), F32))
        out_specs = (tile_spec, pl.BlockSpec((None, tile, 256), lambda *g: back(g) + (0,)))
        semantics = ("parallel", "parallel")
    else:
        out_shape = (jax.ShapeDtypeStruct(h.shape, F32),) + tuple(cast_shapes)
        out_specs = (tile_spec,) + tuple(cast_specs)
        scratch += [pltpu.VMEM((3, 4 * BLOCK, 3 * BLOCK), F32),
                    pltpu.VMEM((BLOCK, 256), BF16),
                    pltpu.VMEM((POOL_HALO, 512), F32)]
        semantics = ("arbitrary",)
    return pl.pallas_call(
        functools.partial(_layer_kernel, tile=tile, seq=seq, n_total=n_total, is_ctx=is_ctx, n_cast=len(next_w)),
        out_shape=out_shape,
        grid_spec=pltpu.PrefetchScalarGridSpec(
            num_scalar_prefetch=len(prefetch),
            grid=grid,
            in_specs=in_specs,
            out_specs=out_specs,
            scratch_shapes=scratch,
        ),
        compiler_params=pltpu.CompilerParams(dimension_semantics=semantics,
                                             vmem_limit_bytes=VMEM_LIMIT),
        name="layer_ctx" if is_ctx else "layer_lat",
    )(*prefetch, *args)


def _ctx_kv_kernel(l_ref, h_ref, mod_ref, vec_ref, w_ref, kv_ref):
    mod = mod_ref[...]
    vv = _vec_views(vec_ref)
    a = _modulated_norm(h_ref[...], vv["nmix"][...], mod[:, 0:D_MODEL], mod[:, D_MODEL:2 * D_MODEL])
    kv = jnp.dot(a, w_ref[...], preferred_element_type=F32)
    kv_ref[:, 0:128] = _head_rms(kv[:, 0:128], vv["kn"][...])
    kv_ref[:, 128:256] = kv[:, 128:256]


def _ctx_kv_call(layer, h, mods, mod_row, lw, w_in_b):
    bsz, seq, _ = h.shape
    return pl.pallas_call(
        _ctx_kv_kernel,
        out_shape=jax.ShapeDtypeStruct((bsz, seq, 256), F32),
        grid_spec=pltpu.PrefetchScalarGridSpec(
            num_scalar_prefetch=1,
            grid=(bsz, 1),
            in_specs=[
                pl.BlockSpec((None, seq, D_MODEL), lambda b, j, l: (b, 0, 0)),
                pl.BlockSpec((None, None, 1, 6 * D_MODEL), lambda b, j, l: (l[0], mod_row, 0, 0)),
                _layer_spec(lw["vecs"].shape[1:]),
                pl.BlockSpec((None, D_MODEL, 256), lambda b, j, l: (0, 0, K_OFF // 256),
                             pipeline_mode=pl.Buffered(1)),
            ],
            out_specs=pl.BlockSpec((None, seq, 256), lambda b, j, l: (b, 0, 0)),
        ),
        compiler_params=pltpu.CompilerParams(dimension_semantics=("parallel", "parallel"),
                                             vmem_limit_bytes=VMEM_LIMIT),
        name="ctx_kv",
    )(layer, h, mods, lw["vecs"], w_in_b)


def _rope_table(length):
    rows = length // GRID_W
    row = np.repeat(np.arange(rows), GRID_W).astype(np.float32)
    col = np.tile(np.arange(GRID_W), rows).astype(np.float32)
    n_freq = HEAD_DIM // 4
    inv = jnp.asarray(ROPE_THETA, F32) ** (-jnp.arange(n_freq, dtype=F32) / n_freq)
    ang_r = jnp.asarray(row)[:, None] * inv[None, :]
    ang_c = jnp.asarray(col)[:, None] * inv[None, :]
    ang = jnp.concatenate([ang_r, ang_r, ang_c, ang_c], axis=-1)
    sign = jnp.asarray(np.where(np.arange(HEAD_DIM) % 32 < 16, -1.0, 1.0), F32)
    return jnp.concatenate([jnp.tile(jnp.cos(ang), (1, 2)), jnp.tile(jnp.sin(ang) * sign, (1, 2))], axis=-1)


def _pack_vectors(conv_w, norm_mix, norm_ff, q_norm, k_norm, sgu_norm, pool_scale):
    depth = conv_w.shape[0]
    pad = lambda a: jnp.pad(a, ((0, 0), (0, 0), (0, D_MODEL - a.shape[-1])))
    row5 = jnp.concatenate([jnp.tile(q_norm, (1, N_Q_HEADS)), jnp.tile(k_norm, (1, N_KV_HEADS)),
                            jnp.zeros((depth, 128), F32), sgu_norm, pool_scale], axis=-1)
    return jnp.concatenate([pad(conv_w), norm_mix[:, None, :], norm_ff[:, None, :], row5[:, None, :],
                            jnp.zeros((depth, 2, D_MODEL), F32)], axis=1)


def kernel(x, c, ctx, c_ctx, norm_mix, norm_ff, w_ada, b_ada, w_in, w_out, conv_w, q_norm, k_norm, sink,
           sgu_norm, w_sgu, b_sgu, w_pool, pool_scale, w_ff1, w_ff2):
    bsz, seq, _ = x.shape
    ctx_len = ctx.shape[1]
    assert bsz + 1 <= MOD_ROWS and seq % LAT_TILE == 0 and ctx_len % BLOCK == 0

    big_w = (w_in, w_out, w_ff1, w_ff2)
    cc = jnp.concatenate([c, c_ctx[None, :], jnp.zeros((MOD_ROWS - bsz - 1, D_MODEL), F32)], axis=0)
    mods, wts = _ada_call(cc, w_ada, b_ada, big_w)
    mods = mods.reshape(DEPTH, MOD_ROWS, 1, 6 * D_MODEL)

    eye = jnp.eye(len(POOL_WINDOWS), dtype=F32)
    lw = dict(
        sink=sink,
        vecs=_pack_vectors(conv_w, norm_mix, norm_ff, q_norm, k_norm, sgu_norm, pool_scale),
        w_sgu=w_sgu.reshape(DEPTH, N_SGU_GROUPS * CHUNK, CHUNK).astype(BF16),
        b_sgu=jnp.repeat(jnp.swapaxes(b_sgu, 1, 2), HEAD_DIM, axis=2),
        w_pool=jnp.einsum("lgcd,gh->lgchd", w_pool, eye).reshape(DEPTH, 256, 256).astype(BF16),
    )
    cs = _rope_table(seq)

    h_lat, h_ctx = x, ctx
    for l in range(DEPTH):
        layer = jnp.full((1,), l, jnp.int32)
        last = l == DEPTH - 1
        if last:
            kv_ctx = _ctx_kv_call(layer, h_ctx, mods, bsz, lw, wts[0])
        else:
            h_ctx, kv_ctx = _layer_call(layer, h_ctx, None, None, mods, bsz, lw, wts, tile=ctx_len, is_ctx=True)
        outs = _layer_call(layer, h_lat, kv_ctx, cs, mods, None, lw, wts, () if last else big_w,
                           tile=LAT_TILE, is_ctx=False)
        h_lat, wts = outs[0], tuple(outs[1:])
    return h_lat
```

```python
import functools

import jax
import jax.numpy as jnp
import numpy as np
from jax import lax
from jax.experimental import pallas as pl
from jax.experimental.pallas import tpu as pltpu

D_MODEL = 1024
DEPTH = 4
GRID_W = 64
HEAD_DIM = 64
N_Q_HEADS = 4
N_KV_HEADS = 2
BLOCK = 128
ROPE_THETA = 10000.0
CHUNK = 128
N_SGU_GROUPS = 4
POOL_WINDOWS = (2, 4, 8, 16)
POOL_HALO = 8
EDGE = 16
D_FF = 4 * D_MODEL
EPS = 1e-6
D_PROJ = 2048

A_H, A_GB, A_GC = 0, 256, 512
Q_OFF, K_OFF, V_OFF = 768, 1024, 1152
C_U, C_V = 1280, 1536
D_OFF = 1792

NEG = -1e30
MOD_ROWS = 16
LAT_TILE = 512
VMEM_LIMIT = 58 * 1024 * 1024

F32 = jnp.float32
BF16 = jnp.bfloat16


def _rms(x, g):
    ms = jnp.mean(x * x, axis=-1, keepdims=True)
    return x * lax.rsqrt(ms + EPS) * g


def _lane_group(shape, width):
    return lax.broadcasted_iota(jnp.int32, shape, len(shape) - 1) // width


def _head_rms(x, g):
    n = x.shape[-1]
    r = lax.broadcasted_iota(jnp.int32, (n, n), 0) // HEAD_DIM
    c = lax.broadcasted_iota(jnp.int32, (n, n), 1) // HEAD_DIM
    ones = jnp.where(r == c, 1.0, 0.0).astype(BF16)
    x2 = x * x
    hi = x2.astype(BF16)
    lo = (x2 - hi.astype(F32)).astype(BF16)
    ms = (jnp.dot(hi, ones, preferred_element_type=F32)
          + jnp.dot(lo, ones, preferred_element_type=F32)) * (1.0 / HEAD_DIM)
    return x * lax.rsqrt(ms + EPS) * g


def _swap_lanes(x, width):
    lane = lax.broadcasted_iota(jnp.int32, x.shape, x.ndim - 1)
    return jnp.where(lane % (2 * width) < width, pltpu.roll(x, 128 - width, x.ndim - 1),
                     pltpu.roll(x, width, x.ndim - 1))


def _rope(x, cs):
    cos, sin = cs[:, 0:128], cs[:, 128:256]
    parts = [x[:, s:s + 128] * cos + _swap_lanes(x[:, s:s + 128], 16) * sin for s in range(0, x.shape[-1], 128)]
    return parts[0] if len(parts) == 1 else jnp.concatenate(parts, axis=-1)


def _modulated_norm(x, g, shift, scale):
    return (_rms(x, g) * (1.0 + scale) + shift).astype(BF16)


def _ada_kernel(cc_ref, w_ref, b_ref, *refs):
    n_cast = (len(refs) - 1) // 2
    o_ref = refs[n_cast]
    cc = cc_ref[...]
    s = cc * jax.nn.sigmoid(cc)
    o_ref[...] = jnp.dot(s.astype(BF16), w_ref[...].astype(BF16), preferred_element_type=F32) + b_ref[...]
    for src_ref, dst_ref in zip(refs[:n_cast], refs[n_cast + 1:]):
        dst_ref[...] = src_ref[...].astype(BF16)


def _ada_call(cc, w_ada, b_ada, big_w):
    nblk = 4
    wb = 6 * D_MODEL // nblk
    steps = DEPTH * nblk
    chunk = lambda l, n: (0, l * nblk + n, 0)
    cast_in = [pl.BlockSpec((None, w.shape[1] // steps, w.shape[2]), chunk) for w in big_w]
    outs = pl.pallas_call(
        _ada_kernel,
        out_shape=(jax.ShapeDtypeStruct((DEPTH, MOD_ROWS, 6 * D_MODEL), F32),)
        + tuple(jax.ShapeDtypeStruct((1,) + w.shape[1:], BF16) for w in big_w),
        grid=(DEPTH, nblk),
        in_specs=[
            pl.BlockSpec((MOD_ROWS, D_MODEL), lambda l, n: (0, 0)),
            pl.BlockSpec((None, D_MODEL, wb), lambda l, n: (l, 0, n)),
            pl.BlockSpec((None, 1, wb), lambda l, n: (l, 0, n)),
        ] + cast_in,
        out_specs=(pl.BlockSpec((None, MOD_ROWS, wb), lambda l, n: (l, 0, n)),) + tuple(cast_in),
        compiler_params=pltpu.CompilerParams(dimension_semantics=("arbitrary", "arbitrary"),
                                             vmem_limit_bytes=VMEM_LIMIT),
        name="ada_mod",
    )(cc, w_ada, b_ada.reshape(DEPTH, 1, 6 * D_MODEL), *big_w)
    return outs[0], tuple(outs[1:])


def _shift_rows(x, k):
    n = x.shape[0]
    return pltpu.roll(x, k % n, 0)


def _conv_mixer(p_ref, halo_prev, halo_next, conv_ref, tile):
    rows = pl.ds(0, tile)
    z = p_ref[rows, A_GC:A_GC + 256] * p_ref[rows, A_H:A_H + 256]
    z_ext = jnp.concatenate([halo_prev, z, halo_next], axis=0)
    z_prev = _shift_rows(z_ext, 1)[POOL_HALO:POOL_HALO + tile]
    z_next = _shift_rows(z_ext, -1)[POOL_HALO:POOL_HALO + tile]
    cw = conv_ref[...]
    y = cw[0:1] * z_prev + cw[1:2] * z + cw[2:3] * z_next
    return p_ref[rows, A_GB:A_GB + 256] * y


def _pool_mixer(p_ref, halo_prev, halo_next, wpool_ref, pscale_ref, tile, seq, tile_start):
    x = p_ref[pl.ds(0, tile), D_OFF:D_OFF + 256]
    x_ext = jnp.concatenate([halo_prev, x, halo_next], axis=0)
    s2 = _shift_rows(x_ext, 1) + x_ext
    s4 = _shift_rows(s2, 1) + _shift_rows(s2, -1)
    s8 = _shift_rows(s4, 2) + _shift_rows(s4, -2)
    s16 = _shift_rows(s8, 4) + _shift_rows(s8, -4)
    grp = _lane_group(x_ext.shape, HEAD_DIM)
    s = jnp.where(grp == 0, s2, jnp.where(grp == 1, s4, jnp.where(grp == 2, s8, s16)))
    s = s[POOL_HALO:POOL_HALO + tile]
    t = tile_start + lax.broadcasted_iota(jnp.int32, (tile, 256), 0)
    half = jnp.left_shift(1, _lane_group((tile, 256), HEAD_DIM))
    cnt = jnp.minimum(t + half, seq) - jnp.maximum(t - half, 0)
    d = s / cnt.astype(F32) - x
    y = jnp.dot(d.astype(BF16), wpool_ref[...], preferred_element_type=F32)
    return y * pscale_ref[...]


def _pair_halves(a, b, half):
    low = lax.broadcasted_iota(jnp.int32, a.shape, 1) < HEAD_DIM
    if half == 0:
        return jnp.where(low, a, pltpu.roll(b, HEAD_DIM, 1))
    return jnp.where(low, pltpu.roll(a, HEAD_DIM, 1), b)


def _sgu_mixer(p_ref, sgun_ref, wsgu_ref, bsgu_ref, tile):
    rows = pl.ds(0, tile)
    nchunk = tile // CHUNK
    assert nchunk % 2 == 0
    vn = _rms(p_ref[rows, C_V:C_V + 256], sgun_ref[...])
    x = [[vn[c * CHUNK:(c + 1) * CHUNK, t * 128:(t + 1) * 128] for t in range(2)] for c in range(nchunk)]
    zg = []
    for g in range(N_SGU_GROUPS):
        t, half = divmod(g, 2)
        rhs = jnp.concatenate([_pair_halves(x[c][t], x[c + 1][t], half) for c in range(0, nchunk, 2)], axis=1)
        w_g = wsgu_ref[g * CHUNK:(g + 1) * CHUNK, :]
        zg.append(jnp.dot(w_g, rhs.astype(BF16), preferred_element_type=F32))
    bias = bsgu_ref[...]
    outs = []
    for c in range(nchunk):
        ct, half = divmod(c, 2)
        tiles = [_pair_halves(zg[2 * t][:, ct * 128:(ct + 1) * 128], zg[2 * t + 1][:, ct * 128:(ct + 1) * 128], half)
                 for t in range(2)]
        outs.append(jnp.concatenate(tiles, axis=1) + bias)
    return p_ref[rows, C_U:C_U + 256] * jnp.concatenate(outs, axis=0)


def _attention_scores(qb, segs, sink):
    low = lax.broadcasted_iota(jnp.int32, (BLOCK, 128), 1) < HEAD_DIM
    t0, t1 = qb[:, 0:128], qb[:, 128:256]
    rows = [jnp.where(low, t0, 0.0), jnp.where(low, pltpu.roll(t0, HEAD_DIM, 1), 0.0),
            jnp.where(low, 0.0, pltpu.roll(t1, HEAD_DIM, 1)), jnp.where(low, 0.0, t1)]
    q4 = jnp.concatenate(rows, axis=0).astype(BF16)
    rb = lax.broadcasted_iota(jnp.int32, (4 * BLOCK, 1), 0) // BLOCK
    sink_col = jnp.where(rb == 0, sink[0], jnp.where(rb == 1, sink[1], jnp.where(rb == 2, sink[2], sink[3])))
    scores = []
    for k, _, bias in segs:
        s = lax.dot_general(q4, k, (((1,), (1,)), ((), ())), preferred_element_type=F32)
        scores.append(s if bias is None else s + bias)
    return scores, sink_col


def _attention_output(scores, sink_col, segs):
    low = lax.broadcasted_iota(jnp.int32, (BLOCK, 128), 1) < HEAD_DIM
    m = sink_col
    for s in scores:
        m = jnp.maximum(m, jnp.max(s, axis=-1, keepdims=True))
    denom = jnp.exp(sink_col - m)
    acc = None
    for s, (_, v, _) in zip(scores, segs):
        pr = jnp.exp(s - m)
        denom = denom + jnp.sum(pr, axis=-1, keepdims=True)
        o = jnp.dot(pr.astype(BF16), v, preferred_element_type=F32)
        acc = o if acc is None else acc + o
    acc = acc / denom
    a = [acc[i * BLOCK:(i + 1) * BLOCK] for i in range(4)]
    out0 = jnp.where(low, a[0], pltpu.roll(a[1], HEAD_DIM, 1))
    out1 = jnp.where(low, pltpu.roll(a[2], HEAD_DIM, 1), a[3])
    return jnp.concatenate([out0, out1], axis=-1)


def _vec_views(vec_ref):
    return dict(conv=vec_ref.at[0:3, 0:256], nmix=vec_ref.at[3:4, :], nff=vec_ref.at[4:5, :],
                qn=vec_ref.at[5:6, 0:256], kn=vec_ref.at[5:6, 256:384], sgun=vec_ref.at[5:6, 512:768],
                pscale=vec_ref.at[5:6, 768:1024])


def _layer_kernel(*refs, tile, seq, n_total, is_ctx, n_cast):
    if is_ctx:
        (l_ref, h_ref, mod_ref, sink_ref, vec_ref, wsgu_ref, bsgu_ref, wpool_ref, win_ref, wout_ref, w1_ref, w2_ref,
         o_ref, kv_ref, p_ref, ycat_ref, hid_ref, h1_ref, f_ref) = refs
        modb_ref = mod_ref
        j = pl.program_id(1)
    else:
        (sched_ref, l_ref, h_ref, hn_ref, cs_ref, csn_ref, kvc_ref, mod_ref, modb_ref, sink_ref, vec_ref, wsgu_ref,
         bsgu_ref, wpool_ref, win_ref, wout_ref, w1_ref, w2_ref) = refs[:18]
        cast_in = refs[18:18 + n_cast]
        o_ref = refs[18 + n_cast]
        cast_out = refs[19 + n_cast:19 + 2 * n_cast]
        (p_ref, ycat_ref, hid_ref, h1_ref, f_ref, bias_ref, kvprev_ref, eprev_ref) = refs[19 + 2 * n_cast:]
        step = pl.program_id(0)
    vv = _vec_views(vec_ref)
    nmix_ref, nff_ref, qn_ref, kn_ref = vv["nmix"], vv["nff"], vv["qn"], vv["kn"]
    conv_ref, sgun_ref, pscale_ref = vv["conv"], vv["sgun"], vv["pscale"]
    if not is_ctx:

        @pl.when(step == 0)
        def _():
            h1_ref[...] = jnp.zeros_like(h1_ref)
            f_ref[...] = jnp.zeros_like(f_ref)
            kvprev_ref[...] = jnp.zeros_like(kvprev_ref)
            eprev_ref[...] = jnp.zeros_like(eprev_ref)
            r = lax.broadcasted_iota(jnp.int32, (4 * BLOCK, 3 * BLOCK), 0) % BLOCK
            col = lax.broadcasted_iota(jnp.int32, (4 * BLOCK, 3 * BLOCK), 1)
            seg = col // BLOCK
            jj = col % BLOCK
            band_prev = (seg == 0) & (jj >= r)
            band_next = (seg == 2) & (jj <= r)
            bias_ref[0] = jnp.where((seg == 1) | band_prev | band_next, 0.0, NEG)
            bias_ref[1] = jnp.where((seg == 1) | band_next, 0.0, NEG)
            bias_ref[2] = jnp.where((seg == 1) | band_prev, 0.0, NEG)

        j = sched_ref[1, step]
        has_prev = j > 0
        has_next = j < seq // tile - 1
    layer = l_ref[0]
    tile_start = j * tile
    mod = mod_ref[...]
    sh1, sc1, g1 = mod[:, 0:D_MODEL], mod[:, D_MODEL:2 * D_MODEL], mod[:, 2 * D_MODEL:3 * D_MODEL]
    rows = pl.ds(0, tile)
    nblk = tile // BLOCK
    st = {}

    ff1_cols = 512

    def ff1(c):
        u = jnp.dot(f_ref[...], w1_ref[:, c:c + ff1_cols], preferred_element_type=F32)
        u = jnp.maximum(u, 0.0)
        hid_ref[:, c:c + ff1_cols] = (u * u).astype(BF16)

    half = D_MODEL // 2

    def ff2(c):
        g2 = modb_ref[:, 5 * D_MODEL + c:5 * D_MODEL + c + half]
        y = jnp.dot(hid_ref[...], w2_ref[:, c:c + half], preferred_element_type=F32)
        o_ref[:, c:c + half] = h1_ref[:, c:c + half] + g2 * y

    back = [functools.partial(ff1, c) for c in range(0, D_FF, ff1_cols)]
    back += [functools.partial(ff2, c) for c in range(0, D_MODEL, half)]

    def norm_in():
        nmix = nmix_ref[...]
        a_main = _modulated_norm(h_ref[...], nmix, sh1, sc1)
        if is_ctx:
            st["a_ext"] = a_main
        else:
            a_next = _modulated_norm(hn_ref[...], nmix, sh1, sc1)
            st["a_ext"] = jnp.concatenate([a_main, a_next[0:EDGE]], axis=0)
            st["a_next"] = a_next

    def in_proj():
        p_ref[...] = jnp.dot(st["a_ext"], win_ref[...], preferred_element_type=F32)
        if not is_ctx:
            st["kv_next"] = jnp.dot(st["a_next"], win_ref[:, K_OFF:K_OFF + 256], preferred_element_type=F32)

    def qk_norm():
        q = _head_rms(p_ref[rows, Q_OFF:K_OFF], qn_ref[...])
        k = _head_rms(p_ref[rows, K_OFF:V_OFF], kn_ref[...])
        if not is_ctx:
            q = _rope(q, cs_ref[...])
            k = _rope(k, cs_ref[...])
        p_ref[rows, Q_OFF:K_OFF] = q * (HEAD_DIM ** -0.5)
        k_tile = k.astype(BF16)
        v_tile = p_ref[rows, V_OFF:V_OFF + 128].astype(BF16)
        if is_ctx:
            kv_ref[:, 0:128] = k
            kv_ref[:, 128:256] = p_ref[rows, V_OFF:V_OFF + 128]
            st["k_ext"], st["v_ext"] = k_tile, v_tile
        else:
            kv_next = st["kv_next"]
            k_next = _rope(_head_rms(kv_next[:, 0:128], kn_ref[...]), csn_ref[...]).astype(BF16)
            v_next = kv_next[:, 128:256].astype(BF16)
            st["k_ext"] = jnp.concatenate([kvprev_ref[:, 0:128], k_tile, k_next], axis=0)
            st["v_ext"] = jnp.concatenate([kvprev_ref[:, 128:256], v_tile, v_next], axis=0)
            kvprev_ref[:, 0:128] = k_tile[tile - BLOCK:tile]
            kvprev_ref[:, 128:256] = v_tile[tile - BLOCK:tile]
            st["k_ctx"] = kvc_ref[:, 0:128].astype(BF16)
            st["v_ctx"] = kvc_ref[:, 128:256].astype(BF16)

    def att_scores(i):
        sink = [sink_ref[layer, n] for n in range(N_Q_HEADS)]
        qb = p_ref[pl.ds(i * BLOCK, BLOCK), Q_OFF:K_OFF]
        if is_ctx:
            segs = [(st["k_ext"], st["v_ext"], None)]
        else:
            which = 0
            if i == 0:
                which = jnp.where(has_prev, 0, 1)
            if i == nblk - 1:
                which = jnp.where(has_next, 0, 2)
            segs = [(st["k_ext"][i * BLOCK:(i + 3) * BLOCK], st["v_ext"][i * BLOCK:(i + 3) * BLOCK],
                     bias_ref[which]),
                    (st["k_ctx"], st["v_ctx"], None)]
        st["att", i] = _attention_scores(qb, segs, sink) + (segs,)

    def att_output(i):
        scores, sink_col, segs = st.pop(("att", i))
        ycat_ref[i * BLOCK:(i + 1) * BLOCK, 256:512] = _attention_output(scores, sink_col, segs).astype(BF16)

    def local_mixers():
        if is_ctx:
            zeros = jnp.zeros((POOL_HALO, 256), F32)
            z_prev = z_next = x_prev = x_next = zeros
        else:
            ne = pl.ds(tile, POOL_HALO)
            le = pl.ds(tile - POOL_HALO, POOL_HALO)
            z_prev = jnp.where(has_prev, eprev_ref[:, 0:256], 0.0)
            z_next = jnp.where(has_next, p_ref[ne, A_GC:A_GC + 256] * p_ref[ne, A_H:A_H + 256], 0.0)
            x_prev = jnp.where(has_prev, eprev_ref[:, 256:512], 0.0)
            x_next = jnp.where(has_next, p_ref[ne, D_OFF:D_OFF + 256], 0.0)
            eprev_ref[:, 0:256] = p_ref[le, A_GC:A_GC + 256] * p_ref[le, A_H:A_H + 256]
            eprev_ref[:, 256:512] = p_ref[le, D_OFF:D_OFF + 256]
        ycat_ref[:, 0:256] = _conv_mixer(p_ref, z_prev, z_next, conv_ref, tile).astype(BF16)
        ycat_ref[:, 512:768] = _sgu_mixer(p_ref, sgun_ref, wsgu_ref, bsgu_ref, tile).astype(BF16)
        ycat_ref[:, 768:1024] = _pool_mixer(p_ref, x_prev, x_next, wpool_ref, pscale_ref,
                                            tile, seq, tile_start).astype(BF16)

    def out_proj():
        st["h1"] = h_ref[...] + g1 * jnp.dot(ycat_ref[...], wout_ref[...], preferred_element_type=F32)

    def norm_ff_f():
        f_ref[...] = _modulated_norm(st["h1"], nff_ref[...], mod[:, 3 * D_MODEL:4 * D_MODEL],
                                     mod[:, 4 * D_MODEL:5 * D_MODEL])

    def norm_ff():
        h1_ref[...] = st["h1"]

    sc = [functools.partial(att_scores, i) for i in range(nblk)]
    av = [functools.partial(att_output, i) for i in range(nblk)]
    if is_ctx:
        order = [norm_in, in_proj, qk_norm] + sc + [local_mixers] + av + [out_proj, norm_ff_f, norm_ff] + back
    else:
        assert nblk == 4 and len(back) == 10

        def cast_next_weights():
            for src_ref, dst_ref in zip(cast_in, cast_out):
                dst_ref[...] = src_ref[...].astype(BF16)

        order = [back[0], back[1], norm_in, in_proj, cast_next_weights, qk_norm, back[2], sc[0], sc[1], back[3], av[0],
                 sc[2], back[4], av[1], sc[3], back[5], av[2], back[6], av[3], local_mixers, out_proj, back[7], back[8],
                 norm_ff_f, back[9], norm_ff]
        first = [norm_in, in_proj, cast_next_weights, qk_norm, sc[0], sc[1], local_mixers, av[0], sc[2], av[1], sc[3],
                 av[2], av[3], out_proj, norm_ff_f, norm_ff]

        def run(pieces):
            st.clear()
            for piece in pieces:
                piece()

        pl.when(step == 0)(functools.partial(run, first))
        pl.when(step > 0)(functools.partial(run, order))
        return
    for piece in order:
        piece()


def _layer_spec(shape):
    nd = len(shape)
    return pl.BlockSpec((None,) + shape, lambda *g: (g[-1][0],) + (0,) * nd, pipeline_mode=pl.Buffered(1))


def _whole_spec(shape):
    nd = len(shape)
    return pl.BlockSpec((None,) + shape, lambda *g: (0,) * (nd + 1), pipeline_mode=pl.Buffered(1))


def _layer_call(layer, h, kv_ctx, cs, mods, mod_row, lw, wts, next_w=(), *, tile, is_ctx):
    bsz, seq, _ = h.shape
    n_tiles = seq // tile
    n_total = bsz * n_tiles
    if is_ctx:
        grid = (bsz, n_tiles)
        front = lambda g: (g[0], g[1])
        back = front
        prefetch = (layer,)
    else:
        grid = (n_total + 1,)
        steps = np.arange(n_total + 1)
        ft = np.minimum(steps, n_total - 1)
        bt = np.maximum(steps - 1, 0)
        table = np.stack([ft // n_tiles, ft % n_tiles,
                          np.minimum((ft % n_tiles + 1) * (tile // BLOCK), seq // BLOCK - 1),
                          bt // n_tiles, bt % n_tiles, ft]).astype(np.int32)
        front = lambda g: (g[-2][0, g[0]], g[-2][1, g[0]])
        back = lambda g: (g[-2][3, g[0]], g[-2][4, g[0]])
        prefetch = (jnp.asarray(table), layer)
    mod_spec = lambda which: pl.BlockSpec(
        (None, None, 1, 6 * D_MODEL),
        lambda *g: (g[-1][0], which(g)[0] if mod_row is None else mod_row, 0, 0))
    in_specs = [pl.BlockSpec((None, tile, D_MODEL), lambda *g: front(g) + (0,))]
    args = [h]
    if not is_ctx:
        next_blk = lambda g: g[-2][2, g[0]]
        in_specs += [
            pl.BlockSpec((None, BLOCK, D_MODEL), lambda *g: (front(g)[0], next_blk(g), 0)),
            pl.BlockSpec((tile, 256), lambda *g: (front(g)[1], 0)),
            pl.BlockSpec((BLOCK, 256), lambda *g: (next_blk(g), 0)),
            pl.BlockSpec((None, kv_ctx.shape[1], 256), lambda *g: (front(g)[0], 0, 0)),
        ]
        args += [h, cs, cs, kv_ctx]
    in_specs.append(mod_spec(front))
    args.append(mods)
    if not is_ctx:
        in_specs.append(mod_spec(back))
        args.append(mods)
    in_specs += [
        pl.BlockSpec(memory_space=pltpu.SMEM),
        _layer_spec(lw["vecs"].shape[1:]),
        _layer_spec((N_SGU_GROUPS * CHUNK, CHUNK)),
        _layer_spec((CHUNK, 256)),
        _layer_spec((256, 256)),
    ]
    args += [lw["sink"], lw["vecs"], lw["w_sgu"], lw["b_sgu"], lw["w_pool"]]
    in_specs += [_whole_spec(w.shape[1:]) for w in wts]
    args += list(wts)
    assert not (is_ctx and next_w)
    chunk = lambda g: g[-2][5, g[0]]
    cast_shapes, cast_specs = [], []
    for w in next_w:
        rows, cols = w.shape[1] // n_total, w.shape[2]
        in_specs.append(pl.BlockSpec((None, rows, cols),
                                     lambda *g: (jnp.minimum(g[-1][0] + 1, DEPTH - 1), chunk(g), 0)))
        args.append(w)
        cast_shapes.append(jax.ShapeDtypeStruct((1,) + w.shape[1:], BF16))
        cast_specs.append(pl.BlockSpec((None, rows, cols), lambda *g: (0, chunk(g), 0)))
    tile_spec = pl.BlockSpec((None, tile, D_MODEL), lambda *g: back(g) + (0,))
    scratch = [pltpu.VMEM((tile if is_ctx else tile + EDGE, D_PROJ), F32),
               pltpu.VMEM((tile, D_MODEL), BF16),
               pltpu.VMEM((tile, D_FF), BF16),
               pltpu.VMEM((tile, D_MODEL), F32),
               pltpu.VMEM((tile, D_MODEL), BF16)]
    if is_ctx:
        out_shape = (jax.ShapeDtypeStruct(h.shape, F32), jax.ShapeDtypeStruct((bsz, seq, 256), F32))
        out_specs = (tile_spec, pl.BlockSpec((None, tile, 256), lambda *g: back(g) + (0,)))
        semantics = ("parallel", "parallel")
    else:
        out_shape = (jax.ShapeDtypeStruct(h.shape, F32),) + tuple(cast_shapes)
        out_specs = (tile_spec,) + tuple(cast_specs)
        scratch += [pltpu.VMEM((3, 4 * BLOCK, 3 * BLOCK), F32),
                    pltpu.VMEM((BLOCK, 256), BF16),
                    pltpu.VMEM((POOL_HALO, 512), F32)]
        semantics = ("arbitrary",)
    return pl.pallas_call(
        functools.partial(_layer_kernel, tile=tile, seq=seq, n_total=n_total, is_ctx=is_ctx, n_cast=len(next_w)),
        out_shape=out_shape,
        grid_spec=pltpu.PrefetchScalarGridSpec(
            num_scalar_prefetch=len(prefetch),
            grid=grid,
            in_specs=in_specs,
            out_specs=out_specs,
            scratch_shapes=scratch,
        ),
        compiler_params=pltpu.CompilerParams(dimension_semantics=semantics,
                                             vmem_limit_bytes=VMEM_LIMIT),
        name="layer_ctx" if is_ctx else "layer_lat",
    )(*prefetch, *args)


def _ctx_kv_kernel(l_ref, h_ref, mod_ref, vec_ref, w_ref, kv_ref):
    mod = mod_ref[...]
    vv = _vec_views(vec_ref)
    a = _modulated_norm(h_ref[...], vv["nmix"][...], mod[:, 0:D_MODEL], mod[:, D_MODEL:2 * D_MODEL])
    kv = jnp.dot(a, w_ref[...], preferred_element_type=F32)
    kv_ref[:, 0:128] = _head_rms(kv[:, 0:128], vv["kn"][...])
    kv_ref[:, 128:256] = kv[:, 128:256]


def _ctx_kv_call(layer, h, mods, mod_row, lw, w_in_b):
    bsz, seq, _ = h.shape
    return pl.pallas_call(
        _ctx_kv_kernel,
        out_shape=jax.ShapeDtypeStruct((bsz, seq, 256), F32),
        grid_spec=pltpu.PrefetchScalarGridSpec(
            num_scalar_prefetch=1,
            grid=(bsz, 1),
            in_specs=[
                pl.BlockSpec((None, seq, D_MODEL), lambda b, j, l: (b, 0, 0)),
                pl.BlockSpec((None, None, 1, 6 * D_MODEL), lambda b, j, l: (l[0], mod_row, 0, 0)),
                _layer_spec(lw["vecs"].shape[1:]),
                pl.BlockSpec((None, D_MODEL, 256), lambda b, j, l: (0, 0, K_OFF // 256),
                             pipeline_mode=pl.Buffered(1)),
            ],
            out_specs=pl.BlockSpec((None, seq, 256), lambda b, j, l: (b, 0, 0)),
        ),
        compiler_params=pltpu.CompilerParams(dimension_semantics=("parallel", "parallel"),
                                             vmem_limit_bytes=VMEM_LIMIT),
        name="ctx_kv",
    )(layer, h, mods, lw["vecs"], w_in_b)


def _rope_table(length):
    rows = length // GRID_W
    row = np.repeat(np.arange(rows), GRID_W).astype(np.float32)
    col = np.tile(np.arange(GRID_W), rows).astype(np.float32)
    n_freq = HEAD_DIM // 4
    inv = jnp.asarray(ROPE_THETA, F32) ** (-jnp.arange(n_freq, dtype=F32) / n_freq)
    ang_r = jnp.asarray(row)[:, None] * inv[None, :]
    ang_c = jnp.asarray(col)[:, None] * inv[None, :]
    ang = jnp.concatenate([ang_r, ang_r, ang_c, ang_c], axis=-1)
    sign = jnp.asarray(np.where(np.arange(HEAD_DIM) % 32 < 16, -1.0, 1.0), F32)
    return jnp.concatenate([jnp.tile(jnp.cos(ang), (1, 2)), jnp.tile(jnp.sin(ang) * sign, (1, 2))], axis=-1)


def _pack_vectors(conv_w, norm_mix, norm_ff, q_norm, k_norm, sgu_norm, pool_scale):
    depth = conv_w.shape[0]
    pad = lambda a: jnp.pad(a, ((0, 0), (0, 0), (0, D_MODEL - a.shape[-1])))
    row5 = jnp.concatenate([jnp.tile(q_norm, (1, N_Q_HEADS)), jnp.tile(k_norm, (1, N_KV_HEADS)),
                            jnp.zeros((depth, 128), F32), sgu_norm, pool_scale], axis=-1)
    return jnp.concatenate([pad(conv_w), norm_mix[:, None, :], norm_ff[:, None, :], row5[:, None, :],
                            jnp.zeros((depth, 2, D_MODEL), F32)], axis=1)


def kernel(x, c, ctx, c_ctx, norm_mix, norm_ff, w_ada, b_ada, w_in, w_out, conv_w, q_norm, k_norm, sink,
           sgu_norm, w_sgu, b_sgu, w_pool, pool_scale, w_ff1, w_ff2):
    bsz, seq, _ = x.shape
    ctx_len = ctx.shape[1]
    assert bsz + 1 <= MOD_ROWS and seq % LAT_TILE == 0 and ctx_len % BLOCK == 0

    big_w = (w_in, w_out, w_ff1, w_ff2)
    cc = jnp.concatenate([c, c_ctx[None, :], jnp.zeros((MOD_ROWS - bsz - 1, D_MODEL), F32)], axis=0)
    mods, wts = _ada_call(cc, w_ada, b_ada, big_w)
    mods = mods.reshape(DEPTH, MOD_ROWS, 1, 6 * D_MODEL)

    eye = jnp.eye(len(POOL_WINDOWS), dtype=F32)
    lw = dict(
        sink=sink,
        vecs=_pack_vectors(conv_w, norm_mix, norm_ff, q_norm, k_norm, sgu_norm, pool_scale),
        w_sgu=w_sgu.reshape(DEPTH, N_SGU_GROUPS * CHUNK, CHUNK).astype(BF16),
        b_sgu=jnp.repeat(jnp.swapaxes(b_sgu, 1, 2), HEAD_DIM, axis=2),
        w_pool=jnp.einsum("lgcd,gh->lgchd", w_pool, eye).reshape(DEPTH, 256, 256).astype(BF16),
    )
    cs = _rope_table(seq)

    h_lat, h_ctx = x, ctx
    for l in range(DEPTH):
        layer = jnp.full((1,), l, jnp.int32)
        last = l == DEPTH - 1
        if last:
            kv_ctx = _ctx_kv_call(layer, h_ctx, mods, bsz, lw, wts[0])
        else:
            h_ctx, kv_ctx = _layer_call(layer, h_ctx, None, None, mods, bsz, lw, wts, tile=ctx_len, is_ctx=True)
        outs = _layer_call(layer, h_lat, kv_ctx, cs, mods, None, lw, wts, () if last else big_w,
                           tile=LAT_TILE, is_ctx=False)
        h_lat, wts = outs[0], tuple(outs[1:])
    return h_lat
```

```python
import functools

import jax
import jax.numpy as jnp
import numpy as np
from jax import lax
from jax.experimental import pallas as pl
from jax.experimental.pallas import tpu as pltpu

D_MODEL = 1024
DEPTH = 4
GRID_W = 64
HEAD_DIM = 64
N_Q_HEADS = 4
N_KV_HEADS = 2
BLOCK = 128
ROPE_THETA = 10000.0
CHUNK = 128
N_SGU_GROUPS = 4
POOL_WINDOWS = (2, 4, 8, 16)
POOL_HALO = 8
EDGE = 16
D_FF = 4 * D_MODEL
EPS = 1e-6
D_PROJ = 2048

A_H, A_GB, A_GC = 0, 256, 512
Q_OFF, K_OFF, V_OFF = 768, 1024, 1152
C_U, C_V = 1280, 1536
D_OFF = 1792

NEG = -1e30
MOD_ROWS = 16
LAT_TILE = 512
VMEM_LIMIT = 58 * 1024 * 1024

F32 = jnp.float32
BF16 = jnp.bfloat16


def _rms(x, g):
    ms = jnp.mean(x * x, axis=-1, keepdims=True)
    return x * lax.rsqrt(ms + EPS) * g


def _lane_group(shape, width):
    return lax.broadcasted_iota(jnp.int32, shape, len(shape) - 1) // width


def _head_rms(x, g):
    n = x.shape[-1]
    r = lax.broadcasted_iota(jnp.int32, (n, n), 0) // HEAD_DIM
    c = lax.broadcasted_iota(jnp.int32, (n, n), 1) // HEAD_DIM
    ones = jnp.where(r == c, 1.0, 0.0).astype(BF16)
    x2 = x * x
    hi = x2.astype(BF16)
    lo = (x2 - hi.astype(F32)).astype(BF16)
    ms = (jnp.dot(hi, ones, preferred_element_type=F32)
          + jnp.dot(lo, ones, preferred_element_type=F32)) * (1.0 / HEAD_DIM)
    return x * lax.rsqrt(ms + EPS) * g


def _swap_lanes(x, width):
    lane = lax.broadcasted_iota(jnp.int32, x.shape, x.ndim - 1)
    return jnp.where(lane % (2 * width) < width, pltpu.roll(x, 128 - width, x.ndim - 1),
                     pltpu.roll(x, width, x.ndim - 1))


def _rope(x, cs):
    cos, sin = cs[:, 0:128], cs[:, 128:256]
    parts = [x[:, s:s + 128] * cos + _swap_lanes(x[:, s:s + 128], 16) * sin for s in range(0, x.shape[-1], 128)]
    return parts[0] if len(parts) == 1 else jnp.concatenate(parts, axis=-1)


def _modulated_norm(x, g, shift, scale):
    return (_rms(x, g) * (1.0 + scale) + shift).astype(BF16)


def _ada_kernel(cc_ref, w_ref, b_ref, *refs):
    n_cast = (len(refs) - 1) // 2
    o_ref = refs[n_cast]
    cc = cc_ref[...]
    s = cc * jax.nn.sigmoid(cc)
    o_ref[...] = jnp.dot(s.astype(BF16), w_ref[...].astype(BF16), preferred_element_type=F32) + b_ref[...]
    for src_ref, dst_ref in zip(refs[:n_cast], refs[n_cast + 1:]):
        dst_ref[...] = src_ref[...].astype(BF16)


def _ada_call(cc, w_ada, b_ada, big_w):
    nblk = 4
    wb = 6 * D_MODEL // nblk
    steps = DEPTH * nblk
    chunk = lambda l, n: (0, l * nblk + n, 0)
    cast_in = [pl.BlockSpec((None, w.shape[1] // steps, w.shape[2]), chunk) for w in big_w]
    outs = pl.pallas_call(
        _ada_kernel,
        out_shape=(jax.ShapeDtypeStruct((DEPTH, MOD_ROWS, 6 * D_MODEL), F32),)
        + tuple(jax.ShapeDtypeStruct((1,) + w.shape[1:], BF16) for w in big_w),
        grid=(DEPTH, nblk),
        in_specs=[
            pl.BlockSpec((MOD_ROWS, D_MODEL), lambda l, n: (0, 0)),
            pl.BlockSpec((None, D_MODEL, wb), lambda l, n: (l, 0, n)),
            pl.BlockSpec((None, 1, wb), lambda l, n: (l, 0, n)),
        ] + cast_in,
        out_specs=(pl.BlockSpec((None, MOD_ROWS, wb), lambda l, n: (l, 0, n)),) + tuple(cast_in),
        compiler_params=pltpu.CompilerParams(dimension_semantics=("arbitrary", "arbitrary"),
                                             vmem_limit_bytes=VMEM_LIMIT),
        name="ada_mod",
    )(cc, w_ada, b_ada.reshape(DEPTH, 1, 6 * D_MODEL), *big_w)
    return outs[0], tuple(outs[1:])


def _shift_rows(x, k):
    n = x.shape[0]
    return pltpu.roll(x, k % n, 0)


def _conv_mixer(p_ref, halo_prev, halo_next, conv_ref, tile):
    rows = pl.ds(0, tile)
    z = p_ref[rows, A_GC:A_GC + 256] * p_ref[rows, A_H:A_H + 256]
    z_ext = jnp.concatenate([halo_prev, z, halo_next], axis=0)
    z_prev = _shift_rows(z_ext, 1)[POOL_HALO:POOL_HALO + tile]
    z_next = _shift_rows(z_ext, -1)[POOL_HALO:POOL_HALO + tile]
    cw = conv_ref[...]
    y = cw[0:1] * z_prev + cw[1:2] * z + cw[2:3] * z_next
    return p_ref[rows, A_GB:A_GB + 256] * y


def _pool_mixer(p_ref, halo_prev, halo_next, wpool_ref, pscale_ref, tile, seq, tile_start):
    x = p_ref[pl.ds(0, tile), D_OFF:D_OFF + 256]
    x_ext = jnp.concatenate([halo_prev, x, halo_next], axis=0)
    s2 = _shift_rows(x_ext, 1) + x_ext
    s4 = _shift_rows(s2, 1) + _shift_rows(s2, -1)
    s8 = _shift_rows(s4, 2) + _shift_rows(s4, -2)
    s16 = _shift_rows(s8, 4) + _shift_rows(s8, -4)
    grp = _lane_group(x_ext.shape, HEAD_DIM)
    s = jnp.where(grp == 0, s2, jnp.where(grp == 1, s4, jnp.where(grp == 2, s8, s16)))
    s = s[POOL_HALO:POOL_HALO + tile]
    t = tile_start + lax.broadcasted_iota(jnp.int32, (tile, 256), 0)
    half = jnp.left_shift(1, _lane_group((tile, 256), HEAD_DIM))
    cnt = jnp.minimum(t + half, seq) - jnp.maximum(t - half, 0)
    d = s / cnt.astype(F32) - x
    y = jnp.dot(d.astype(BF16), wpool_ref[...], preferred_element_type=F32)
    return y * pscale_ref[...]


def _pair_halves(a, b, half):
    low = lax.broadcasted_iota(jnp.int32, a.shape, 1) < HEAD_DIM
    if half == 0:
        return jnp.where(low, a, pltpu.roll(b, HEAD_DIM, 1))
    return jnp.where(low, pltpu.roll(a, HEAD_DIM, 1), b)


def _sgu_mixer(p_ref, sgun_ref, wsgu_ref, bsgu_ref, tile):
    rows = pl.ds(0, tile)
    nchunk = tile // CHUNK
    assert nchunk % 2 == 0
    vn = _rms(p_ref[rows, C_V:C_V + 256], sgun_ref[...])
    x = [[vn[c * CHUNK:(c + 1) * CHUNK, t * 128:(t + 1) * 128] for t in range(2)] for c in range(nchunk)]
    zg = []
    for g in range(N_SGU_GROUPS):
        t, half = divmod(g, 2)
        rhs = jnp.concatenate([_pair_halves(x[c][t], x[c + 1][t], half) for c in range(0, nchunk, 2)], axis=1)
        w_g = wsgu_ref[g * CHUNK:(g + 1) * CHUNK, :]
        zg.append(jnp.dot(w_g, rhs.astype(BF16), preferred_element_type=F32))
    bias = bsgu_ref[...]
    outs = []
    for c in range(nchunk):
        ct, half = divmod(c, 2)
        tiles = [_pair_halves(zg[2 * t][:, ct * 128:(ct + 1) * 128], zg[2 * t + 1][:, ct * 128:(ct + 1) * 128], half)
                 for t in range(2)]
        outs.append(jnp.concatenate(tiles, axis=1) + bias)
    return p_ref[rows, C_U:C_U + 256] * jnp.concatenate(outs, axis=0)


def _attention_scores(qb, segs, sink):
    low = lax.broadcasted_iota(jnp.int32, (BLOCK, 128), 1) < HEAD_DIM
    t0, t1 = qb[:, 0:128], qb[:, 128:256]
    rows = [jnp.where(low, t0, 0.0), jnp.where(low, pltpu.roll(t0, HEAD_DIM, 1), 0.0),
            jnp.where(low, 0.0, pltpu.roll(t1, HEAD_DIM, 1)), jnp.where(low, 0.0, t1)]
    q4 = jnp.concatenate(rows, axis=0).astype(BF16)
    rb = lax.broadcasted_iota(jnp.int32, (4 * BLOCK, 1), 0) // BLOCK
    sink_col = jnp.where(rb == 0, sink[0], jnp.where(rb == 1, sink[1], jnp.where(rb == 2, sink[2], sink[3])))
    scores = []
    for k, _, bias in segs:
        s = lax.dot_general(q4, k, (((1,), (1,)), ((), ())), preferred_element_type=F32)
        scores.append(s if bias is None else s + bias)
    return scores, sink_col


def _attention_output(scores, sink_col, segs):
    low = lax.broadcasted_iota(jnp.int32, (BLOCK, 128), 1) < HEAD_DIM
    m = sink_col
    for s in scores:
        m = jnp.maximum(m, jnp.max(s, axis=-1, keepdims=True))
    denom = jnp.exp(sink_col - m)
    acc = None
    for s, (_, v, _) in zip(scores, segs):
        pr = jnp.exp(s - m)
        denom = denom + jnp.sum(pr, axis=-1, keepdims=True)
        o = jnp.dot(pr.astype(BF16), v, preferred_element_type=F32)
        acc = o if acc is None else acc + o
    acc = acc / denom
    a = [acc[i * BLOCK:(i + 1) * BLOCK] for i in range(4)]
    out0 = jnp.where(low, a[0], pltpu.roll(a[1], HEAD_DIM, 1))
    out1 = jnp.where(low, pltpu.roll(a[2], HEAD_DIM, 1), a[3])
    return jnp.concatenate([out0, out1], axis=-1)


def _vec_views(vec_ref):
    return dict(conv=vec_ref.at[0:3, 0:256], nmix=vec_ref.at[3:4, :], nff=vec_ref.at[4:5, :],
                qn=vec_ref.at[5:6, 0:256], kn=vec_ref.at[5:6, 256:384], sgun=vec_ref.at[5:6, 512:768],
                pscale=vec_ref.at[5:6, 768:1024])


def _layer_kernel(*refs, tile, seq, n_total, is_ctx, n_cast):
    if is_ctx:
        (l_ref, h_ref, mod_ref, sink_ref, vec_ref, wsgu_ref, bsgu_ref, wpool_ref, win_ref, wout_ref, w1_ref, w2_ref,
         o_ref, kv_ref, p_ref, ycat_ref, hid_ref, h1_ref, f_ref) = refs
        modb_ref = mod_ref
        j = pl.program_id(1)
    else:
        (sched_ref, l_ref, h_ref, hn_ref, cs_ref, csn_ref, kvc_ref, mod_ref, modb_ref, sink_ref, vec_ref, wsgu_ref,
         bsgu_ref, wpool_ref, win_ref, wout_ref, w1_ref, w2_ref) = refs[:18]
        cast_in = refs[18:18 + n_cast]
        o_ref = refs[18 + n_cast]
        cast_out = refs[19 + n_cast:19 + 2 * n_cast]
        (p_ref, ycat_ref, hid_ref, h1_ref, f_ref, bias_ref, kvprev_ref, eprev_ref) = refs[19 + 2 * n_cast:]
        step = pl.program_id(0)
    vv = _vec_views(vec_ref)
    nmix_ref, nff_ref, qn_ref, kn_ref = vv["nmix"], vv["nff"], vv["qn"], vv["kn"]
    conv_ref, sgun_ref, pscale_ref = vv["conv"], vv["sgun"], vv["pscale"]
    if not is_ctx:

        @pl.when(step == 0)
        def _():
            kvprev_ref[...] = jnp.zeros_like(kvprev_ref)
            eprev_ref[...] = jnp.zeros_like(eprev_ref)
            r = lax.broadcasted_iota(jnp.int32, (4 * BLOCK, 3 * BLOCK), 0) % BLOCK
            col = lax.broadcasted_iota(jnp.int32, (4 * BLOCK, 3 * BLOCK), 1)
            seg = col // BLOCK
            jj = col % BLOCK
            band_prev = (seg == 0) & (jj >= r)
            band_next = (seg == 2) & (jj <= r)
            bias_ref[0] = jnp.where((seg == 1) | band_prev | band_next, 0.0, NEG)
            bias_ref[1] = jnp.where((seg == 1) | band_next, 0.0, NEG)
            bias_ref[2] = jnp.where((seg == 1) | band_prev, 0.0, NEG)

        j = sched_ref[1, step]
        has_prev = j > 0
        has_next = j < seq // tile - 1
    layer = l_ref[0]
    tile_start = j * tile
    mod = mod_ref[...]
    sh1, sc1, g1 = mod[:, 0:D_MODEL], mod[:, D_MODEL:2 * D_MODEL], mod[:, 2 * D_MODEL:3 * D_MODEL]
    rows = pl.ds(0, tile)
    nblk = tile // BLOCK
    st = {}

    ff1_cols = 512

    def ff1(c):
        u = jnp.dot(f_ref[...], w1_ref[:, c:c + ff1_cols], preferred_element_type=F32)
        u = jnp.maximum(u, 0.0)
        hid_ref[:, c:c + ff1_cols] = (u * u).astype(BF16)

    half = D_MODEL // 2

    def ff2(c):
        g2 = modb_ref[:, 5 * D_MODEL + c:5 * D_MODEL + c + half]
        y = jnp.dot(hid_ref[...], w2_ref[:, c:c + half], preferred_element_type=F32)
        o_ref[:, c:c + half] = h1_ref[:, c:c + half] + g2 * y

    back = [functools.partial(ff1, c) for c in range(0, D_FF, ff1_cols)]
    back += [functools.partial(ff2, c) for c in range(0, D_MODEL, half)]

    def norm_in():
        nmix = nmix_ref[...]
        a_main = _modulated_norm(h_ref[...], nmix, sh1, sc1)
        if is_ctx:
            st["a_ext"] = a_main
        else:
            a_next = _modulated_norm(hn_ref[...], nmix, sh1, sc1)
            st["a_ext"] = jnp.concatenate([a_main, a_next[0:EDGE]], axis=0)
            st["a_next"] = a_next

    def in_proj():
        p_ref[...] = jnp.dot(st["a_ext"], win_ref[...], preferred_element_type=F32)
        if not is_ctx:
            st["kv_next"] = jnp.dot(st["a_next"], win_ref[:, K_OFF:K_OFF + 256], preferred_element_type=F32)

    def qk_norm():
        q = _head_rms(p_ref[rows, Q_OFF:K_OFF], qn_ref[...])
        k = _head_rms(p_ref[rows, K_OFF:V_OFF], kn_ref[...])
        if not is_ctx:
            q = _rope(q, cs_ref[...])
            k = _rope(k, cs_ref[...])
        p_ref[rows, Q_OFF:K_OFF] = q * (HEAD_DIM ** -0.5)
        k_tile = k.astype(BF16)
        v_tile = p_ref[rows, V_OFF:V_OFF + 128].astype(BF16)
        if is_ctx:
            kv_ref[:, 0:128] = k
            kv_ref[:, 128:256] = p_ref[rows, V_OFF:V_OFF + 128]
            st["k_ext"], st["v_ext"] = k_tile, v_tile
        else:
            kv_next = st["kv_next"]
            k_next = _rope(_head_rms(kv_next[:, 0:128], kn_ref[...]), csn_ref[...]).astype(BF16)
            v_next = kv_next[:, 128:256].astype(BF16)
            st["k_ext"] = jnp.concatenate([kvprev_ref[:, 0:128], k_tile, k_next], axis=0)
            st["v_ext"] = jnp.concatenate([kvprev_ref[:, 128:256], v_tile, v_next], axis=0)
            kvprev_ref[:, 0:128] = k_tile[tile - BLOCK:tile]
            kvprev_ref[:, 128:256] = v_tile[tile - BLOCK:tile]
            st["k_ctx"] = kvc_ref[:, 0:128].astype(BF16)
            st["v_ctx"] = kvc_ref[:, 128:256].astype(BF16)

    def att_scores(i):
        sink = [sink_ref[layer, n] for n in range(N_Q_HEADS)]
        qb = p_ref[pl.ds(i * BLOCK, BLOCK), Q_OFF:K_OFF]
        if is_ctx:
            segs = [(st["k_ext"], st["v_ext"], None)]
        else:
            which = 0
            if i == 0:
                which = jnp.where(has_prev, 0, 1)
            if i == nblk - 1:
                which = jnp.where(has_next, 0, 2)
            segs = [(st["k_ext"][i * BLOCK:(i + 3) * BLOCK], st["v_ext"][i * BLOCK:(i + 3) * BLOCK],
                     bias_ref[which]),
                    (st["k_ctx"], st["v_ctx"], None)]
        st["att", i] = _attention_scores(qb, segs, sink) + (segs,)

    def att_output(i):
        scores, sink_col, segs = st.pop(("att", i))
        ycat_ref[i * BLOCK:(i + 1) * BLOCK, 256:512] = _attention_output(scores, sink_col, segs).astype(BF16)

    def local_mixers():
        if is_ctx:
            zeros = jnp.zeros((POOL_HALO, 256), F32)
            z_prev = z_next = x_prev = x_next = zeros
        else:
            ne = pl.ds(tile, POOL_HALO)
            le = pl.ds(tile - POOL_HALO, POOL_HALO)
            z_prev = jnp.where(has_prev, eprev_ref[:, 0:256], 0.0)
            z_next = jnp.where(has_next, p_ref[ne, A_GC:A_GC + 256] * p_ref[ne, A_H:A_H + 256], 0.0)
            x_prev = jnp.where(has_prev, eprev_ref[:, 256:512], 0.0)
            x_next = jnp.where(has_next, p_ref[ne, D_OFF:D_OFF + 256], 0.0)
            eprev_ref[:, 0:256] = p_ref[le, A_GC:A_GC + 256] * p_ref[le, A_H:A_H + 256]
            eprev_ref[:, 256:512] = p_ref[le, D_OFF:D_OFF + 256]
        ycat_ref[:, 0:256] = _conv_mixer(p_ref, z_prev, z_next, conv_ref, tile).astype(BF16)
        ycat_ref[:, 512:768] = _sgu_mixer(p_ref, sgun_ref, wsgu_ref, bsgu_ref, tile).astype(BF16)
        ycat_ref[:, 768:1024] = _pool_mixer(p_ref, x_prev, x_next, wpool_ref, pscale_ref,
                                            tile, seq, tile_start).astype(BF16)

    def out_proj():
        st["h1"] = h_ref[...] + g1 * jnp.dot(ycat_ref[...], wout_ref[...], preferred_element_type=F32)

    def norm_ff_f():
        f_ref[...] = _modulated_norm(st["h1"], nff_ref[...], mod[:, 3 * D_MODEL:4 * D_MODEL],
                                     mod[:, 4 * D_MODEL:5 * D_MODEL])

    def norm_ff():
        h1_ref[...] = st["h1"]

    sc = [functools.partial(att_scores, i) for i in range(nblk)]
    av = [functools.partial(att_output, i) for i in range(nblk)]
    if is_ctx:
        order = [norm_in, in_proj, qk_norm] + sc + [local_mixers] + av + [out_proj, norm_ff_f, norm_ff] + back
    else:
        assert nblk == 4 and len(back) == 10

        def cast_next_weights():
            for src_ref, dst_ref in zip(cast_in, cast_out):
                dst_ref[...] = src_ref[...].astype(BF16)

        order = [back[0], back[1], norm_in, in_proj, cast_next_weights, qk_norm, back[2], sc[0], sc[1], back[3], av[0],
                 sc[2], back[4], av[1], sc[3], back[5], av[2], back[6], av[3], local_mixers, out_proj, back[7], back[8],
                 norm_ff_f, back[9], norm_ff]
        first = [norm_in, in_proj, cast_next_weights, qk_norm, sc[0], sc[1], local_mixers, av[0], sc[2], av[1], sc[3],
                 av[2], av[3], out_proj, norm_ff_f, norm_ff]

        def run(pieces):
            st.clear()
            for piece in pieces:
                piece()

        pl.when(step == 0)(functools.partial(run, first))
        pl.when(step > 0)(functools.partial(run, order))
        return
    for piece in order:
        piece()


def _layer_spec(shape):
    nd = len(shape)
    return pl.BlockSpec((None,) + shape, lambda *g: (g[-1][0],) + (0,) * nd, pipeline_mode=pl.Buffered(1))


def _whole_spec(shape):
    nd = len(shape)
    return pl.BlockSpec((None,) + shape, lambda *g: (0,) * (nd + 1), pipeline_mode=pl.Buffered(1))


def _layer_call(layer, h, kv_ctx, cs, mods, mod_row, lw, wts, next_w=(), *, tile, is_ctx):
    bsz, seq, _ = h.shape
    n_tiles = seq // tile
    n_total = bsz * n_tiles
    if is_ctx:
        grid = (bsz, n_tiles)
        front = lambda g: (g[0], g[1])
        back = front
        prefetch = (layer,)
    else:
        grid = (n_total + 1,)
        steps = np.arange(n_total + 1)
        ft = np.minimum(steps, n_total - 1)
        bt = np.maximum(steps - 1, 0)
        table = np.stack([ft // n_tiles, ft % n_tiles,
                          np.minimum((ft % n_tiles + 1) * (tile // BLOCK), seq // BLOCK - 1),
                          bt // n_tiles, bt % n_tiles, ft]).astype(np.int32)
        front = lambda g: (g[-2][0, g[0]], g[-2][1, g[0]])
        back = lambda g: (g[-2][3, g[0]], g[-2][4, g[0]])
        prefetch = (jnp.asarray(table), layer)
    mod_spec = lambda which: pl.BlockSpec(
        (None, None, 1, 6 * D_MODEL),
        lambda *g: (g[-1][0], which(g)[0] if mod_row is None else mod_row, 0, 0))
    in_specs = [pl.BlockSpec((None, tile, D_MODEL), lambda *g: front(g) + (0,))]
    args = [h]
    if not is_ctx:
        next_blk = lambda g: g[-2][2, g[0]]
        in_specs += [
            pl.BlockSpec((None, BLOCK, D_MODEL), lambda *g: (front(g)[0], next_blk(g), 0)),
            pl.BlockSpec((tile, 256), lambda *g: (front(g)[1], 0)),
            pl.BlockSpec((BLOCK, 256), lambda *g: (next_blk(g), 0)),
            pl.BlockSpec((None, kv_ctx.shape[1], 256), lambda *g: (front(g)[0], 0, 0)),
        ]
        args += [h, cs, cs, kv_ctx]
    in_specs.append(mod_spec(front))
    args.append(mods)
    if not is_ctx:
        in_specs.append(mod_spec(back))
        args.append(mods)
    in_specs += [
        pl.BlockSpec(memory_space=pltpu.SMEM),
        _layer_spec(lw["vecs"].shape[1:]),
        _layer_spec((N_SGU_GROUPS * CHUNK, CHUNK)),
        _layer_spec((CHUNK, 256)),
        _layer_spec((256, 256)),
    ]
    args += [lw["sink"], lw["vecs"], lw["w_sgu"], lw["b_sgu"], lw["w_pool"]]
    in_specs += [_whole_spec(w.shape[1:]) for w in wts]
    args += list(wts)
    assert not (is_ctx and next_w)
    chunk = lambda g: g[-2][5, g[0]]
    cast_shapes, cast_specs = [], []
    for w in next_w:
        rows, cols = w.shape[1] // n_total, w.shape[2]
        in_specs.append(pl.BlockSpec((None, rows, cols),
                                     lambda *g: (jnp.minimum(g[-1][0] + 1, DEPTH - 1), chunk(g), 0)))
        args.append(w)
        cast_shapes.append(jax.ShapeDtypeStruct((1,) + w.shape[1:], BF16))
        cast_specs.append(pl.BlockSpec((None, rows, cols), lambda *g: (0, chunk(g), 0)))
    tile_spec = pl.BlockSpec((None, tile, D_MODEL), lambda *g: back(g) + (0,))
    scratch = [pltpu.VMEM((tile if is_ctx else tile + EDGE, D_PROJ), F32),
               pltpu.VMEM((tile, D_MODEL), BF16),
               pltpu.VMEM((tile, D_FF), BF16),
               pltpu.VMEM((tile, D_MODEL), F32),
               pltpu.VMEM((tile, D_MODEL), BF16)]
    if is_ctx:
        out_shape = (jax.ShapeDtypeStruct(h.shape, F32), jax.ShapeDtypeStruct((bsz, seq, 256), F32))
        out_specs = (tile_spec, pl.BlockSpec((None, tile, 256), lambda *g: back(g) + (0,)))
        semantics = ("parallel", "parallel")
    else:
        out_shape = (jax.ShapeDtypeStruct(h.shape, F32),) + tuple(cast_shapes)
        out_specs = (tile_spec,) + tuple(cast_specs)
        scratch += [pltpu.VMEM((3, 4 * BLOCK, 3 * BLOCK), F32),
                    pltpu.VMEM((BLOCK, 256), BF16),
                    pltpu.VMEM((POOL_HALO, 512), F32)]
        semantics = ("arbitrary",)
    return pl.pallas_call(
        functools.partial(_layer_kernel, tile=tile, seq=seq, n_total=n_total, is_ctx=is_ctx, n_cast=len(next_w)),
        out_shape=out_shape,
        grid_spec=pltpu.PrefetchScalarGridSpec(
            num_scalar_prefetch=len(prefetch),
            grid=grid,
            in_specs=in_specs,
            out_specs=out_specs,
            scratch_shapes=scratch,
        ),
        compiler_params=pltpu.CompilerParams(dimension_semantics=semantics,
                                             vmem_limit_bytes=VMEM_LIMIT),
        name="layer_ctx" if is_ctx else "layer_lat",
    )(*prefetch, *args)


def _ctx_kv_kernel(l_ref, h_ref, mod_ref, vec_ref, w_ref, kv_ref):
    mod = mod_ref[...]
    vv = _vec_views(vec_ref)
    a = _modulated_norm(h_ref[...], vv["nmix"][...], mod[:, 0:D_MODEL], mod[:, D_MODEL:2 * D_MODEL])
    kv = jnp.dot(a, w_ref[...], preferred_element_type=F32)
    kv_ref[:, 0:128] = _head_rms(kv[:, 0:128], vv["kn"][...])
    kv_ref[:, 128:256] = kv[:, 128:256]


def _ctx_kv_call(layer, h, mods, mod_row, lw, w_in_b):
    bsz, seq, _ = h.shape
    return pl.pallas_call(
        _ctx_kv_kernel,
        out_shape=jax.ShapeDtypeStruct((bsz, seq, 256), F32),
        grid_spec=pltpu.PrefetchScalarGridSpec(
            num_scalar_prefetch=1,
            grid=(bsz, 1),
            in_specs=[
                pl.BlockSpec((None, seq, D_MODEL), lambda b, j, l: (b, 0, 0)),
                pl.BlockSpec((None, None, 1, 6 * D_MODEL), lambda b, j, l: (l[0], mod_row, 0, 0)),
                _layer_spec(lw["vecs"].shape[1:]),
                pl.BlockSpec((None, D_MODEL, 256), lambda b, j, l: (0, 0, K_OFF // 256),
                             pipeline_mode=pl.Buffered(1)),
            ],
            out_specs=pl.BlockSpec((None, seq, 256), lambda b, j, l: (b, 0, 0)),
        ),
        compiler_params=pltpu.CompilerParams(dimension_semantics=("parallel", "parallel"),
                                             vmem_limit_bytes=VMEM_LIMIT),
        name="ctx_kv",
    )(layer, h, mods, lw["vecs"], w_in_b)


def _rope_table(length):
    rows = length // GRID_W
    row = np.repeat(np.arange(rows), GRID_W).astype(np.float32)
    col = np.tile(np.arange(GRID_W), rows).astype(np.float32)
    n_freq = HEAD_DIM // 4
    inv = jnp.asarray(ROPE_THETA, F32) ** (-jnp.arange(n_freq, dtype=F32) / n_freq)
    ang_r = jnp.asarray(row)[:, None] * inv[None, :]
    ang_c = jnp.asarray(col)[:, None] * inv[None, :]
    ang = jnp.concatenate([ang_r, ang_r, ang_c, ang_c], axis=-1)
    sign = jnp.asarray(np.where(np.arange(HEAD_DIM) % 32 < 16, -1.0, 1.0), F32)
    return jnp.concatenate([jnp.tile(jnp.cos(ang), (1, 2)), jnp.tile(jnp.sin(ang) * sign, (1, 2))], axis=-1)


def _pack_vectors(conv_w, norm_mix, norm_ff, q_norm, k_norm, sgu_norm, pool_scale):
    depth = conv_w.shape[0]
    pad = lambda a: jnp.pad(a, ((0, 0), (0, 0), (0, D_MODEL - a.shape[-1])))
    row5 = jnp.concatenate([jnp.tile(q_norm, (1, N_Q_HEADS)), jnp.tile(k_norm, (1, N_KV_HEADS)),
                            jnp.zeros((depth, 128), F32), sgu_norm, pool_scale], axis=-1)
    return jnp.concatenate([pad(conv_w), norm_mix[:, None, :], norm_ff[:, None, :], row5[:, None, :],
                            jnp.zeros((depth, 2, D_MODEL), F32)], axis=1)


def kernel(x, c, ctx, c_ctx, norm_mix, norm_ff, w_ada, b_ada, w_in, w_out, conv_w, q_norm, k_norm, sink,
           sgu_norm, w_sgu, b_sgu, w_pool, pool_scale, w_ff1, w_ff2):
    bsz, seq, _ = x.shape
    ctx_len = ctx.shape[1]
    assert bsz + 1 <= MOD_ROWS and seq % LAT_TILE == 0 and ctx_len % BLOCK == 0

    big_w = (w_in, w_out, w_ff1, w_ff2)
    cc = jnp.concatenate([c, c_ctx[None, :], jnp.zeros((MOD_ROWS - bsz - 1, D_MODEL), F32)], axis=0)
    mods, wts = _ada_call(cc, w_ada, b_ada, big_w)
    mods = mods.reshape(DEPTH, MOD_ROWS, 1, 6 * D_MODEL)

    eye = jnp.eye(len(POOL_WINDOWS), dtype=F32)
    lw = dict(
        sink=sink,
        vecs=_pack_vectors(conv_w, norm_mix, norm_ff, q_norm, k_norm, sgu_norm, pool_scale),
        w_sgu=w_sgu.reshape(DEPTH, N_SGU_GROUPS * CHUNK, CHUNK).astype(BF16),
        b_sgu=jnp.repeat(jnp.swapaxes(b_sgu, 1, 2), HEAD_DIM, axis=2),
        w_pool=jnp.einsum("lgcd,gh->lgchd", w_pool, eye).reshape(DEPTH, 256, 256).astype(BF16),
    )
    cs = _rope_table(seq)

    h_lat, h_ctx = x, ctx
    for l in range(DEPTH):
        layer = jnp.full((1,), l, jnp.int32)
        last = l == DEPTH - 1
        if last:
            kv_ctx = _ctx_kv_call(layer, h_ctx, mods, bsz, lw, wts[0])
        else:
            h_ctx, kv_ctx = _layer_call(layer, h_ctx, None, None, mods, bsz, lw, wts, tile=ctx_len, is_ctx=True)
        outs = _layer_call(layer, h_lat, kv_ctx, cs, mods, None, lw, wts, () if last else big_w,
                           tile=LAT_TILE, is_ctx=False)
        h_lat, wts = outs[0], tuple(outs[1:])
    return h_lat
```

```python
import functools

import jax
import jax.numpy as jnp
import numpy as np
from jax import lax
from jax.experimental import pallas as pl
from jax.experimental.pallas import tpu as pltpu

D_MODEL = 1024
DEPTH = 4
GRID_W = 64
HEAD_DIM = 64
N_Q_HEADS = 4
N_KV_HEADS = 2
BLOCK = 128
ROPE_THETA = 10000.0
CHUNK = 128
N_SGU_GROUPS = 4
POOL_WINDOWS = (2, 4, 8, 16)
POOL_HALO = 8
EDGE = 16
D_FF = 4 * D_MODEL
EPS = 1e-6
D_PROJ = 2048

A_H, A_GB, A_GC = 0, 256, 512
Q_OFF, K_OFF, V_OFF = 768, 1024, 1152
C_U, C_V = 1280, 1536
D_OFF = 1792

NEG = -1e30
MOD_ROWS = 16
LAT_TILE = 512
CTX_PAIR = 2
VMEM_LIMIT = 58 * 1024 * 1024

F32 = jnp.float32
BF16 = jnp.bfloat16


def _rms(x, g):
    ms = jnp.mean(x * x, axis=-1, keepdims=True)
    return x * lax.rsqrt(ms + EPS) * g


def _lane_group(shape, width):
    return lax.broadcasted_iota(jnp.int32, shape, len(shape) - 1) // width


def _head_rms(x, g):
    n = x.shape[-1]
    r = lax.broadcasted_iota(jnp.int32, (n, n), 0) // HEAD_DIM
    c = lax.broadcasted_iota(jnp.int32, (n, n), 1) // HEAD_DIM
    ones = jnp.where(r == c, 1.0, 0.0).astype(BF16)
    x2 = x * x
    hi = x2.astype(BF16)
    lo = (x2 - hi.astype(F32)).astype(BF16)
    ms = (jnp.dot(hi, ones, preferred_element_type=F32)
          + jnp.dot(lo, ones, preferred_element_type=F32)) * (1.0 / HEAD_DIM)
    return x * lax.rsqrt(ms + EPS) * g


def _swap_lanes(x, width):
    lane = lax.broadcasted_iota(jnp.int32, x.shape, x.ndim - 1)
    return jnp.where(lane % (2 * width) < width, pltpu.roll(x, 128 - width, x.ndim - 1),
                     pltpu.roll(x, width, x.ndim - 1))


def _rope(x, cs):
    cos, sin = cs[:, 0:128], cs[:, 128:256]
    parts = [x[:, s:s + 128] * cos + _swap_lanes(x[:, s:s + 128], 16) * sin for s in range(0, x.shape[-1], 128)]
    return parts[0] if len(parts) == 1 else jnp.concatenate(parts, axis=-1)


def _modulated_norm(x, g, shift, scale):
    return (_rms(x, g) * (1.0 + scale) + shift).astype(BF16)


def _ada_kernel(cc_ref, w_ref, b_ref, *refs):
    n_cast = (len(refs) - 1) // 2
    o_ref = refs[n_cast]
    cc = cc_ref[...]
    s = cc * jax.nn.sigmoid(cc)
    o_ref[...] = jnp.dot(s.astype(BF16), w_ref[...].astype(BF16), preferred_element_type=F32) + b_ref[...]
    for src_ref, dst_ref in zip(refs[:n_cast], refs[n_cast + 1:]):
        dst_ref[...] = src_ref[...].astype(BF16)


def _ada_call(cc, w_ada, b_ada, big_w):
    nblk = 4
    wb = 6 * D_MODEL // nblk
    steps = DEPTH * nblk
    chunk = lambda l, n: (0, l * nblk + n, 0)
    cast_in = [pl.BlockSpec((None, w.shape[1] // steps, w.shape[2]), chunk) for w in big_w]
    outs = pl.pallas_call(
        _ada_kernel,
        out_shape=(jax.ShapeDtypeStruct((DEPTH, MOD_ROWS, 6 * D_MODEL), F32),)
        + tuple(jax.ShapeDtypeStruct((1,) + w.shape[1:], BF16) for w in big_w),
        grid=(DEPTH, nblk),
        in_specs=[
            pl.BlockSpec((MOD_ROWS, D_MODEL), lambda l, n: (0, 0)),
            pl.BlockSpec((None, D_MODEL, wb), lambda l, n: (l, 0, n)),
            pl.BlockSpec((None, 1, wb), lambda l, n: (l, 0, n)),
        ] + cast_in,
        out_specs=(pl.BlockSpec((None, MOD_ROWS, wb), lambda l, n: (l, 0, n)),) + tuple(cast_in),
        compiler_params=pltpu.CompilerParams(dimension_semantics=("arbitrary", "arbitrary"),
                                             vmem_limit_bytes=VMEM_LIMIT),
        name="ada_mod",
    )(cc, w_ada, b_ada.reshape(DEPTH, 1, 6 * D_MODEL), *big_w)
    return outs[0], tuple(outs[1:])


def _shift_rows(x, k):
    n = x.shape[0]
    return pltpu.roll(x, k % n, 0)


def _conv_mixer(p_ref, r0, halo_prev, halo_next, conv_ref, tile):
    rows = pl.ds(r0, tile)
    z = p_ref[rows, A_GC:A_GC + 256] * p_ref[rows, A_H:A_H + 256]
    z_ext = jnp.concatenate([halo_prev, z, halo_next], axis=0)
    z_prev = _shift_rows(z_ext, 1)[POOL_HALO:POOL_HALO + tile]
    z_next = _shift_rows(z_ext, -1)[POOL_HALO:POOL_HALO + tile]
    cw = conv_ref[...]
    y = cw[0:1] * z_prev + cw[1:2] * z + cw[2:3] * z_next
    return p_ref[rows, A_GB:A_GB + 256] * y


def _pool_mixer(p_ref, r0, halo_prev, halo_next, wpool_ref, pscale_ref, tile, seq, tile_start):
    x = p_ref[pl.ds(r0, tile), D_OFF:D_OFF + 256]
    x_ext = jnp.concatenate([halo_prev, x, halo_next], axis=0)
    s2 = _shift_rows(x_ext, 1) + x_ext
    s4 = _shift_rows(s2, 1) + _shift_rows(s2, -1)
    s8 = _shift_rows(s4, 2) + _shift_rows(s4, -2)
    s16 = _shift_rows(s8, 4) + _shift_rows(s8, -4)
    grp = _lane_group(x_ext.shape, HEAD_DIM)
    s = jnp.where(grp == 0, s2, jnp.where(grp == 1, s4, jnp.where(grp == 2, s8, s16)))
    s = s[POOL_HALO:POOL_HALO + tile]
    t = tile_start + lax.broadcasted_iota(jnp.int32, (tile, 256), 0)
    half = jnp.left_shift(1, _lane_group((tile, 256), HEAD_DIM))
    cnt = jnp.minimum(t + half, seq) - jnp.maximum(t - half, 0)
    d = s / cnt.astype(F32) - x
    y = jnp.dot(d.astype(BF16), wpool_ref[...], preferred_element_type=F32)
    return y * pscale_ref[...]


def _pair_halves(a, b, half):
    low = lax.broadcasted_iota(jnp.int32, a.shape, 1) < HEAD_DIM
    if half == 0:
        return jnp.where(low, a, pltpu.roll(b, HEAD_DIM, 1))
    return jnp.where(low, pltpu.roll(a, HEAD_DIM, 1), b)


def _sgu_mixer(p_ref, sgun_ref, wsgu_ref, bsgu_ref, tile):
    rows = pl.ds(0, tile)
    nchunk = tile // CHUNK
    assert nchunk % 2 == 0
    vn = _rms(p_ref[rows, C_V:C_V + 256], sgun_ref[...])
    x = [[vn[c * CHUNK:(c + 1) * CHUNK, t * 128:(t + 1) * 128] for t in range(2)] for c in range(nchunk)]
    zg = []
    for g in range(N_SGU_GROUPS):
        t, half = divmod(g, 2)
        rhs = jnp.concatenate([_pair_halves(x[c][t], x[c + 1][t], half) for c in range(0, nchunk, 2)], axis=1)
        w_g = wsgu_ref[g * CHUNK:(g + 1) * CHUNK, :]
        zg.append(jnp.dot(w_g, rhs.astype(BF16), preferred_element_type=F32))
    bias = bsgu_ref[...]
    outs = []
    for c in range(nchunk):
        ct, half = divmod(c, 2)
        tiles = [_pair_halves(zg[2 * t][:, ct * 128:(ct + 1) * 128], zg[2 * t + 1][:, ct * 128:(ct + 1) * 128], half)
                 for t in range(2)]
        outs.append(jnp.concatenate(tiles, axis=1) + bias)
    return p_ref[rows, C_U:C_U + 256] * jnp.concatenate(outs, axis=0)


def _attention_scores(qb, segs, sink):
    low = lax.broadcasted_iota(jnp.int32, (BLOCK, 128), 1) < HEAD_DIM
    t0, t1 = qb[:, 0:128], qb[:, 128:256]
    rows = [jnp.where(low, t0, 0.0), jnp.where(low, pltpu.roll(t0, HEAD_DIM, 1), 0.0),
            jnp.where(low, 0.0, pltpu.roll(t1, HEAD_DIM, 1)), jnp.where(low, 0.0, t1)]
    q4 = jnp.concatenate(rows, axis=0).astype(BF16)
    rb = lax.broadcasted_iota(jnp.int32, (4 * BLOCK, 1), 0) // BLOCK
    sink_col = jnp.where(rb == 0, sink[0], jnp.where(rb == 1, sink[1], jnp.where(rb == 2, sink[2], sink[3])))
    scores = []
    for k, _, bias in segs:
        s = lax.dot_general(q4, k, (((1,), (1,)), ((), ())), preferred_element_type=F32)
        scores.append(s if bias is None else s + bias)
    return scores, sink_col


def _attention_output(scores, sink_col, segs):
    low = lax.broadcasted_iota(jnp.int32, (BLOCK, 128), 1) < HEAD_DIM
    m = sink_col
    for s in scores:
        m = jnp.maximum(m, jnp.max(s, axis=-1, keepdims=True))
    denom = jnp.exp(sink_col - m)
    acc = None
    for s, (_, v, _) in zip(scores, segs):
        pr = jnp.exp(s - m)
        denom = denom + jnp.sum(pr, axis=-1, keepdims=True)
        o = jnp.dot(pr.astype(BF16), v, preferred_element_type=F32)
        acc = o if acc is None else acc + o
    acc = acc / denom
    a = [acc[i * BLOCK:(i + 1) * BLOCK] for i in range(4)]
    out0 = jnp.where(low, a[0], pltpu.roll(a[1], HEAD_DIM, 1))
    out1 = jnp.where(low, pltpu.roll(a[2], HEAD_DIM, 1), a[3])
    return jnp.concatenate([out0, out1], axis=-1)


def _vec_views(vec_ref):
    return dict(conv=vec_ref.at[0:3, 0:256], nmix=vec_ref.at[3:4, :], nff=vec_ref.at[4:5, :],
                qn=vec_ref.at[5:6, 0:256], kn=vec_ref.at[5:6, 256:384], sgun=vec_ref.at[5:6, 512:768],
                pscale=vec_ref.at[5:6, 768:1024])


def _layer_kernel(*refs, tile, seq, n_total, is_ctx, n_cast):
    if is_ctx:
        (l_ref, h_ref, mod_ref, sink_ref, vec_ref, wsgu_ref, bsgu_ref, wpool_ref, win_ref, wout_ref, w1_ref, w2_ref,
         o_ref, kv_ref, p_ref, ycat_ref, hid_ref, h1_ref, f_ref) = refs
        modb_ref = mod_ref
        j = pl.program_id(1)
    else:
        (sched_ref, l_ref, h_ref, hn_ref, cs_ref, csn_ref, kvc_ref, mod_ref, modb_ref, sink_ref, vec_ref, wsgu_ref,
         bsgu_ref, wpool_ref, win_ref, wout_ref, w1_ref, w2_ref) = refs[:18]
        cast_in = refs[18:18 + n_cast]
        o_ref = refs[18 + n_cast]
        cast_out = refs[19 + n_cast:19 + 2 * n_cast]
        (p_ref, ycat_ref, hid_ref, h1_ref, f_ref, bias_ref, kvprev_ref, eprev_ref) = refs[19 + 2 * n_cast:]
        step = pl.program_id(0)
    vv = _vec_views(vec_ref)
    nmix_ref, nff_ref, qn_ref, kn_ref = vv["nmix"], vv["nff"], vv["qn"], vv["kn"]
    conv_ref, sgun_ref, pscale_ref = vv["conv"], vv["sgun"], vv["pscale"]
    if not is_ctx:

        @pl.when(step == 0)
        def _():
            kvprev_ref[...] = jnp.zeros_like(kvprev_ref)
            eprev_ref[...] = jnp.zeros_like(eprev_ref)
            r = lax.broadcasted_iota(jnp.int32, (4 * BLOCK, 3 * BLOCK), 0) % BLOCK
            col = lax.broadcasted_iota(jnp.int32, (4 * BLOCK, 3 * BLOCK), 1)
            seg = col // BLOCK
            jj = col % BLOCK
            band_prev = (seg == 0) & (jj >= r)
            band_next = (seg == 2) & (jj <= r)
            bias_ref[0] = jnp.where((seg == 1) | band_prev | band_next, 0.0, NEG)
            bias_ref[1] = jnp.where((seg == 1) | band_next, 0.0, NEG)
            bias_ref[2] = jnp.where((seg == 1) | band_prev, 0.0, NEG)

        j = sched_ref[1, step]
        has_prev = j > 0
        has_next = j < seq // tile - 1
    layer = l_ref[0]
    tile_start = j * tile
    mod = mod_ref[...]
    sh1, sc1, g1 = mod[:, 0:D_MODEL], mod[:, D_MODEL:2 * D_MODEL], mod[:, 2 * D_MODEL:3 * D_MODEL]
    rows = pl.ds(0, tile)
    nblk = tile // BLOCK
    st = {}

    ff1_cols = 512

    def ff1(c):
        u = jnp.dot(f_ref[...], w1_ref[:, c:c + ff1_cols], preferred_element_type=F32)
        u = jnp.maximum(u, 0.0)
        hid_ref[:, c:c + ff1_cols] = (u * u).astype(BF16)

    half = D_MODEL // 2

    def ff2(c):
        g2 = modb_ref[:, 5 * D_MODEL + c:5 * D_MODEL + c + half]
        y = jnp.dot(hid_ref[...], w2_ref[:, c:c + half], preferred_element_type=F32)
        o_ref[:, c:c + half] = h1_ref[:, c:c + half] + g2 * y

    back = [functools.partial(ff1, c) for c in range(0, D_FF, ff1_cols)]
    back += [functools.partial(ff2, c) for c in range(0, D_MODEL, half)]

    def norm_in():
        nmix = nmix_ref[...]
        a_main = _modulated_norm(h_ref[...], nmix, sh1, sc1)
        if is_ctx:
            st["a_ext"] = a_main
        else:
            a_next = _modulated_norm(hn_ref[...], nmix, sh1, sc1)
            st["a_ext"] = jnp.concatenate([a_main, a_next[0:EDGE]], axis=0)
            st["a_next"] = a_next

    def in_proj():
        p_ref[...] = jnp.dot(st["a_ext"], win_ref[...], preferred_element_type=F32)
        if not is_ctx:
            st["kv_next"] = jnp.dot(st["a_next"], win_ref[:, K_OFF:K_OFF + 256], preferred_element_type=F32)

    def qk_norm():
        q = _head_rms(p_ref[rows, Q_OFF:K_OFF], qn_ref[...])
        k = _head_rms(p_ref[rows, K_OFF:V_OFF], kn_ref[...])
        if not is_ctx:
            q = _rope(q, cs_ref[...])
            k = _rope(k, cs_ref[...])
        p_ref[rows, Q_OFF:K_OFF] = q * (HEAD_DIM ** -0.5)
        k_tile = k.astype(BF16)
        v_tile = p_ref[rows, V_OFF:V_OFF + 128].astype(BF16)
        if is_ctx:
            kv_ref[:, 0:128] = k
            kv_ref[:, 128:256] = p_ref[rows, V_OFF:V_OFF + 128]
            st["k_ext"], st["v_ext"] = k_tile, v_tile
        else:
            kv_next = st["kv_next"]
            k_next = _rope(_head_rms(kv_next[:, 0:128], kn_ref[...]), csn_ref[...]).astype(BF16)
            v_next = kv_next[:, 128:256].astype(BF16)
            st["k_ext"] = jnp.concatenate([kvprev_ref[:, 0:128], k_tile, k_next], axis=0)
            st["v_ext"] = jnp.concatenate([kvprev_ref[:, 128:256], v_tile, v_next], axis=0)
            kvprev_ref[:, 0:128] = k_tile[tile - BLOCK:tile]
            kvprev_ref[:, 128:256] = v_tile[tile - BLOCK:tile]
            st["k_ctx"] = kvc_ref[:, 0:128].astype(BF16)
            st["v_ctx"] = kvc_ref[:, 128:256].astype(BF16)

    def att_scores(i):
        sink = [sink_ref[layer, n] for n in range(N_Q_HEADS)]
        qb = p_ref[pl.ds(i * BLOCK, BLOCK), Q_OFF:K_OFF]
        if is_ctx:
            r = (i * BLOCK) // seq * seq
            segs = [(st["k_ext"][r:r + seq], st["v_ext"][r:r + seq], None)]
        else:
            which = 0
            if i == 0:
                which = jnp.where(has_prev, 0, 1)
            if i == nblk - 1:
                which = jnp.where(has_next, 0, 2)
            segs = [(st["k_ext"][i * BLOCK:(i + 3) * BLOCK], st["v_ext"][i * BLOCK:(i + 3) * BLOCK],
                     bias_ref[which]),
                    (st["k_ctx"], st["v_ctx"], None)]
        st["att", i] = _attention_scores(qb, segs, sink) + (segs,)

    def att_output(i):
        scores, sink_col, segs = st.pop(("att", i))
        ycat_ref[i * BLOCK:(i + 1) * BLOCK, 256:512] = _attention_output(scores, sink_col, segs).astype(BF16)

    def local_mixers():
        ycat_ref[:, 512:768] = _sgu_mixer(p_ref, sgun_ref, wsgu_ref, bsgu_ref, tile).astype(BF16)
        if is_ctx:
            zeros = jnp.zeros((POOL_HALO, 256), F32)
            for r in range(0, tile, seq):
                ycat_ref[r:r + seq, 0:256] = _conv_mixer(p_ref, r, zeros, zeros, conv_ref, seq).astype(BF16)
                ycat_ref[r:r + seq, 768:1024] = _pool_mixer(p_ref, r, zeros, zeros, wpool_ref, pscale_ref,
                                                            seq, seq, 0).astype(BF16)
            return
        else:
            ne = pl.ds(tile, POOL_HALO)
            le = pl.ds(tile - POOL_HALO, POOL_HALO)
            z_prev = jnp.where(has_prev, eprev_ref[:, 0:256], 0.0)
            z_next = jnp.where(has_next, p_ref[ne, A_GC:A_GC + 256] * p_ref[ne, A_H:A_H + 256], 0.0)
            x_prev = jnp.where(has_prev, eprev_ref[:, 256:512], 0.0)
            x_next = jnp.where(has_next, p_ref[ne, D_OFF:D_OFF + 256], 0.0)
            eprev_ref[:, 0:256] = p_ref[le, A_GC:A_GC + 256] * p_ref[le, A_H:A_H + 256]
            eprev_ref[:, 256:512] = p_ref[le, D_OFF:D_OFF + 256]
        ycat_ref[:, 0:256] = _conv_mixer(p_ref, 0, z_prev, z_next, conv_ref, tile).astype(BF16)
        ycat_ref[:, 768:1024] = _pool_mixer(p_ref, 0, x_prev, x_next, wpool_ref, pscale_ref,
                                            tile, seq, tile_start).astype(BF16)

    def out_proj():
        st["h1"] = h_ref[...] + g1 * jnp.dot(ycat_ref[...], wout_ref[...], preferred_element_type=F32)

    def norm_ff_f():
        f_ref[...] = _modulated_norm(st["h1"], nff_ref[...], mod[:, 3 * D_MODEL:4 * D_MODEL],
                                     mod[:, 4 * D_MODEL:5 * D_MODEL])

    def norm_ff():
        h1_ref[...] = st["h1"]

    sc = [functools.partial(att_scores, i) for i in range(nblk)]
    av = [functools.partial(att_output, i) for i in range(nblk)]
    if is_ctx:
        order = [norm_in, in_proj, qk_norm] + sc + [local_mixers] + av + [out_proj, norm_ff_f, norm_ff] + back
    else:
        assert nblk == 4 and len(back) == 10

        def cast_next_weights():
            for src_ref, dst_ref in zip(cast_in, cast_out):
                dst_ref[...] = src_ref[...].astype(BF16)

        order = [back[0], back[1], norm_in, in_proj, cast_next_weights, qk_norm, back[2], sc[0], sc[1], back[3], av[0],
                 sc[2], back[4], av[1], sc[3], back[5], av[2], back[6], av[3], local_mixers, out_proj, back[7], back[8],
                 norm_ff_f, back[9], norm_ff]
        first = [norm_in, in_proj, cast_next_weights, qk_norm, sc[0], sc[1], local_mixers, av[0], sc[2], av[1], sc[3],
                 av[2], av[3], out_proj, norm_ff_f, norm_ff]

        def run(pieces):
            st.clear()
            for piece in pieces:
                piece()

        pl.when(step == 0)(functools.partial(run, first))
        pl.when(step > 0)(functools.partial(run, order))
        return
    for piece in order:
        piece()


def _layer_spec(shape):
    nd = len(shape)
    return pl.BlockSpec((None,) + shape, lambda *g: (g[-1][0],) + (0,) * nd, pipeline_mode=pl.Buffered(1))


def _whole_spec(shape):
    nd = len(shape)
    return pl.BlockSpec((None,) + shape, lambda *g: (0,) * (nd + 1), pipeline_mode=pl.Buffered(1))


def _layer_call(layer, h, kv_ctx, cs, mods, mod_row, lw, wts, next_w=(), *, tile, is_ctx, seq_len=None):
    bsz, seq, _ = h.shape
    n_tiles = seq // tile
    n_total = bsz * n_tiles
    if is_ctx:
        grid = (bsz, n_tiles)
        front = lambda g: (g[0], g[1])
        back = front
        prefetch = (layer,)
    else:
        grid = (n_total + 1,)
        steps = np.arange(n_total + 1)
        ft = np.minimum(steps, n_total - 1)
        bt = np.maximum(steps - 1, 0)
        table = np.stack([ft // n_tiles, ft % n_tiles,
                          np.minimum((ft % n_tiles + 1) * (tile // BLOCK), seq // BLOCK - 1),
                          bt // n_tiles, bt % n_tiles, ft]).astype(np.int32)
        front = lambda g: (g[-2][0, g[0]], g[-2][1, g[0]])
        back = lambda g: (g[-2][3, g[0]], g[-2][4, g[0]])
        prefetch = (jnp.asarray(table), layer)
    mod_spec = lambda which: pl.BlockSpec(
        (None, None, 1, 6 * D_MODEL),
        lambda *g: (g[-1][0], which(g)[0] if mod_row is None else mod_row, 0, 0))
    in_specs = [pl.BlockSpec((None, tile, D_MODEL), lambda *g: front(g) + (0,))]
    args = [h]
    if not is_ctx:
        next_blk = lambda g: g[-2][2, g[0]]
        in_specs += [
            pl.BlockSpec((None, BLOCK, D_MODEL), lambda *g: (front(g)[0], next_blk(g), 0)),
            pl.BlockSpec((tile, 256), lambda *g: (front(g)[1], 0)),
            pl.BlockSpec((BLOCK, 256), lambda *g: (next_blk(g), 0)),
            pl.BlockSpec((None, kv_ctx.shape[1], 256), lambda *g: (front(g)[0], 0, 0)),
        ]
        args += [h, cs, cs, kv_ctx]
    in_specs.append(mod_spec(front))
    args.append(mods)
    if not is_ctx:
        in_specs.append(mod_spec(back))
        args.append(mods)
    in_specs += [
        pl.BlockSpec(memory_space=pltpu.SMEM),
        _layer_spec(lw["vecs"].shape[1:]),
        _layer_spec((N_SGU_GROUPS * CHUNK, CHUNK)),
        _layer_spec((CHUNK, 256)),
        _layer_spec((256, 256)),
    ]
    args += [lw["sink"], lw["vecs"], lw["w_sgu"], lw["b_sgu"], lw["w_pool"]]
    in_specs += [_whole_spec(w.shape[1:]) for w in wts]
    args += list(wts)
    assert not (is_ctx and next_w)
    chunk = lambda g: g[-2][5, g[0]]
    cast_shapes, cast_specs = [], []
    for w in next_w:
        rows, cols = w.shape[1] // n_total, w.shape[2]
        in_specs.append(pl.BlockSpec((None, rows, cols),
                                     lambda *g: (jnp.minimum(g[-1][0] + 1, DEPTH - 1), chunk(g), 0)))
        args.append(w)
        cast_shapes.append(jax.ShapeDtypeStruct((1,) + w.shape[1:], BF16))
        cast_specs.append(pl.BlockSpec((None, rows, cols), lambda *g: (0, chunk(g), 0)))
    tile_spec = pl.BlockSpec((None, tile, D_MODEL), lambda *g: back(g) + (0,))
    scratch = [pltpu.VMEM((tile if is_ctx else tile + EDGE, D_PROJ), F32),
               pltpu.VMEM((tile, D_MODEL), BF16),
               pltpu.VMEM((tile, D_FF), BF16),
               pltpu.VMEM((tile, D_MODEL), F32),
               pltpu.VMEM((tile, D_MODEL), BF16)]
    if is_ctx:
        out_shape = (jax.ShapeDtypeStruct(h.shape, F32), jax.ShapeDtypeStruct((bsz, seq, 256), F32))
        out_specs = (tile_spec, pl.BlockSpec((None, tile, 256), lambda *g: back(g) + (0,)))
        semantics = ("parallel", "parallel")
    else:
        out_shape = (jax.ShapeDtypeStruct(h.shape, F32),) + tuple(cast_shapes)
        out_specs = (tile_spec,) + tuple(cast_specs)
        scratch += [pltpu.VMEM((3, 4 * BLOCK, 3 * BLOCK), F32),
                    pltpu.VMEM((BLOCK, 256), BF16),
                    pltpu.VMEM((POOL_HALO, 512), F32)]
        semantics = ("arbitrary",)
    return pl.pallas_call(
        functools.partial(_layer_kernel, tile=tile, seq=seq_len or seq, n_total=n_total, is_ctx=is_ctx,
                          n_cast=len(next_w)),
        out_shape=out_shape,
        grid_spec=pltpu.PrefetchScalarGridSpec(
            num_scalar_prefetch=len(prefetch),
            grid=grid,
            in_specs=in_specs,
            out_specs=out_specs,
            scratch_shapes=scratch,
        ),
        compiler_params=pltpu.CompilerParams(dimension_semantics=semantics,
                                             vmem_limit_bytes=VMEM_LIMIT),
        name="layer_ctx" if is_ctx else "layer_lat",
    )(*prefetch, *args)


def _ctx_kv_kernel(l_ref, h_ref, mod_ref, vec_ref, w_ref, kv_ref):
    mod = mod_ref[...]
    vv = _vec_views(vec_ref)
    a = _modulated_norm(h_ref[...], vv["nmix"][...], mod[:, 0:D_MODEL], mod[:, D_MODEL:2 * D_MODEL])
    kv = jnp.dot(a, w_ref[...], preferred_element_type=F32)
    kv_ref[:, 0:128] = _head_rms(kv[:, 0:128], vv["kn"][...])
    kv_ref[:, 128:256] = kv[:, 128:256]


def _ctx_kv_call(layer, h, mods, mod_row, lw, w_in_b):
    bsz, seq, _ = h.shape
    return pl.pallas_call(
        _ctx_kv_kernel,
        out_shape=jax.ShapeDtypeStruct((bsz, seq, 256), F32),
        grid_spec=pltpu.PrefetchScalarGridSpec(
            num_scalar_prefetch=1,
            grid=(bsz, 1),
            in_specs=[
                pl.BlockSpec((None, seq, D_MODEL), lambda b, j, l: (b, 0, 0)),
                pl.BlockSpec((None, None, 1, 6 * D_MODEL), lambda b, j, l: (l[0], mod_row, 0, 0)),
                _layer_spec(lw["vecs"].shape[1:]),
                pl.BlockSpec((None, D_MODEL, 256), lambda b, j, l: (0, 0, K_OFF // 256),
                             pipeline_mode=pl.Buffered(1)),
            ],
            out_specs=pl.BlockSpec((None, seq, 256), lambda b, j, l: (b, 0, 0)),
        ),
        compiler_params=pltpu.CompilerParams(dimension_semantics=("parallel", "parallel"),
                                             vmem_limit_bytes=VMEM_LIMIT),
        name="ctx_kv",
    )(layer, h, mods, lw["vecs"], w_in_b)


def _rope_table(length):
    rows = length // GRID_W
    row = np.repeat(np.arange(rows), GRID_W).astype(np.float32)
    col = np.tile(np.arange(GRID_W), rows).astype(np.float32)
    n_freq = HEAD_DIM // 4
    inv = jnp.asarray(ROPE_THETA, F32) ** (-jnp.arange(n_freq, dtype=F32) / n_freq)
    ang_r = jnp.asarray(row)[:, None] * inv[None, :]
    ang_c = jnp.asarray(col)[:, None] * inv[None, :]
    ang = jnp.concatenate([ang_r, ang_r, ang_c, ang_c], axis=-1)
    sign = jnp.asarray(np.where(np.arange(HEAD_DIM) % 32 < 16, -1.0, 1.0), F32)
    return jnp.concatenate([jnp.tile(jnp.cos(ang), (1, 2)), jnp.tile(jnp.sin(ang) * sign, (1, 2))], axis=-1)


def _pack_vectors(conv_w, norm_mix, norm_ff, q_norm, k_norm, sgu_norm, pool_scale):
    depth = conv_w.shape[0]
    pad = lambda a: jnp.pad(a, ((0, 0), (0, 0), (0, D_MODEL - a.shape[-1])))
    row5 = jnp.concatenate([jnp.tile(q_norm, (1, N_Q_HEADS)), jnp.tile(k_norm, (1, N_KV_HEADS)),
                            jnp.zeros((depth, 128), F32), sgu_norm, pool_scale], axis=-1)
    return jnp.concatenate([pad(conv_w), norm_mix[:, None, :], norm_ff[:, None, :], row5[:, None, :],
                            jnp.zeros((depth, 2, D_MODEL), F32)], axis=1)


def kernel(x, c, ctx, c_ctx, norm_mix, norm_ff, w_ada, b_ada, w_in, w_out, conv_w, q_norm, k_norm, sink,
           sgu_norm, w_sgu, b_sgu, w_pool, pool_scale, w_ff1, w_ff2):
    bsz, seq, _ = x.shape
    ctx_len = ctx.shape[1]
    assert bsz + 1 <= MOD_ROWS and seq % LAT_TILE == 0 and ctx_len % (2 * BLOCK) == 0 and bsz % CTX_PAIR == 0

    big_w = (w_in, w_out, w_ff1, w_ff2)
    cc = jnp.concatenate([c, c_ctx[None, :], jnp.zeros((MOD_ROWS - bsz - 1, D_MODEL), F32)], axis=0)
    mods, wts = _ada_call(cc, w_ada, b_ada, big_w)
    mods = mods.reshape(DEPTH, MOD_ROWS, 1, 6 * D_MODEL)

    eye = jnp.eye(len(POOL_WINDOWS), dtype=F32)
    lw = dict(
        sink=sink,
        vecs=_pack_vectors(conv_w, norm_mix, norm_ff, q_norm, k_norm, sgu_norm, pool_scale),
        w_sgu=w_sgu.reshape(DEPTH, N_SGU_GROUPS * CHUNK, CHUNK).astype(BF16),
        b_sgu=jnp.repeat(jnp.swapaxes(b_sgu, 1, 2), HEAD_DIM, axis=2),
        w_pool=jnp.einsum("lgcd,gh->lgchd", w_pool, eye).reshape(DEPTH, 256, 256).astype(BF16),
    )
    cs = _rope_table(seq)

    h_lat, h_ctx = x, ctx.reshape(bsz // CTX_PAIR, CTX_PAIR * ctx_len, D_MODEL)
    for l in range(DEPTH):
        layer = jnp.full((1,), l, jnp.int32)
        last = l == DEPTH - 1
        if last:
            kv_ctx = _ctx_kv_call(layer, h_ctx.reshape(bsz, ctx_len, D_MODEL), mods, bsz, lw, wts[0])
        else:
            h_ctx, kv_ctx = _layer_call(layer, h_ctx, None, None, mods, bsz, lw, wts, tile=CTX_PAIR * ctx_len,
                                        is_ctx=True, seq_len=ctx_len)
            kv_ctx = kv_ctx.reshape(bsz, ctx_len, 256)
        outs = _layer_call(layer, h_lat, kv_ctx, cs, mods, None, lw, wts, () if last else big_w,
                           tile=LAT_TILE, is_ctx=False)
        h_lat, wts = outs[0], tuple(outs[1:])
    return h_lat
```

```python
import functools

import jax
import jax.numpy as jnp
import numpy as np
from jax import lax
from jax.experimental import pallas as pl
from jax.experimental.pallas import tpu as pltpu

D_MODEL = 1024
DEPTH = 4
GRID_W = 64
HEAD_DIM = 64
N_Q_HEADS = 4
N_KV_HEADS = 2
BLOCK = 128
ROPE_THETA = 10000.0
CHUNK = 128
N_SGU_GROUPS = 4
POOL_WINDOWS = (2, 4, 8, 16)
POOL_HALO = 8
EDGE = 16
D_FF = 4 * D_MODEL
EPS = 1e-6
D_PROJ = 2048

A_H, A_GB, A_GC = 0, 256, 512
Q_OFF, K_OFF, V_OFF = 768, 1024, 1152
C_U, C_V = 1280, 1536
D_OFF = 1792

NEG = -1e30
MOD_ROWS = 16
LAT_TILE = 512
CTX_PAIR = 2
VMEM_LIMIT = 58 * 1024 * 1024

F32 = jnp.float32
BF16 = jnp.bfloat16


def _rms(x, g):
    ms = jnp.mean(x * x, axis=-1, keepdims=True)
    return x * lax.rsqrt(ms + EPS) * g


def _lane_group(shape, width):
    return lax.broadcasted_iota(jnp.int32, shape, len(shape) - 1) // width


def _head_rms(x, g):
    n = x.shape[-1]
    r = lax.broadcasted_iota(jnp.int32, (n, n), 0) // HEAD_DIM
    c = lax.broadcasted_iota(jnp.int32, (n, n), 1) // HEAD_DIM
    ones = jnp.where(r == c, 1.0, 0.0).astype(BF16)
    x2 = x * x
    hi = x2.astype(BF16)
    lo = (x2 - hi.astype(F32)).astype(BF16)
    ms = (jnp.dot(hi, ones, preferred_element_type=F32)
          + jnp.dot(lo, ones, preferred_element_type=F32)) * (1.0 / HEAD_DIM)
    return x * lax.rsqrt(ms + EPS) * g


def _swap_lanes(x, width):
    lane = lax.broadcasted_iota(jnp.int32, x.shape, x.ndim - 1)
    return jnp.where(lane % (2 * width) < width, pltpu.roll(x, 128 - width, x.ndim - 1),
                     pltpu.roll(x, width, x.ndim - 1))


def _rope(x, cs):
    cos, sin = cs[:, 0:128], cs[:, 128:256]
    parts = [x[:, s:s + 128] * cos + _swap_lanes(x[:, s:s + 128], 16) * sin for s in range(0, x.shape[-1], 128)]
    return parts[0] if len(parts) == 1 else jnp.concatenate(parts, axis=-1)


def _modulated_norm(x, g, shift, scale):
    return (_rms(x, g) * (1.0 + scale) + shift).astype(BF16)


def _ada_kernel(cc_ref, w_ref, b_ref, *refs):
    n_cast = (len(refs) - 1) // 2
    o_ref = refs[n_cast]
    cc = cc_ref[...]
    s = cc * jax.nn.sigmoid(cc)
    o_ref[...] = jnp.dot(s.astype(BF16), w_ref[...].astype(BF16), preferred_element_type=F32) + b_ref[...]
    for src_ref, dst_ref in zip(refs[:n_cast], refs[n_cast + 1:]):
        dst_ref[...] = src_ref[...].astype(BF16)


def _ada_call(cc, w_ada, b_ada, big_w):
    nblk = 2
    wb = 6 * D_MODEL // nblk
    steps = DEPTH * nblk
    chunk = lambda l, n: (0, l * nblk + n, 0)
    cast_in = [pl.BlockSpec((None, w.shape[1] // steps, w.shape[2]), chunk) for w in big_w]
    outs = pl.pallas_call(
        _ada_kernel,
        out_shape=(jax.ShapeDtypeStruct((DEPTH, MOD_ROWS, 6 * D_MODEL), F32),)
        + tuple(jax.ShapeDtypeStruct((1,) + w.shape[1:], BF16) for w in big_w),
        grid=(DEPTH, nblk),
        in_specs=[
            pl.BlockSpec((MOD_ROWS, D_MODEL), lambda l, n: (0, 0)),
            pl.BlockSpec((None, D_MODEL, wb), lambda l, n: (l, 0, n)),
            pl.BlockSpec((None, 1, wb), lambda l, n: (l, 0, n)),
        ] + cast_in,
        out_specs=(pl.BlockSpec((None, MOD_ROWS, wb), lambda l, n: (l, 0, n)),) + tuple(cast_in),
        compiler_params=pltpu.CompilerParams(dimension_semantics=("arbitrary", "arbitrary"),
                                             vmem_limit_bytes=VMEM_LIMIT),
        name="ada_mod",
    )(cc, w_ada, b_ada.reshape(DEPTH, 1, 6 * D_MODEL), *big_w)
    return outs[0], tuple(outs[1:])


def _shift_rows(x, k):
    n = x.shape[0]
    return pltpu.roll(x, k % n, 0)


def _conv_mixer(p_ref, r0, halo_prev, halo_next, conv_ref, tile):
    rows = pl.ds(r0, tile)
    z = p_ref[rows, A_GC:A_GC + 256] * p_ref[rows, A_H:A_H + 256]
    z_ext = jnp.concatenate([halo_prev, z, halo_next], axis=0)
    z_prev = _shift_rows(z_ext, 1)[POOL_HALO:POOL_HALO + tile]
    z_next = _shift_rows(z_ext, -1)[POOL_HALO:POOL_HALO + tile]
    cw = conv_ref[...]
    y = cw[0:1] * z_prev + cw[1:2] * z + cw[2:3] * z_next
    return p_ref[rows, A_GB:A_GB + 256] * y


def _pool_mixer(p_ref, r0, halo_prev, halo_next, wpool_ref, pscale_ref, tile, seq, tile_start):
    x = p_ref[pl.ds(r0, tile), D_OFF:D_OFF + 256]
    x_ext = jnp.concatenate([halo_prev, x, halo_next], axis=0)
    s2 = _shift_rows(x_ext, 1) + x_ext
    s4 = _shift_rows(s2, 1) + _shift_rows(s2, -1)
    s8 = _shift_rows(s4, 2) + _shift_rows(s4, -2)
    s16 = _shift_rows(s8, 4) + _shift_rows(s8, -4)
    grp = _lane_group(x_ext.shape, HEAD_DIM)
    s = jnp.where(grp == 0, s2, jnp.where(grp == 1, s4, jnp.where(grp == 2, s8, s16)))
    s = s[POOL_HALO:POOL_HALO + tile]
    t = tile_start + lax.broadcasted_iota(jnp.int32, (tile, 256), 0)
    half = jnp.left_shift(1, _lane_group((tile, 256), HEAD_DIM))
    cnt = jnp.minimum(t + half, seq) - jnp.maximum(t - half, 0)
    d = s / cnt.astype(F32) - x
    y = jnp.dot(d.astype(BF16), wpool_ref[...], preferred_element_type=F32)
    return y * pscale_ref[...]


def _pair_halves(a, b, half):
    low = lax.broadcasted_iota(jnp.int32, a.shape, 1) < HEAD_DIM
    if half == 0:
        return jnp.where(low, a, pltpu.roll(b, HEAD_DIM, 1))
    return jnp.where(low, pltpu.roll(a, HEAD_DIM, 1), b)


def _sgu_mixer(p_ref, sgun_ref, wsgu_ref, bsgu_ref, tile):
    rows = pl.ds(0, tile)
    nchunk = tile // CHUNK
    assert nchunk % 2 == 0
    vn = _rms(p_ref[rows, C_V:C_V + 256], sgun_ref[...])
    x = [[vn[c * CHUNK:(c + 1) * CHUNK, t * 128:(t + 1) * 128] for t in range(2)] for c in range(nchunk)]
    zg = []
    for g in range(N_SGU_GROUPS):
        t, half = divmod(g, 2)
        rhs = jnp.concatenate([_pair_halves(x[c][t], x[c + 1][t], half) for c in range(0, nchunk, 2)], axis=1)
        w_g = wsgu_ref[g * CHUNK:(g + 1) * CHUNK, :]
        zg.append(jnp.dot(w_g, rhs.astype(BF16), preferred_element_type=F32))
    bias = bsgu_ref[...]
    outs = []
    for c in range(nchunk):
        ct, half = divmod(c, 2)
        tiles = [_pair_halves(zg[2 * t][:, ct * 128:(ct + 1) * 128], zg[2 * t + 1][:, ct * 128:(ct + 1) * 128], half)
                 for t in range(2)]
        outs.append(jnp.concatenate(tiles, axis=1) + bias)
    return p_ref[rows, C_U:C_U + 256] * jnp.concatenate(outs, axis=0)


def _attention_scores(qb, segs, sink):
    low = lax.broadcasted_iota(jnp.int32, (BLOCK, 128), 1) < HEAD_DIM
    t0, t1 = qb[:, 0:128], qb[:, 128:256]
    rows = [jnp.where(low, t0, 0.0), jnp.where(low, pltpu.roll(t0, HEAD_DIM, 1), 0.0),
            jnp.where(low, 0.0, pltpu.roll(t1, HEAD_DIM, 1)), jnp.where(low, 0.0, t1)]
    q4 = jnp.concatenate(rows, axis=0).astype(BF16)
    rb = lax.broadcasted_iota(jnp.int32, (4 * BLOCK, 1), 0) // BLOCK
    sink_col = jnp.where(rb == 0, sink[0], jnp.where(rb == 1, sink[1], jnp.where(rb == 2, sink[2], sink[3])))
    scores = []
    for k, _, bias in segs:
        s = lax.dot_general(q4, k, (((1,), (1,)), ((), ())), preferred_element_type=F32)
        scores.append(s if bias is None else s + bias)
    return scores, sink_col


def _attention_output(scores, sink_col, segs):
    low = lax.broadcasted_iota(jnp.int32, (BLOCK, 128), 1) < HEAD_DIM
    m = sink_col
    for s in scores:
        m = jnp.maximum(m, jnp.max(s, axis=-1, keepdims=True))
    denom = jnp.exp(sink_col - m)
    acc = None
    for s, (_, v, _) in zip(scores, segs):
        pr = jnp.exp(s - m)
        denom = denom + jnp.sum(pr, axis=-1, keepdims=True)
        o = jnp.dot(pr.astype(BF16), v, preferred_element_type=F32)
        acc = o if acc is None else acc + o
    acc = acc / denom
    a = [acc[i * BLOCK:(i + 1) * BLOCK] for i in range(4)]
    out0 = jnp.where(low, a[0], pltpu.roll(a[1], HEAD_DIM, 1))
    out1 = jnp.where(low, pltpu.roll(a[2], HEAD_DIM, 1), a[3])
    return jnp.concatenate([out0, out1], axis=-1)


def _vec_views(vec_ref):
    return dict(conv=vec_ref.at[0:3, 0:256], nmix=vec_ref.at[3:4, :], nff=vec_ref.at[4:5, :],
                qn=vec_ref.at[5:6, 0:256], kn=vec_ref.at[5:6, 256:384], sgun=vec_ref.at[5:6, 512:768],
                pscale=vec_ref.at[5:6, 768:1024])


def _layer_kernel(*refs, tile, seq, n_total, is_ctx, n_cast, mod_row):
    if is_ctx:
        (l_ref, h_ref, mods_ref, sink_ref, vec_ref, wsgu_ref, bsgu_ref, wpool_ref, win_ref, wout_ref, w1_ref, w2_ref,
         o_ref, kv_ref, p_ref, ycat_ref, hid_ref, h1_ref, f_ref) = refs
        mod_ref = modb_ref = mods_ref.at[mod_row:mod_row + 1]
        j = pl.program_id(1)
    else:
        (sched_ref, l_ref, h_ref, hn_ref, cs_ref, csn_ref, kvc_ref, mods_ref, sink_ref, vec_ref, wsgu_ref,
         bsgu_ref, wpool_ref, win_ref, wout_ref, w1_ref, w2_ref) = refs[:17]
        cast_in = refs[17:17 + n_cast]
        o_ref = refs[17 + n_cast]
        cast_out = refs[18 + n_cast:18 + 2 * n_cast]
        (p_ref, ycat_ref, hid_ref, h1_ref, f_ref, bias_ref, kvprev_ref, eprev_ref) = refs[18 + 2 * n_cast:]
        step = pl.program_id(0)
        mod_ref = mods_ref.at[pl.ds(sched_ref[0, step], 1)]
        modb_ref = mods_ref.at[pl.ds(sched_ref[3, step], 1)]
    vv = _vec_views(vec_ref)
    nmix_ref, nff_ref, qn_ref, kn_ref = vv["nmix"], vv["nff"], vv["qn"], vv["kn"]
    conv_ref, sgun_ref, pscale_ref = vv["conv"], vv["sgun"], vv["pscale"]
    if not is_ctx:

        @pl.when(step == 0)
        def _():
            kvprev_ref[...] = jnp.zeros_like(kvprev_ref)
            eprev_ref[...] = jnp.zeros_like(eprev_ref)
            r = lax.broadcasted_iota(jnp.int32, (4 * BLOCK, 3 * BLOCK), 0) % BLOCK
            col = lax.broadcasted_iota(jnp.int32, (4 * BLOCK, 3 * BLOCK), 1)
            seg = col // BLOCK
            jj = col % BLOCK
            band_prev = (seg == 0) & (jj >= r)
            band_next = (seg == 2) & (jj <= r)
            bias_ref[0] = jnp.where((seg == 1) | band_prev | band_next, 0.0, NEG)
            bias_ref[1] = jnp.where((seg == 1) | band_next, 0.0, NEG)
            bias_ref[2] = jnp.where((seg == 1) | band_prev, 0.0, NEG)

        j = sched_ref[1, step]
        has_prev = j > 0
        has_next = j < seq // tile - 1
    layer = l_ref[0]
    tile_start = j * tile
    mod = mod_ref[...]
    sh1, sc1, g1 = mod[:, 0:D_MODEL], mod[:, D_MODEL:2 * D_MODEL], mod[:, 2 * D_MODEL:3 * D_MODEL]
    rows = pl.ds(0, tile)
    nblk = tile // BLOCK
    st = {}

    ff1_cols = 512

    def ff1(c):
        u = jnp.dot(f_ref[...], w1_ref[:, c:c + ff1_cols], preferred_element_type=F32)
        u = jnp.maximum(u, 0.0)
        hid_ref[:, c:c + ff1_cols] = (u * u).astype(BF16)

    half = D_MODEL // 2

    def ff2(c):
        g2 = modb_ref[:, 5 * D_MODEL + c:5 * D_MODEL + c + half]
        y = jnp.dot(hid_ref[...], w2_ref[:, c:c + half], preferred_element_type=F32)
        o_ref[:, c:c + half] = h1_ref[:, c:c + half] + g2 * y

    back = [functools.partial(ff1, c) for c in range(0, D_FF, ff1_cols)]
    back += [functools.partial(ff2, c) for c in range(0, D_MODEL, half)]

    def norm_in():
        nmix = nmix_ref[...]
        a_main = _modulated_norm(h_ref[...], nmix, sh1, sc1)
        if is_ctx:
            st["a_ext"] = a_main
        else:
            a_next = _modulated_norm(hn_ref[...], nmix, sh1, sc1)
            st["a_ext"] = jnp.concatenate([a_main, a_next[0:EDGE]], axis=0)
            st["a_next"] = a_next

    def in_proj():
        p_ref[...] = jnp.dot(st["a_ext"], win_ref[...], preferred_element_type=F32)
        if not is_ctx:
            st["kv_next"] = jnp.dot(st["a_next"], win_ref[:, K_OFF:K_OFF + 256], preferred_element_type=F32)

    def qk_norm():
        q = _head_rms(p_ref[rows, Q_OFF:K_OFF], qn_ref[...])
        k = _head_rms(p_ref[rows, K_OFF:V_OFF], kn_ref[...])
        if not is_ctx:
            q = _rope(q, cs_ref[...])
            k = _rope(k, cs_ref[...])
        p_ref[rows, Q_OFF:K_OFF] = q * (HEAD_DIM ** -0.5)
        k_tile = k.astype(BF16)
        v_tile = p_ref[rows, V_OFF:V_OFF + 128].astype(BF16)
        if is_ctx:
            kv_ref[:, 0:128] = k.astype(BF16)
            kv_ref[:, 128:256] = p_ref[rows, V_OFF:V_OFF + 128].astype(BF16)
            st["k_ext"], st["v_ext"] = k_tile, v_tile
        else:
            kv_next = st["kv_next"]
            k_next = _rope(_head_rms(kv_next[:, 0:128], kn_ref[...]), csn_ref[...]).astype(BF16)
            v_next = kv_next[:, 128:256].astype(BF16)
            st["k_ext"] = jnp.concatenate([kvprev_ref[:, 0:128], k_tile, k_next], axis=0)
            st["v_ext"] = jnp.concatenate([kvprev_ref[:, 128:256], v_tile, v_next], axis=0)
            kvprev_ref[:, 0:128] = k_tile[tile - BLOCK:tile]
            kvprev_ref[:, 128:256] = v_tile[tile - BLOCK:tile]
            st["k_ctx"] = kvc_ref[:, 0:128].astype(BF16)
            st["v_ctx"] = kvc_ref[:, 128:256].astype(BF16)

    def att_scores(i):
        sink = [sink_ref[layer, n] for n in range(N_Q_HEADS)]
        qb = p_ref[pl.ds(i * BLOCK, BLOCK), Q_OFF:K_OFF]
        if is_ctx:
            r = (i * BLOCK) // seq * seq
            segs = [(st["k_ext"][r:r + seq], st["v_ext"][r:r + seq], None)]
        else:
            which = 0
            if i == 0:
                which = jnp.where(has_prev, 0, 1)
            if i == nblk - 1:
                which = jnp.where(has_next, 0, 2)
            segs = [(st["k_ext"][i * BLOCK:(i + 3) * BLOCK], st["v_ext"][i * BLOCK:(i + 3) * BLOCK],
                     bias_ref[which]),
                    (st["k_ctx"], st["v_ctx"], None)]
        st["att", i] = _attention_scores(qb, segs, sink) + (segs,)

    def att_output(i):
        scores, sink_col, segs = st.pop(("att", i))
        ycat_ref[i * BLOCK:(i + 1) * BLOCK, 256:512] = _attention_output(scores, sink_col, segs).astype(BF16)

    def local_mixers():
        ycat_ref[:, 512:768] = _sgu_mixer(p_ref, sgun_ref, wsgu_ref, bsgu_ref, tile).astype(BF16)
        if is_ctx:
            zeros = jnp.zeros((POOL_HALO, 256), F32)
            for r in range(0, tile, seq):
                ycat_ref[r:r + seq, 0:256] = _conv_mixer(p_ref, r, zeros, zeros, conv_ref, seq).astype(BF16)
                ycat_ref[r:r + seq, 768:1024] = _pool_mixer(p_ref, r, zeros, zeros, wpool_ref, pscale_ref,
                                                            seq, seq, 0).astype(BF16)
            return
        else:
            ne = pl.ds(tile, POOL_HALO)
            le = pl.ds(tile - POOL_HALO, POOL_HALO)
            z_prev = jnp.where(has_prev, eprev_ref[:, 0:256], 0.0)
            z_next = jnp.where(has_next, p_ref[ne, A_GC:A_GC + 256] * p_ref[ne, A_H:A_H + 256], 0.0)
            x_prev = jnp.where(has_prev, eprev_ref[:, 256:512], 0.0)
            x_next = jnp.where(has_next, p_ref[ne, D_OFF:D_OFF + 256], 0.0)
            eprev_ref[:, 0:256] = p_ref[le, A_GC:A_GC + 256] * p_ref[le, A_H:A_H + 256]
            eprev_ref[:, 256:512] = p_ref[le, D_OFF:D_OFF + 256]
        ycat_ref[:, 0:256] = _conv_mixer(p_ref, 0, z_prev, z_next, conv_ref, tile).astype(BF16)
        ycat_ref[:, 768:1024] = _pool_mixer(p_ref, 0, x_prev, x_next, wpool_ref, pscale_ref,
                                            tile, seq, tile_start).astype(BF16)

    def out_proj():
        st["h1"] = h_ref[...] + g1 * jnp.dot(ycat_ref[...], wout_ref[...], preferred_element_type=F32)

    def norm_ff_f():
        f_ref[...] = _modulated_norm(st["h1"], nff_ref[...], mod[:, 3 * D_MODEL:4 * D_MODEL],
                                     mod[:, 4 * D_MODEL:5 * D_MODEL])

    def norm_ff():
        h1_ref[...] = st["h1"]

    sc = [functools.partial(att_scores, i) for i in range(nblk)]
    av = [functools.partial(att_output, i) for i in range(nblk)]
    if is_ctx:
        order = [norm_in, in_proj, qk_norm] + sc + [local_mixers] + av + [out_proj, norm_ff_f, norm_ff] + back
    else:
        assert nblk == 4 and len(back) == 10

        def cast_next_weights():
            for src_ref, dst_ref in zip(cast_in, cast_out):
                dst_ref[...] = src_ref[...].astype(BF16)

        order = [back[0], back[1], norm_in, in_proj, cast_next_weights, qk_norm, back[2], sc[0], sc[1], back[3], av[0],
                 sc[2], back[4], av[1], sc[3], back[5], av[2], back[6], av[3], local_mixers, out_proj, back[7], back[8],
                 norm_ff_f, back[9], norm_ff]
        first = [norm_in, in_proj, cast_next_weights, qk_norm, sc[0], sc[1], local_mixers, av[0], sc[2], av[1], sc[3],
                 av[2], av[3], out_proj, norm_ff_f, norm_ff]

        def run(pieces):
            st.clear()
            for piece in pieces:
                piece()

        pl.when(step == 0)(functools.partial(run, first))
        pl.when(step > 0)(functools.partial(run, order))
        return
    for piece in order:
        piece()


def _layer_spec(shape):
    nd = len(shape)
    return pl.BlockSpec((None,) + shape, lambda *g: (g[-1][0],) + (0,) * nd, pipeline_mode=pl.Buffered(1))


def _whole_spec(shape):
    nd = len(shape)
    return pl.BlockSpec((None,) + shape, lambda *g: (0,) * (nd + 1), pipeline_mode=pl.Buffered(1))


def _layer_call(layer, h, kv_ctx, cs, mods, mod_row, lw, wts, next_w=(), *, tile, is_ctx, seq_len=None):
    bsz, seq, _ = h.shape
    n_tiles = seq // tile
    n_total = bsz * n_tiles
    if is_ctx:
        grid = (bsz, n_tiles)
        front = lambda g: (g[0], g[1])
        back = front
        prefetch = (layer,)
    else:
        grid = (n_total + 1,)
        steps = np.arange(n_total + 1)
        ft = np.minimum(steps, n_total - 1)
        bt = np.maximum(steps - 1, 0)
        table = np.stack([ft // n_tiles, ft % n_tiles,
                          np.minimum((ft % n_tiles + 1) * (tile // BLOCK), seq // BLOCK - 1),
                          bt // n_tiles, bt % n_tiles, ft]).astype(np.int32)
        front = lambda g: (g[-2][0, g[0]], g[-2][1, g[0]])
        back = lambda g: (g[-2][3, g[0]], g[-2][4, g[0]])
        prefetch = (jnp.asarray(table), layer)
    in_specs = [pl.BlockSpec((None, tile, D_MODEL), lambda *g: front(g) + (0,))]
    args = [h]
    if not is_ctx:
        next_blk = lambda g: g[-2][2, g[0]]
        in_specs += [
            pl.BlockSpec((None, BLOCK, D_MODEL), lambda *g: (front(g)[0], next_blk(g), 0)),
            pl.BlockSpec((tile, 256), lambda *g: (front(g)[1], 0)),
            pl.BlockSpec((BLOCK, 256), lambda *g: (next_blk(g), 0)),
            pl.BlockSpec((None, kv_ctx.shape[1], 256), lambda *g: (front(g)[0], 0, 0)),
        ]
        args += [h, cs, cs, kv_ctx]
    in_specs += [
        _layer_spec(mods.shape[1:]),
        pl.BlockSpec(memory_space=pltpu.SMEM),
        _layer_spec(lw["vecs"].shape[1:]),
        _layer_spec((N_SGU_GROUPS * CHUNK, CHUNK)),
        _layer_spec((CHUNK, 256)),
        _layer_spec((256, 256)),
    ]
    args += [mods, lw["sink"], lw["vecs"], lw["w_sgu"], lw["b_sgu"], lw["w_pool"]]
    in_specs += [_whole_spec(w.shape[1:]) for w in wts]
    args += list(wts)
    assert not (is_ctx and next_w)
    chunk = lambda g: g[-2][5, g[0]]
    cast_shapes, cast_specs = [], []
    for w in next_w:
        rows, cols = w.shape[1] // n_total, w.shape[2]
        in_specs.append(pl.BlockSpec((None, rows, cols),
                                     lambda *g: (jnp.minimum(g[-1][0] + 1, DEPTH - 1), chunk(g), 0)))
        args.append(w)
        cast_shapes.append(jax.ShapeDtypeStruct((1,) + w.shape[1:], BF16))
        cast_specs.append(pl.BlockSpec((None, rows, cols), lambda *g: (0, chunk(g), 0)))
    tile_spec = pl.BlockSpec((None, tile, D_MODEL), lambda *g: back(g) + (0,))
    scratch = [pltpu.VMEM((tile if is_ctx else tile + EDGE, D_PROJ), F32),
               pltpu.VMEM((tile, D_MODEL), BF16),
               pltpu.VMEM((tile, D_FF), BF16),
               pltpu.VMEM((tile, D_MODEL), F32),
               pltpu.VMEM((tile, D_MODEL), BF16)]
    if is_ctx:
        out_shape = (jax.ShapeDtypeStruct(h.shape, F32), jax.ShapeDtypeStruct((bsz, seq, 256), BF16))
        out_specs = (tile_spec, pl.BlockSpec((None, tile, 256), lambda *g: back(g) + (0,)))
        semantics = ("parallel", "parallel")
    else:
        out_shape = (jax.ShapeDtypeStruct(h.shape, F32),) + tuple(cast_shapes)
        out_specs = (tile_spec,) + tuple(cast_specs)
        scratch += [pltpu.VMEM((3, 4 * BLOCK, 3 * BLOCK), F32),
                    pltpu.VMEM((BLOCK, 256), BF16),
                    pltpu.VMEM((POOL_HALO, 512), F32)]
        semantics = ("arbitrary",)
    return pl.pallas_call(
        functools.partial(_layer_kernel, tile=tile, seq=seq_len or seq, n_total=n_total, is_ctx=is_ctx,
                          n_cast=len(next_w), mod_row=mod_row),
        out_shape=out_shape,
        grid_spec=pltpu.PrefetchScalarGridSpec(
            num_scalar_prefetch=len(prefetch),
            grid=grid,
            in_specs=in_specs,
            out_specs=out_specs,
            scratch_shapes=scratch,
        ),
        compiler_params=pltpu.CompilerParams(dimension_semantics=semantics,
                                             vmem_limit_bytes=VMEM_LIMIT),
        name="layer_ctx" if is_ctx else "layer_lat",
    )(*prefetch, *args)


def _ctx_kv_kernel(l_ref, h_ref, mods_ref, vec_ref, w_ref, kv_ref, *, mod_row):
    mod = mods_ref[mod_row:mod_row + 1, :]
    vv = _vec_views(vec_ref)
    a = _modulated_norm(h_ref[...], vv["nmix"][...], mod[:, 0:D_MODEL], mod[:, D_MODEL:2 * D_MODEL])
    kv = jnp.dot(a, w_ref[...], preferred_element_type=F32)
    kv_ref[:, 0:128] = _head_rms(kv[:, 0:128], vv["kn"][...]).astype(BF16)
    kv_ref[:, 128:256] = kv[:, 128:256].astype(BF16)


def _ctx_kv_call(layer, h, mods, mod_row, lw, w_in_b):
    bsz, seq, _ = h.shape
    return pl.pallas_call(
        functools.partial(_ctx_kv_kernel, mod_row=mod_row),
        out_shape=jax.ShapeDtypeStruct((bsz, seq, 256), BF16),
        grid_spec=pltpu.PrefetchScalarGridSpec(
            num_scalar_prefetch=1,
            grid=(bsz, 1),
            in_specs=[
                pl.BlockSpec((None, seq, D_MODEL), lambda b, j, l: (b, 0, 0)),
                _layer_spec(mods.shape[1:]),
                _layer_spec(lw["vecs"].shape[1:]),
                pl.BlockSpec((None, D_MODEL, 256), lambda b, j, l: (0, 0, K_OFF // 256),
                             pipeline_mode=pl.Buffered(1)),
            ],
            out_specs=pl.BlockSpec((None, seq, 256), lambda b, j, l: (b, 0, 0)),
        ),
        compiler_params=pltpu.CompilerParams(dimension_semantics=("parallel", "parallel"),
                                             vmem_limit_bytes=VMEM_LIMIT),
        name="ctx_kv",
    )(layer, h, mods, lw["vecs"], w_in_b)


def _rope_table(length):
    rows = length // GRID_W
    row = np.repeat(np.arange(rows), GRID_W).astype(np.float32)
    col = np.tile(np.arange(GRID_W), rows).astype(np.float32)
    n_freq = HEAD_DIM // 4
    inv = jnp.asarray(ROPE_THETA, F32) ** (-jnp.arange(n_freq, dtype=F32) / n_freq)
    ang_r = jnp.asarray(row)[:, None] * inv[None, :]
    ang_c = jnp.asarray(col)[:, None] * inv[None, :]
    ang = jnp.concatenate([ang_r, ang_r, ang_c, ang_c], axis=-1)
    sign = jnp.asarray(np.where(np.arange(HEAD_DIM) % 32 < 16, -1.0, 1.0), F32)
    return jnp.concatenate([jnp.tile(jnp.cos(ang), (1, 2)), jnp.tile(jnp.sin(ang) * sign, (1, 2))], axis=-1)


def _pack_vectors(conv_w, norm_mix, norm_ff, q_norm, k_norm, sgu_norm, pool_scale):
    depth = conv_w.shape[0]
    pad = lambda a: jnp.pad(a, ((0, 0), (0, 0), (0, D_MODEL - a.shape[-1])))
    row5 = jnp.concatenate([jnp.tile(q_norm, (1, N_Q_HEADS)), jnp.tile(k_norm, (1, N_KV_HEADS)),
                            jnp.zeros((depth, 128), F32), sgu_norm, pool_scale], axis=-1)
    return jnp.concatenate([pad(conv_w), norm_mix[:, None, :], norm_ff[:, None, :], row5[:, None, :],
                            jnp.zeros((depth, 2, D_MODEL), F32)], axis=1)


def kernel(x, c, ctx, c_ctx, norm_mix, norm_ff, w_ada, b_ada, w_in, w_out, conv_w, q_norm, k_norm, sink,
           sgu_norm, w_sgu, b_sgu, w_pool, pool_scale, w_ff1, w_ff2):
    bsz, seq, _ = x.shape
    ctx_len = ctx.shape[1]
    assert bsz + 1 <= MOD_ROWS and seq % LAT_TILE == 0 and ctx_len % (2 * BLOCK) == 0 and bsz % CTX_PAIR == 0

    big_w = (w_in, w_out, w_ff1, w_ff2)
    cc = jnp.concatenate([c, c_ctx[None, :], jnp.zeros((MOD_ROWS - bsz - 1, D_MODEL), F32)], axis=0)
    mods, wts = _ada_call(cc, w_ada, b_ada, big_w)

    eye = jnp.eye(len(POOL_WINDOWS), dtype=F32)
    lw = dict(
        sink=sink,
        vecs=_pack_vectors(conv_w, norm_mix, norm_ff, q_norm, k_norm, sgu_norm, pool_scale),
        w_sgu=w_sgu.reshape(DEPTH, N_SGU_GROUPS * CHUNK, CHUNK).astype(BF16),
        b_sgu=jnp.repeat(jnp.swapaxes(b_sgu, 1, 2), HEAD_DIM, axis=2),
        w_pool=jnp.einsum("lgcd,gh->lgchd", w_pool, eye).reshape(DEPTH, 256, 256).astype(BF16),
    )
    cs = _rope_table(seq)

    h_lat, h_ctx = x, ctx.reshape(bsz // CTX_PAIR, CTX_PAIR * ctx_len, D_MODEL)
    for l in range(DEPTH):
        layer = jnp.full((1,), l, jnp.int32)
        last = l == DEPTH - 1
        if last:
            kv_ctx = _ctx_kv_call(layer, h_ctx.reshape(2, -1, D_MODEL), mods, bsz, lw, wts[0])
            kv_ctx = kv_ctx.reshape(bsz, ctx_len, 256)
        else:
            h_ctx, kv_ctx = _layer_call(layer, h_ctx, None, None, mods, bsz, lw, wts, tile=CTX_PAIR * ctx_len,
                                        is_ctx=True, seq_len=ctx_len)
            kv_ctx = kv_ctx.reshape(bsz, ctx_len, 256)
        outs = _layer_call(layer, h_lat, kv_ctx, cs, mods, None, lw, wts, () if last else big_w,
                           tile=LAT_TILE, is_ctx=False)
        h_lat, wts = outs[0], tuple(outs[1:])
    return h_lat
```
